```python
import numpy as np
import jax
import jax.numpy as jnp
from jax import lax

D_MODEL = 1024
BATCH = 8
SEQ = 4096
DEPTH = 4

CHUNK = 64
N_MIXERS = 3
N_A = (DEPTH + 2) // 3
N_B = (DEPTH + 1) // 3
N_C = DEPTH // 3
EPS = 1e-6
ROPE_THETA = 500000.0
GM_BLOCK = 128
GM_HID = 2 * D_MODEL
GM_GROUPS = 8
SSD_D_INNER = 2 * D_MODEL
SSD_HEAD_DIM = 64
SSD_HEADS = SSD_D_INNER // SSD_HEAD_DIM
SSD_GROUPS = 8
SSD_STATE = 128
SSD_CONV = 4
SSD_CHUNK = 128
SSD_CONV_CH = SSD_D_INNER + 2 * SSD_GROUPS * SSD_STATE
SSD_IN = SSD_D_INNER + SSD_CONV_CH + SSD_HEADS
DSA_HEADS = 16
DSA_KV_HEADS = 4
DSA_HEAD_DIM = 64
DSA_IDX_HEADS = 8
DSA_IDX_DIM = 64
DSA_TOPK_MAX = 256
DSA_QBLOCK = 128
DSA_Q_W = DSA_HEADS * DSA_HEAD_DIM
DSA_KV_W = DSA_KV_HEADS * DSA_HEAD_DIM
DSA_QI_W = DSA_IDX_HEADS * DSA_IDX_DIM
DSA_IN = DSA_Q_W + 2 * DSA_KV_W + DSA_QI_W + DSA_IDX_DIM + DSA_IDX_HEADS
MEM_LEN = 256
XA_HEADS = 4
XA_HEAD_DIM = 128
XA_W = XA_HEADS * XA_HEAD_DIM
FFN_HID = 4 * D_MODEL

kernel_name = 'hybrid_chunk_causal_encoder'


def rms_norm(x, gain):
    x32 = x.astype(jnp.float32)
    y = x32 * lax.rsqrt(jnp.mean(x32 * x32, axis=-1, keepdims=True) + EPS)
    return y.astype(x.dtype) * gain


def rope_partial(x, positions):
    rot = x.shape[-1] // 4
    half = rot // 2
    inv_freq = ROPE_THETA ** (-jnp.arange(half, dtype=jnp.float32) / half)
    ang = positions.astype(jnp.float32)[..., None] * inv_freq
    cos = jnp.cos(ang)[:, :, None, :]
    sin = jnp.sin(ang)[:, :, None, :]
    xr = x[..., :rot].astype(jnp.float32)
    x1, x2 = xr[..., :half], xr[..., half:]
    r = jnp.concatenate([x1 * cos - x2 * sin, x2 * cos + x1 * sin], axis=-1).astype(x.dtype)
    return jnp.concatenate([r, x[..., rot:]], axis=-1)


def gmlp_mixer(h, w_in, v_norm, w_s, b_s, w_out):
    B_, S_, _ = h.shape
    uv = jax.nn.gelu(h @ w_in)
    u, v = uv[..., :GM_HID], uv[..., GM_HID:]
    v = rms_norm(v, v_norm)
    nb = S_ // GM_BLOCK
    v = v.reshape(B_, nb, GM_BLOCK, GM_GROUPS, GM_HID // GM_GROUPS)
    t = jnp.arange(GM_BLOCK)
    mask = (t[None, :] // CHUNK) <= (t[:, None] // CHUNK)
    ws = jnp.where(mask[None], w_s, 0)
    v = jnp.einsum('gts,bnsgc->bntgc', ws, v) + jnp.transpose(b_s)[None, None, :, :, None]
    v = v.reshape(B_, S_, GM_HID)
    return (u * v) @ w_out


def causal_dwconv(x, w, b):
    K = w.shape[0]
    y = lax.conv_general_dilated(x, w[:, None, :], window_strides=(1,), padding=[(K - 1, 0)],
                                 dimension_numbers=('NWC', 'WIO', 'NWC'),
                                 feature_group_count=x.shape[-1])
    return y + b


def ssd_mixer(h, w_in, conv_w, conv_b, dt_bias, a_log, d_skip, out_norm, w_out):
    B_, S_, _ = h.shape
    G, N, H, P, Q = SSD_GROUPS, SSD_STATE, SSD_HEADS, SSD_HEAD_DIM, SSD_CHUNK
    R = H // G
    nc = S_ // Q
    f32 = jnp.float32
    zxbcdt = h @ w_in
    z = zxbcdt[..., :SSD_D_INNER]
    xbc = zxbcdt[..., SSD_D_INNER:SSD_D_INNER + SSD_CONV_CH]
    dt = zxbcdt[..., SSD_D_INNER + SSD_CONV_CH:]
    xbc = jax.nn.silu(causal_dwconv(xbc, conv_w, conv_b))
    xs = xbc[..., :SSD_D_INNER]
    Bm = xbc[..., SSD_D_INNER:SSD_D_INNER + G * N]
    Cm = xbc[..., SSD_D_INNER + G * N:]
    dt = jax.nn.softplus((dt + dt_bias).astype(f32))
    A = -jnp.exp(a_log.astype(f32))
    xc = xs.reshape(B_, nc, Q, G, R, P).astype(f32)
    Bc = Bm.reshape(B_, nc, Q, G, N).astype(f32)
    Cc = Cm.reshape(B_, nc, Q, G, N).astype(f32)
    dtc = dt.reshape(B_, nc, Q, G, R)
    a_cum = jnp.cumsum(dtc * A.reshape(G, R), axis=2)
    xdt = xc * dtc[..., None]
    a_t = jnp.moveaxis(a_cum, 2, -1)
    tq = jnp.arange(Q)
    causal = tq[:, None] >= tq[None, :]
    seg = a_t[..., :, None] - a_t[..., None, :]
    L = jnp.exp(jnp.where(causal, seg, -jnp.inf))
    cb = jnp.einsum('bclgn,bcsgn->bcgls', Cc, Bc)
    y_diag = jnp.einsum('bcgrls,bcsgrp->bclgrp', L * cb[:, :, :, None], xdt)
    decay_states = jnp.exp(a_cum[:, :, -1:] - a_cum)
    states = jnp.einsum('bclgn,bclgrp->bcgrpn', Bc, xdt * decay_states[..., None])
    chunk_decay = jnp.exp(a_cum[:, :, -1])

    def step(carry, inp):
        st, dec = inp
        return carry * dec[..., None, None] + st, carry

    init = jnp.zeros((B_, G, R, P, N), f32)
    _, prev = lax.scan(step, init, (jnp.moveaxis(states, 1, 0), jnp.moveaxis(chunk_decay, 1, 0)))
    prev = jnp.moveaxis(prev, 0, 1)
    y_off = jnp.einsum('bclgn,bcgrpn->bclgrp', Cc, prev) * jnp.exp(a_cum)[..., None]
    y = y_diag + y_off + xc * d_skip.astype(f32).reshape(G, R)[..., None]
    y = y.reshape(B_, S_, SSD_D_INNER)
    g = (y * jax.nn.silu(z.astype(f32))).reshape(B_, S_, G, SSD_D_INNER // G)
    g = g * lax.rsqrt(jnp.mean(g * g, axis=-1, keepdims=True) + EPS)
    y = g.reshape(B_, S_, SSD_D_INNER).astype(h.dtype) * out_norm
    return y @ w_out


def dsa_mixer(h, positions, w_in, q_norm, k_norm, kidx_norm, w_out):
    B_, S_, _ = h.shape
    H, KVH, Dh, HI, DI = DSA_HEADS, DSA_KV_HEADS, DSA_HEAD_DIM, DSA_IDX_HEADS, DSA_IDX_DIM
    R = H // KVH
    f32 = jnp.float32
    o1 = DSA_Q_W
    o2 = o1 + DSA_KV_W
    o3 = o2 + DSA_KV_W
    o4 = o3 + DSA_QI_W
    o5 = o4 + DI
    proj = h @ w_in
    q = rope_partial(rms_norm(proj[..., :o1].reshape(B_, S_, H, Dh), q_norm), positions)
    k = rope_partial(rms_norm(proj[..., o1:o2].reshape(B_, S_, KVH, Dh), k_norm), positions)
    v = proj[..., o2:o3].reshape(B_, S_, KVH, Dh)
    qi = rope_partial(proj[..., o3:o4].reshape(B_, S_, HI, DI), positions)
    ki = rope_partial(rms_norm(proj[..., o4:o5], kidx_norm)[:, :, None, :], positions)[:, :, 0, :]
    wi = proj[..., o5:].astype(f32) * (HI ** -0.5 * DI ** -0.5)
    kv = jnp.concatenate([k, v], axis=-1)
    topk = min(DSA_TOPK_MAX, S_ // 4)
    nb = S_ // DSA_QBLOCK
    key_chunk = jnp.arange(S_) // CHUNK

    def to_blocks(a):
        return jnp.moveaxis(a.reshape(B_, nb, DSA_QBLOCK, *a.shape[2:]), 1, 0)

    def block_fn(args):
        qb, qib, wib, start = args
        q_chunk = (start + jnp.arange(DSA_QBLOCK)) // CHUNK
        admissible = key_chunk[None, :] <= q_chunk[:, None]
        logits = jnp.einsum('bqhd,bsd->bqhs', qib, ki)
        score = jnp.einsum('bqh,bqhs->bqs', wib, jax.nn.relu(logits).astype(f32))
        score = jnp.where(admissible[None], score, -jnp.inf)
        top_score, idx = lax.top_k(score, topk)
        valid = jnp.isfinite(top_score)
        kv_sel = jax.vmap(lambda kv_b, i_b: kv_b[i_b])(kv, idx)
        k_sel, v_sel = kv_sel[..., :Dh], kv_sel[..., Dh:]
        qg = qb.reshape(B_, DSA_QBLOCK, KVH, R, Dh)
        s = jnp.einsum('bqgrd,bqkgd->bqgrk', qg, k_sel).astype(f32) * (Dh ** -0.5)
        s = jnp.where(valid[:, :, None, None, :], s, -jnp.inf)
        p = jax.nn.softmax(s, axis=-1).astype(v_sel.dtype)
        o = jnp.einsum('bqgrk,bqkgd->bqgrd', p, v_sel)
        return o.reshape(B_, DSA_QBLOCK, H * Dh)

    starts = jnp.arange(nb, dtype=jnp.int32) * DSA_QBLOCK
    out = lax.map(block_fn, (to_blocks(q), to_blocks(qi), to_blocks(wi), starts))
    out = jnp.moveaxis(out, 0, 1).reshape(B_, S_, H * Dh)
    return out @ w_out


def memory_xattn(h, mem_n, w_q, w_kv, q_norm, k_norm, w_out):
    B_, S_, _ = h.shape
    M = mem_n.shape[1]
    q = rms_norm((h @ w_q).reshape(B_, S_, XA_HEADS, XA_HEAD_DIM), q_norm)
    kvm = mem_n @ w_kv
    k = rms_norm(kvm[..., :XA_W].reshape(B_, M, XA_HEADS, XA_HEAD_DIM), k_norm)
    v = kvm[..., XA_W:].reshape(B_, M, XA_HEADS, XA_HEAD_DIM)
    s = jnp.einsum('bshd,bmhd->bhsm', q, k).astype(jnp.float32) * (XA_HEAD_DIM ** -0.5)
    p = jax.nn.softmax(s, axis=-1).astype(v.dtype)
    o = jnp.einsum('bhsm,bmhd->bshd', p, v).reshape(B_, S_, XA_W)
    return o @ w_out


def sqrelu_ffn(h, w_up, w_down):
    return jnp.square(jax.nn.relu(h @ w_up)) @ w_down


def setup_inputs(seed: int = 0) -> dict:
    key = jax.random.key(seed)
    ks = jax.random.split(key, 32)
    f32 = jnp.float32

    def nrm(k, shape, scale):
        return jax.random.normal(k, shape, f32) * scale

    def gain(k, shape):
        return 1.0 + 0.02 * jax.random.normal(k, shape, f32)

    x = nrm(ks[0], (BATCH, SEQ, D_MODEL), 1.0)
    mem = nrm(ks[1], (BATCH, MEM_LEN, D_MODEL), 1.0)
    start = jax.random.randint(ks[2], (BATCH,), 0, 512, dtype=jnp.int32) * CHUNK
    positions = start[:, None] + jnp.arange(SEQ, dtype=jnp.int32)[None, :]
    dt0 = jnp.exp(jax.random.uniform(ks[12], (N_B, SSD_HEADS), f32, float(np.log(1e-3)), float(np.log(1e-1))))
    dt_bias = dt0 + jnp.log(-jnp.expm1(-dt0))
    a_log = jnp.log(jax.random.uniform(ks[13], (N_B, SSD_HEADS), f32, 1.0, 16.0))
    return {
        'x': x,
        'mem': mem,
        'positions': positions,
        'norm_mix': gain(ks[3], (DEPTH, D_MODEL)),
        'gm_w_in': nrm(ks[4], (N_A, D_MODEL, 2 * GM_HID), D_MODEL ** -0.5),
        'gm_v_norm': gain(ks[5], (N_A, GM_HID)),
        'gm_w_s': nrm(ks[6], (N_A, GM_GROUPS, GM_BLOCK, GM_BLOCK), GM_BLOCK ** -0.5),
        'gm_b_s': gain(ks[7], (N_A, GM_GROUPS, GM_BLOCK)),
        'gm_w_out': nrm(ks[8], (N_A, GM_HID, D_MODEL), GM_HID ** -0.5),
        'ssd_w_in': nrm(ks[9], (N_B, D_MODEL, SSD_IN), D_MODEL ** -0.5),
        'ssd_conv_w': nrm(ks[10], (N_B, SSD_CONV, SSD_CONV_CH), SSD_CONV ** -0.5),
        'ssd_conv_b': nrm(ks[11], (N_B, SSD_CONV_CH), 0.02),
        'ssd_dt_bias': dt_bias,
        'ssd_a_log': a_log,
        'ssd_d': gain(ks[14], (N_B, SSD_HEADS)),
        'ssd_out_norm': gain(ks[15], (N_B, SSD_D_INNER)),
        'ssd_w_out': nrm(ks[16], (N_B, SSD_D_INNER, D_MODEL), SSD_D_INNER ** -0.5),
        'dsa_w_in': nrm(ks[17], (N_C, D_MODEL, DSA_IN), D_MODEL ** -0.5),
        'dsa_q_norm': gain(ks[18], (N_C, DSA_HEAD_DIM)),
        'dsa_k_norm': gain(ks[19], (N_C, DSA_HEAD_DIM)),
        'dsa_kidx_norm': gain(ks[20], (N_C, DSA_IDX_DIM)),
        'dsa_w_out': nrm(ks[21], (N_C, DSA_Q_W, D_MODEL), DSA_Q_W ** -0.5),
        'norm_xa': gain(ks[22], (DEPTH, D_MODEL)),
        'norm_mem': gain(ks[23], (DEPTH, D_MODEL)),
        'xa_w_q': nrm(ks[24], (DEPTH, D_MODEL, XA_W), D_MODEL ** -0.5),
        'xa_w_kv': nrm(ks[25], (DEPTH, D_MODEL, 2 * XA_W), D_MODEL ** -0.5),
        'xa_q_norm': gain(ks[26], (DEPTH, XA_HEAD_DIM)),
        'xa_k_norm': gain(ks[27], (DEPTH, XA_HEAD_DIM)),
        'xa_w_out': nrm(ks[28], (DEPTH, XA_W, D_MODEL), XA_W ** -0.5),
        'norm_ffn': gain(ks[29], (DEPTH, D_MODEL)),
        'ffn_w_up': nrm(ks[30], (DEPTH, D_MODEL, FFN_HID), D_MODEL ** -0.5),
        'ffn_w_down': nrm(ks[31], (DEPTH, FFN_HID, D_MODEL), FFN_HID ** -0.5),
    }


def reference(x, mem, positions, norm_mix, gm_w_in, gm_v_norm, gm_w_s, gm_b_s, gm_w_out,
              ssd_w_in, ssd_conv_w, ssd_conv_b, ssd_dt_bias, ssd_a_log, ssd_d, ssd_out_norm, ssd_w_out,
              dsa_w_in, dsa_q_norm, dsa_k_norm, dsa_kidx_norm, dsa_w_out,
              norm_xa, norm_mem, xa_w_q, xa_w_kv, xa_q_norm, xa_k_norm, xa_w_out,
              norm_ffn, ffn_w_up, ffn_w_down):
    for i in range(DEPTH):
        kind = i % N_MIXERS
        j = i // N_MIXERS
        hn = rms_norm(x, norm_mix[i])
        if kind == 0:
            x = x + gmlp_mixer(hn, gm_w_in[j], gm_v_norm[j], gm_w_s[j], gm_b_s[j], gm_w_out[j])
        elif kind == 1:
            x = x + ssd_mixer(hn, ssd_w_in[j], ssd_conv_w[j], ssd_conv_b[j], ssd_dt_bias[j],
                              ssd_a_log[j], ssd_d[j], ssd_out_norm[j], ssd_w_out[j])
        else:
            x = x + dsa_mixer(hn, positions, dsa_w_in[j], dsa_q_norm[j], dsa_k_norm[j],
                              dsa_kidx_norm[j], dsa_w_out[j])
        x = x + memory_xattn(rms_norm(x, norm_xa[i]), rms_norm(mem, norm_mem[i]), xa_w_q[i],
                             xa_w_kv[i], xa_q_norm[i], xa_k_norm[i], xa_w_out[i])
        x = x + sqrelu_ffn(rms_norm(x, norm_ffn[i]), ffn_w_up[i], ffn_w_down[i])
    return x
```

```python
import functools

import jax
import jax.numpy as jnp
from jax import lax
from jax.experimental import pallas as pl
from jax.experimental.pallas import tpu as pltpu

F32 = jnp.float32
BF16 = jnp.bfloat16
I32 = jnp.int32

EPS = 1e-6
ROPE_THETA = 500000.0
CHUNK = 64
GM_BLOCK = 128
GM_GROUPS = 8
SSD_HEAD_DIM = 64
SSD_GROUPS = 8
SSD_STATE = 128
SSD_CONV = 4
SSD_CHUNK = 128
DSA_HEADS = 16
DSA_KV_HEADS = 4
DSA_HEAD_DIM = 64
DSA_IDX_HEADS = 8
DSA_IDX_DIM = 64
DSA_TOPK_MAX = 256
DSA_QBLOCK = 128
XA_HEADS = 4
XA_HEAD_DIM = 128

LANES = 128
VMEM_LIMIT_BYTES = 56 * 1024 * 1024
NEG_BIG = -1e30
INT_MIN = -(2 ** 31)


def _params(*semantics):
    return pltpu.CompilerParams(dimension_semantics=semantics, vmem_limit_bytes=VMEM_LIMIT_BYTES)


def _const_spec(shape):
    nd = len(shape)
    return pl.BlockSpec(shape, lambda *_: (0,) * nd, pipeline_mode=pl.Buffered(1))


def _row_tile(n, want):
    t = min(n, want)
    assert n % t == 0, (n, t)
    return t


def _rms(x, gain):
    ms = jnp.mean(x * x, axis=-1, keepdims=True)
    return x * lax.rsqrt(ms + EPS) * gain


def _dot(a, b):
    return jnp.dot(a, b, preferred_element_type=F32)


def _dot_nt(a, b):
    return lax.dot_general(a, b, (((1,), (1,)), ((), ())), preferred_element_type=F32)


def _dot_tn(a, b):
    return lax.dot_general(a, b, (((0,), (0,)), ((), ())), preferred_element_type=F32)


def _split3(a):
    hi = a.astype(BF16)
    r1 = a - hi.astype(F32)
    mid = r1.astype(BF16)
    lo = (r1 - mid.astype(F32)).astype(BF16)
    return hi, mid, lo


def _dot_f32_lhs(a, b_exact):
    hi, mid, lo = _split3(a)
    return _dot(hi, b_exact) + _dot(mid, b_exact) + _dot(lo, b_exact)


def _dot_f32_rhs(a_exact, b):
    hi, mid, lo = _split3(b)
    return _dot(a_exact, hi) + _dot(a_exact, mid) + _dot(a_exact, lo)


def _mem_kv_body(mem_ref, g_ref, w_ref, kn_ref, k_ref, v_ref):
    mn = _rms(mem_ref[0], g_ref[0]).astype(BF16)
    kv = _dot(mn, w_ref[0])
    xa_w = XA_HEADS * XA_HEAD_DIM
    for h in range(XA_HEADS):
        cols = slice(h * XA_HEAD_DIM, (h + 1) * XA_HEAD_DIM)
        k_ref[0, 0, :, cols] = _rms(kv[:, cols], kn_ref[0]).astype(BF16)
    v_ref[0, 0] = kv[:, xa_w:].astype(BF16)


def _mem_kv(mem, norm_mem, w_kv, k_norm):
    depth, d, _ = w_kv.shape
    b, m, _ = mem.shape
    xa_w = XA_HEADS * XA_HEAD_DIM
    out = jax.ShapeDtypeStruct((depth, b, m, xa_w), BF16)
    return pl.pallas_call(
        _mem_kv_body,
        grid=(depth, b),
        in_specs=[
            pl.BlockSpec((1, m, d), lambda i, j: (j, 0, 0)),
            pl.BlockSpec((1, 1, d), lambda i, j: (i, 0, 0)),
            pl.BlockSpec((1, d, 2 * xa_w), lambda i, j: (i, 0, 0)),
            pl.BlockSpec((1, 1, XA_HEAD_DIM), lambda i, j: (i, 0, 0)),
        ],
        out_specs=[pl.BlockSpec((1, 1, m, xa_w), lambda i, j: (i, j, 0, 0))] * 2,
        out_shape=[out, out],
        compiler_params=_params("arbitrary", "arbitrary"),
        name="mem_kv",
    )(mem, norm_mem[:, None, :], w_kv.astype(BF16), k_norm[:, None, :])


def _xattn_body(x_ref, g_ref, wq_ref, qn_ref, k_ref, v_ref, wo_ref, o_ref):
    x = x_ref[0]
    xn = _rms(x, g_ref[...]).astype(BF16)
    q = _dot(xn, wq_ref[...])
    scale = XA_HEAD_DIM ** -0.5
    heads = []
    for h in range(XA_HEADS):
        cols = slice(h * XA_HEAD_DIM, (h + 1) * XA_HEAD_DIM)
        qh = _rms(q[:, cols], qn_ref[...]).astype(BF16)
        s = _dot_nt(qh, k_ref[0, :, cols]) * scale
        p = jnp.exp(s - jnp.max(s, axis=-1, keepdims=True))
        l = jnp.sum(p, axis=-1, keepdims=True)
        oh = _dot(p.astype(BF16), v_ref[0, :, cols]) / l
        heads.append(oh.astype(BF16))
    o = jnp.concatenate(heads, axis=-1)
    o_ref[0] = x + _dot(o, wo_ref[...])


def _xattn(x, gain, w_q, q_norm, k, v, w_out, tm=512):
    b, s, d = x.shape
    m = k.shape[1]
    xa_w = XA_HEADS * XA_HEAD_DIM
    tm = _row_tile(s, tm)
    return pl.pallas_call(
        _xattn_body,
        grid=(b, s // tm),
        in_specs=[
            pl.BlockSpec((1, tm, d), lambda i, j: (i, j, 0)),
            _const_spec((1, d)),
            _const_spec((d, xa_w)),
            _const_spec((1, XA_HEAD_DIM)),
            pl.BlockSpec((1, m, xa_w), lambda i, j: (i, 0, 0)),
            pl.BlockSpec((1, m, xa_w), lambda i, j: (i, 0, 0)),
            _const_spec((xa_w, d)),
        ],
        out_specs=pl.BlockSpec((1, tm, d), lambda i, j: (i, j, 0)),
        out_shape=jax.ShapeDtypeStruct(x.shape, F32),
        compiler_params=_params("arbitrary", "arbitrary"),
        name="xattn",
    )(x, gain[None, :], w_q.astype(BF16), q_norm[None, :], k, v, w_out.astype(BF16))


def _ffn_body(x_ref, g_ref, wu_ref, wd_ref, o_ref, *, hid_chunk):
    x = x_ref[...]
    xn = _rms(x, g_ref[...]).astype(BF16)
    acc = x
    for c in range(0, wu_ref.shape[1], hid_chunk):
        h = _dot(xn, wu_ref[:, c:c + hid_chunk])
        h = jnp.square(jnp.maximum(h, 0.0)).astype(BF16)
        acc = acc + _dot(h, wd_ref[c:c + hid_chunk, :])
    o_ref[...] = acc


def _ffn(x2, gain, w_up, w_down, tm=512, hid_chunk=1024):
    t, d = x2.shape
    hid = w_up.shape[1]
    tm = _row_tile(t, tm)
    return pl.pallas_call(
        functools.partial(_ffn_body, hid_chunk=hid_chunk),
        grid=(t // tm,),
        in_specs=[
            pl.BlockSpec((tm, d), lambda i: (i, 0)),
            _const_spec((1, d)),
            _const_spec((d, hid)),
            _const_spec((hid, d)),
        ],
        out_specs=pl.BlockSpec((tm, d), lambda i: (i, 0)),
        out_shape=jax.ShapeDtypeStruct(x2.shape, F32),
        compiler_params=_params("arbitrary"),
        name="ffn",
    )(x2, gain[None, :], w_up.astype(BF16), w_down.astype(BF16))


def _proj_res_body(x_ref, y_ref, w_ref, o_ref):
    o_ref[...] = x_ref[...] + _dot(y_ref[...], w_ref[...])


def _proj_res(x2, y2, w, tm=512):
    t, d = x2.shape
    k = y2.shape[1]
    tm = _row_tile(t, tm)
    return pl.pallas_call(
        _proj_res_body,
        grid=(t // tm,),
        in_specs=[
            pl.BlockSpec((tm, d), lambda i: (i, 0)),
            pl.BlockSpec((tm, k), lambda i: (i, 0)),
            _const_spec((k, d)),
        ],
        out_specs=pl.BlockSpec((tm, d), lambda i: (i, 0)),
        out_shape=jax.ShapeDtypeStruct(x2.shape, F32),
        compiler_params=_params("arbitrary"),
        name="proj_res",
    )(x2, y2, w.astype(BF16))


def _gmlp_body(x_ref, g_ref, win_ref, vn_ref, ws_ref, bs_ref, wout_ref, o_ref,
               xn_s, u_s, v_s, h_s, *, col_chunk):
    tm = x_ref.shape[0]
    hid = u_s.shape[1]
    gw = hid // GM_GROUPS
    x = x_ref[...]
    xn_s[...] = _rms(x, g_ref[...]).astype(BF16)
    for c in range(0, hid, col_chunk):
        u_s[:, c:c + col_chunk] = jax.nn.gelu(_dot(xn_s[...], win_ref[:, c:c + col_chunk]))
    ssq = jnp.zeros((tm, 1), F32)
    for c in range(0, hid, col_chunk):
        vc = jax.nn.gelu(_dot(xn_s[...], win_ref[:, hid + c:hid + c + col_chunk]))
        v_s[:, c:c + col_chunk] = vc
        ssq = ssq + jnp.sum(vc * vc, axis=-1, keepdims=True)
    inv = lax.rsqrt(ssq / hid + EPS)
    t_i = lax.broadcasted_iota(I32, (GM_BLOCK, GM_BLOCK), 0)
    s_i = lax.broadcasted_iota(I32, (GM_BLOCK, GM_BLOCK), 1)
    causal = (s_i // CHUNK) <= (t_i // CHUNK)
    for g in range(GM_GROUPS):
        cols = slice(g * gw, (g + 1) * gw)
        wsg = jnp.where(causal, ws_ref[g], 0.0).astype(BF16)
        bias = bs_ref[:, g:g + 1]
        for n in range(tm // GM_BLOCK):
            rows = slice(n * GM_BLOCK, (n + 1) * GM_BLOCK)
            vg = (v_s[rows, cols] * inv[rows] * vn_ref[:, cols]).astype(BF16)
            sg = _dot(wsg, vg) + bias
            h_s[rows, cols] = (u_s[rows, cols] * sg).astype(BF16)
    o_ref[...] = x + _dot(h_s[...], wout_ref[...])


def _gmlp(x2, gain, w_in, v_norm, w_s, b_s, w_out, tm=256, col_chunk=512):
    t, d = x2.shape
    hid = w_out.shape[0]
    tm = _row_tile(t, tm)
    assert tm % GM_BLOCK == 0
    return pl.pallas_call(
        functools.partial(_gmlp_body, col_chunk=col_chunk),
        grid=(t // tm,),
        in_specs=[
            pl.BlockSpec((tm, d), lambda i: (i, 0)),
            _const_spec((1, d)),
            _const_spec((d, 2 * hid)),
            _const_spec((1, hid)),
            _const_spec((GM_GROUPS, GM_BLOCK, GM_BLOCK)),
            _const_spec((GM_BLOCK, GM_GROUPS)),
            _const_spec((hid, d)),
        ],
        out_specs=pl.BlockSpec((tm, d), lambda i: (i, 0)),
        out_shape=jax.ShapeDtypeStruct(x2.shape, F32),
        scratch_shapes=[
            pltpu.VMEM((tm, d), BF16),
            pltpu.VMEM((tm, hid), F32),
            pltpu.VMEM((tm, hid), F32),
            pltpu.VMEM((tm, hid), BF16),
        ],
        compiler_params=_params("arbitrary"),
        name="gmlp",
    )(x2, gain[None, :], w_in.astype(BF16), v_norm[None, :], w_s, b_s.T, w_out.astype(BF16))


def _ssd_in_body(x_ref, g_ref, wz_ref, wx_ref, wdt_ref, z_ref, xbc_ref, dt_ref, *, col_chunk):
    xn = _rms(x_ref[...], g_ref[...]).astype(BF16)
    for c in range(0, wz_ref.shape[1], col_chunk):
        z_ref[:, c:c + col_chunk] = _dot(xn, wz_ref[:, c:c + col_chunk]).astype(BF16)
    for c in range(0, wx_ref.shape[1], col_chunk):
        xbc_ref[:, c:c + col_chunk] = _dot(xn, wx_ref[:, c:c + col_chunk]).astype(BF16)
    dt_ref[...] = _dot(xn, wdt_ref[...])


def _ssd_in(x2, gain, w_z, w_xbc, w_dt, tm=512, col_chunk=512):
    t, d = x2.shape
    tm = _row_tile(t, tm)
    nz, nx, ndt = w_z.shape[1], w_xbc.shape[1], w_dt.shape[1]
    return pl.pallas_call(
        functools.partial(_ssd_in_body, col_chunk=col_chunk),
        grid=(t // tm,),
        in_specs=[
            pl.BlockSpec((tm, d), lambda i: (i, 0)),
            _const_spec((1, d)),
            _const_spec((d, nz)),
            _const_spec((d, nx)),
            _const_spec((d, ndt)),
        ],
        out_specs=[
            pl.BlockSpec((tm, nz), lambda i: (i, 0)),
            pl.BlockSpec((tm, nx), lambda i: (i, 0)),
            pl.BlockSpec((tm, ndt), lambda i: (i, 0)),
        ],
        out_shape=[
            jax.ShapeDtypeStruct((t, nz), BF16),
            jax.ShapeDtypeStruct((t, nx), BF16),
            jax.ShapeDtypeStruct((t, ndt), F32),
        ],
        compiler_params=_params("arbitrary"),
        name="ssd_in",
    )(x2, gain[None, :], w_z, w_xbc, w_dt)


def _ssd_core_body(z_ref, xbc_ref, dt_ref, cw_ref, cb_ref, dtb_ref, alog_ref, dskip_ref, onorm_ref,
                   tri_ref, expand_ref, y_ref, state_s, ext_s):
    q = SSD_CHUNK
    n_state = SSD_STATE
    d_inner = z_ref.shape[2]
    heads_per_group = d_inner // SSD_HEAD_DIM // SSD_GROUPS
    gw = heads_per_group * SSD_HEAD_DIM
    halo = 8

    @pl.when(pl.program_id(1) == 0)
    def _():
        state_s[...] = jnp.zeros_like(state_s)
        ext_s[0:halo, :] = jnp.zeros((halo, ext_s.shape[1]), F32)

    ext_s[halo:halo + q, :] = xbc_ref[0].astype(F32)
    conv = cb_ref[...]
    for k in range(SSD_CONV):
        off = halo - (SSD_CONV - 1) + k
        conv = conv + cw_ref[k:k + 1, :] * ext_s[off:off + q, :]
    ext_s[0:halo, :] = ext_s[q:q + halo, :]
    xbc = conv * jax.nn.sigmoid(conv)
    xs = xbc[:, :d_inner]
    bm = xbc[:, d_inner:d_inner + SSD_GROUPS * n_state].astype(BF16)
    cm = xbc[:, d_inner + SSD_GROUPS * n_state:].astype(BF16)

    dt_raw = dt_ref[0] + dtb_ref[...]
    dt = jnp.maximum(dt_raw, 0.0) + jnp.log1p(jnp.exp(-jnp.abs(dt_raw)))
    a = dt * (-jnp.exp(alog_ref[...]))
    a_cum = _dot_f32_rhs(tri_ref[...], a)
    a_cum_t = a_cum.T
    expand = expand_ref[...]
    dt_e = _dot_f32_lhs(dt, expand)
    acum_e = _dot_f32_lhs(a_cum, expand)
    alast_e = acum_e[q - 1:q, :]
    xdt = xs * dt_e
    xw = (xdt * jnp.exp(alast_e - acum_e)).astype(BF16)
    xdt_b = xdt.astype(BF16)
    ea_e = jnp.exp(acum_e)
    chunk_decay_e = jnp.exp(alast_e)

    l_i = lax.broadcasted_iota(I32, (q, q), 0)
    s_i = lax.broadcasted_iota(I32, (q, q), 1)
    causal = l_i >= s_i
    lane_g = lax.broadcasted_iota(I32, (q, gw), 1) // SSD_HEAD_DIM

    zf = z_ref[0].astype(F32)
    for g in range(SSD_GROUPS):
        gcols = slice(g * gw, (g + 1) * gw)
        ncols = slice(g * n_state, (g + 1) * n_state)
        cg = cm[:, ncols]
        bg = bm[:, ncols]
        cb = _dot_nt(cg, bg)
        st = state_s[g]
        y_off = _dot(cg, st.astype(BF16)) * ea_e[:, gcols]
        ms = []
        xblk = []
        xg = xdt_b[:, gcols]
        for r in range(heads_per_group):
            h = g * heads_per_group + r
            seg = a_cum[:, h:h + 1] - a_cum_t[h:h + 1, :]
            decay = jnp.exp(jnp.where(causal, seg, -jnp.inf))
            ms.append((decay * cb).astype(BF16))
            xblk.append(jnp.where(lane_g == r, xg, jnp.zeros_like(xg)))
        y_diag = _dot(jnp.concatenate(ms, axis=1), jnp.concatenate(xblk, axis=0))
        y = y_diag + y_off + xs[:, gcols] * dskip_ref[:, gcols]
        state_s[g] = st * chunk_decay_e[:, gcols] + _dot_tn(bg, xw[:, gcols])
        zg = zf[:, gcols]
        gated = y * (zg * jax.nn.sigmoid(zg))
        gated = gated * lax.rsqrt(jnp.mean(gated * gated, axis=-1, keepdims=True) + EPS)
        y_ref[0, :, gcols] = (gated * onorm_ref[:, gcols]).astype(BF16)


def _ssd_core(z, xbc, dt, conv_w, conv_b, dt_bias, a_log, d_skip, out_norm):
    b, s, d_inner = z.shape
    conv_ch = xbc.shape[2]
    n_heads = d_inner // SSD_HEAD_DIM
    gw = d_inner // SSD_GROUPS
    q = SSD_CHUNK
    assert s % q == 0 and n_heads <= LANES and dt.shape[2] == LANES

    def pad_heads(v):
        return jnp.pad(v, (0, LANES - n_heads))[None, :]

    tri = (jnp.arange(q)[:, None] >= jnp.arange(q)[None, :]).astype(BF16)
    expand = (jnp.arange(LANES)[:, None] == (jnp.arange(d_inner) // SSD_HEAD_DIM)[None, :]).astype(BF16)
    return pl.pallas_call(
        _ssd_core_body,
        grid=(b, s // q),
        in_specs=[
            pl.BlockSpec((1, q, d_inner), lambda i, j: (i, j, 0)),
            pl.BlockSpec((1, q, conv_ch), lambda i, j: (i, j, 0)),
            pl.BlockSpec((1, q, LANES), lambda i, j: (i, j, 0)),
            _const_spec((SSD_CONV, conv_ch)),
            _const_spec((1, conv_ch)),
            _const_spec((1, LANES)),
            _const_spec((1, LANES)),
            _const_spec((1, d_inner)),
            _const_spec((1, d_inner)),
            _const_spec((q, q)),
            _const_spec((LANES, d_inner)),
        ],
        out_specs=pl.BlockSpec((1, q, d_inner), lambda i, j: (i, j, 0)),
        out_shape=jax.ShapeDtypeStruct((b, s, d_inner), BF16),
        scratch_shapes=[
            pltpu.VMEM((SSD_GROUPS, SSD_STATE, gw), F32),
            pltpu.VMEM((q + 8, conv_ch), F32),
        ],
        compiler_params=_params("arbitrary", "arbitrary"),
        name="ssd_core",
    )(z, xbc, dt, conv_w, conv_b[None, :], pad_heads(dt_bias), pad_heads(a_log),
      jnp.repeat(d_skip, SSD_HEAD_DIM)[None, :], out_norm[None, :], tri, expand)


def _ssd_mixer(x, gain, w_in, conv_w, conv_b, dt_bias, a_log, d_skip, out_norm, w_out):
    b, s, d = x.shape
    d_inner = w_out.shape[0]
    conv_ch = conv_w.shape[1]
    n_heads = d_inner // SSD_HEAD_DIM
    w_in = w_in.astype(BF16)
    w_z = w_in[:, :d_inner]
    w_xbc = w_in[:, d_inner:d_inner + conv_ch]
    w_dt = jnp.pad(w_in[:, d_inner + conv_ch:], ((0, 0), (0, LANES - n_heads)))
    x2 = x.reshape(b * s, d)
    z, xbc, dt = _ssd_in(x2, gain, w_z, w_xbc, w_dt)
    y = _ssd_core(z.reshape(b, s, d_inner), xbc.reshape(b, s, conv_ch), dt.reshape(b, s, LANES),
                  conv_w, conv_b, dt_bias, a_log, d_skip, out_norm)
    return _proj_res(x2, y.reshape(b * s, d_inner), w_out).reshape(b, s, d)


def _halves_rms(x, gain, lane):
    lo = lane < (LANES // 2)
    x2 = x * x
    s_lo = jnp.sum(jnp.where(lo, x2, 0.0), axis=-1, keepdims=True)
    s_hi = jnp.sum(jnp.where(lo, 0.0, x2), axis=-1, keepdims=True)
    ms = jnp.where(lo, s_lo, s_hi) * (2.0 / LANES)
    return x * lax.rsqrt(ms + EPS) * gain


def _rope(x, cos, sin_lo, sin_hi):
    half = DSA_HEAD_DIM // 8
    return x * cos + pltpu.roll(x, LANES - half, 1) * sin_lo + pltpu.roll(x, half, 1) * sin_hi


def _dsa_prep_body(x_ref, g_ref, w_ref, pos_ref, invf_ref, invfk_ref, qn_ref, kn_ref, kin_ref,
                   q_ref, k_ref, v_ref, qi_ref, ki_ref, wi_ref):
    tm = x_ref.shape[1]
    q_w = DSA_HEADS * DSA_HEAD_DIM
    kv_w = DSA_KV_HEADS * DSA_HEAD_DIM
    qi_w = DSA_IDX_HEADS * DSA_IDX_DIM
    half = DSA_HEAD_DIM // 8
    xn = _rms(x_ref[0], g_ref[...]).astype(BF16)
    pos = pos_ref[0].astype(F32)
    lane = lax.broadcasted_iota(I32, (tm, LANES), 1)
    j = lane % DSA_HEAD_DIM

    def tables(invf, jj):
        ang = pos * invf
        c, s = jnp.cos(ang), jnp.sin(ang)
        return c, jnp.where(jj < half, -s, 0.0), jnp.where(jj >= half, s, 0.0)

    cos, sin_lo, sin_hi = tables(invf_ref[...], j)

    def proj(c0, width):
        return _dot(xn, w_ref[:, c0:c0 + width])

    o1, o2, o3, o4 = q_w, q_w + kv_w, q_w + 2 * kv_w, q_w + 2 * kv_w + qi_w
    for c in range(0, q_w, LANES):
        blk = _halves_rms(proj(c, LANES), qn_ref[...], lane)
        q_ref[0, :, c:c + LANES] = (_rope(blk, cos, sin_lo, sin_hi) * (DSA_HEAD_DIM ** -0.5)).astype(BF16)
    for c in range(0, kv_w, LANES):
        blk = _halves_rms(proj(o1 + c, LANES), kn_ref[...], lane)
        k_ref[0, :, c:c + LANES] = _rope(blk, cos, sin_lo, sin_hi).astype(BF16)
    v_ref[0] = proj(o2, kv_w).astype(BF16)
    for c in range(0, qi_w, LANES):
        qi_ref[0, :, c:c + LANES] = _rope(proj(o3 + c, LANES), cos, sin_lo, sin_hi).astype(BF16)
    blk = proj(o4, LANES)
    is_ki = lane < DSA_IDX_DIM
    ms = jnp.sum(jnp.where(is_ki, blk * blk, 0.0), axis=-1, keepdims=True) * (1.0 / DSA_IDX_DIM)
    kin = blk * lax.rsqrt(ms + EPS) * kin_ref[...]
    cos_k, sin_lo_k, sin_hi_k = tables(invfk_ref[...], lane)
    ki_ref[0] = _rope(kin, cos_k, sin_lo_k, sin_hi_k).astype(BF16)
    wi_ref[0] = blk * (DSA_IDX_HEADS ** -0.5 * DSA_IDX_DIM ** -0.5)


def _dsa_prep(x, gain, w_in, positions, q_norm, k_norm, kidx_norm, tm=256):
    b, s, d = x.shape
    q_w = DSA_HEADS * DSA_HEAD_DIM
    kv_w = DSA_KV_HEADS * DSA_HEAD_DIM
    qi_w = DSA_IDX_HEADS * DSA_IDX_DIM
    n_in = w_in.shape[1]
    assert n_in == q_w + 2 * kv_w + qi_w + DSA_IDX_DIM + DSA_IDX_HEADS
    n_pad = q_w + 2 * kv_w + qi_w + LANES
    w = jnp.pad(w_in.astype(BF16), ((0, 0), (0, n_pad - n_in)))
    tm = _row_tile(s, tm)
    half = DSA_HEAD_DIM // 8
    inv_freq = ROPE_THETA ** (-jnp.arange(half, dtype=F32) / half)
    lane = jnp.arange(LANES)
    j = lane % DSA_HEAD_DIM
    invf = jnp.where(j < 2 * half, inv_freq[j % half], 0.0)[None, :]
    invf_k = jnp.where(lane < 2 * half, inv_freq[lane % half], 0.0)[None, :]
    kin = jnp.concatenate([kidx_norm, jnp.ones((LANES - DSA_IDX_DIM,), F32)])[None, :]

    def tok(width, dtype):
        return pl.BlockSpec((1, tm, width), lambda i, t: (i, t, 0)), jax.ShapeDtypeStruct((b, s, width), dtype)

    outs = [tok(q_w, BF16), tok(kv_w, BF16), tok(kv_w, BF16), tok(qi_w, BF16), tok(LANES, BF16), tok(LANES, F32)]
    return pl.pallas_call(
        _dsa_prep_body,
        grid=(b, s // tm),
        in_specs=[
            pl.BlockSpec((1, tm, d), lambda i, t: (i, t, 0)),
            _const_spec((1, d)),
            _const_spec((d, n_pad)),
            pl.BlockSpec((1, tm, 1), lambda i, t: (i, t, 0)),
            _const_spec((1, LANES)),
            _const_spec((1, LANES)),
            _const_spec((1, LANES)),
            _const_spec((1, LANES)),
            _const_spec((1, LANES)),
        ],
        out_specs=[o[0] for o in outs],
        out_shape=[o[1] for o in outs],
        compiler_params=_params("arbitrary", "arbitrary"),
        name="dsa_prep",
    )(x, gain[None, :], w, positions[:, :, None], invf, invf_k,
      jnp.tile(q_norm, 2)[None, :], jnp.tile(k_norm, 2)[None, :], kin)


def _dsa_attn_body(q_ref, k_ref, v_ref, qi_ref, ki_ref, wi_ref, o_ref, key_s, mask_s, *, topk):
    qb = DSA_QBLOCK
    tk = qb
    i = pl.program_id(1)
    n_tiles = i + 1
    rep = DSA_HEADS // DSA_KV_HEADS
    row = lax.broadcasted_iota(I32, (qb, tk), 0)
    col = lax.broadcasted_iota(I32, (qb, tk), 1)
    diag_ok = (col // CHUNK) <= (row // CHUNK)
    wi = wi_ref[0]
    wcols = [wi[:, DSA_IDX_DIM + h:DSA_IDX_DIM + h + 1] for h in range(DSA_IDX_HEADS)]

    def score_tile(t, carry):
        kt = ki_ref[0, pl.ds(pl.multiple_of(t * tk, tk), tk), :]
        score = jnp.zeros((qb, tk), F32)
        for h in range(DSA_IDX_HEADS):
            score = score + wcols[h] * jnp.maximum(_dot_nt(qi_ref[0, h], kt), 0.0)
        score = score + 0.0
        bits = pltpu.bitcast(score, I32)
        key = bits ^ ((bits >> 31) & jnp.int32(0x7FFFFFFF))
        key = jnp.where((t < i) | diag_ok, key, jnp.int32(INT_MIN))
        key_s[t] = key
        return carry

    lax.fori_loop(0, n_tiles, score_tile, 0)

    def count(pred_fn):
        def body(t, acc):
            return acc + jnp.where(pred_fn(key_s[t]), 1, 0).astype(I32)
        acc = lax.fori_loop(0, n_tiles, body, jnp.zeros((qb, tk), I32))
        return jnp.sum(acc, axis=-1, keepdims=True)

    def radix_step(b, thr):
        cand = thr + lax.shift_left(jnp.int32(1), 31 - b)
        cnt = count(lambda key: key >= cand)
        return jnp.where(cnt >= topk, cand, thr)

    thr = lax.fori_loop(0, 32, radix_step, jnp.full((qb, 1), INT_MIN, I32))
    n_gt = count(lambda key: key > thr)
    need = (topk - n_gt).astype(F32)

    strict_upper = (row < col).astype(BF16)
    ones = jnp.ones((tk, tk), BF16)

    def mask_tile(t, before):
        key = key_s[t]
        eq = key == thr
        eqb = jnp.where(eq, 1.0, 0.0).astype(BF16)
        rank = before + _dot(eqb, strict_upper)
        sel = (key > thr) | (eq & (rank < need))
        sel = sel & (key != jnp.int32(INT_MIN))
        mask_s[t] = jnp.where(sel, 0.0, NEG_BIG)
        return before + _dot(eqb, ones)

    lax.fori_loop(0, n_tiles, mask_tile, jnp.zeros((qb, tk), F32))

    for g in range(DSA_KV_HEADS):
        qg = q_ref[0, g * rep:(g + 1) * rep].reshape(rep * qb, DSA_HEAD_DIM)

        def attn_tile(t, carry):
            m, l, acc = carry
            rows = pl.ds(pl.multiple_of(t * tk, tk), tk)
            s = _dot_nt(qg, k_ref[0, g, rows, :])
            s = (s.reshape(rep, qb, tk) + mask_s[t][None]).reshape(rep * qb, tk)
            m_new = jnp.maximum(m, jnp.max(s, axis=-1, keepdims=True))
            alpha = jnp.exp(m - m_new)
            p = jnp.exp(s - m_new)
            l = alpha * l + jnp.sum(p, axis=-1, keepdims=True)
            acc = alpha * acc + _dot(p.astype(BF16), v_ref[0, g, rows, :])
            return m_new, l, acc

        init = (jnp.full((rep * qb, 1), NEG_BIG, F32), jnp.zeros((rep * qb, 1), F32),
                jnp.zeros((rep * qb, DSA_HEAD_DIM), F32))
        _, l, acc = lax.fori_loop(0, n_tiles, attn_tile, init)
        o_ref[0, g * rep:(g + 1) * rep] = (acc / l).reshape(rep, qb, DSA_HEAD_DIM).astype(BF16)


def _dsa_attn(q, k, v, qi, ki, wi):
    b, h, s, dh = q.shape
    qb = DSA_QBLOCK
    nq = s // qb
    topk = min(DSA_TOPK_MAX, s // 4)
    return pl.pallas_call(
        functools.partial(_dsa_attn_body, topk=topk),
        grid=(b, nq),
        in_specs=[
            pl.BlockSpec((1, h, qb, dh), lambda i, j: (i, 0, j, 0)),
            pl.BlockSpec((1, DSA_KV_HEADS, s, dh), lambda i, j: (i, 0, 0, 0)),
            pl.BlockSpec((1, DSA_KV_HEADS, s, dh), lambda i, j: (i, 0, 0, 0)),
            pl.BlockSpec((1, DSA_IDX_HEADS, qb, DSA_IDX_DIM), lambda i, j: (i, 0, j, 0)),
            pl.BlockSpec((1, s, DSA_IDX_DIM), lambda i, j: (i, 0, 0)),
            pl.BlockSpec((1, qb, LANES), lambda i, j: (i, j, 0)),
        ],
        out_specs=pl.BlockSpec((1, h, qb, dh), lambda i, j: (i, 0, j, 0)),
        out_shape=jax.ShapeDtypeStruct((b, h, s, dh), BF16),
        scratch_shapes=[
            pltpu.VMEM((nq, qb, qb), I32),
            pltpu.VMEM((nq, qb, qb), F32),
        ],
        compiler_params=_params("arbitrary", "arbitrary"),
        name="dsa_attn",
    )(q, k, v, qi, ki, wi)


def _dsa_mixer(x, positions, gain, w_in, q_norm, k_norm, kidx_norm, w_out):
    b, s, d = x.shape
    q, k, v, qi, kiw, wi = _dsa_prep(x, gain, w_in, positions, q_norm, k_norm, kidx_norm)

    def heads_major(a, n_heads):
        return a.reshape(b, s, n_heads, a.shape[2] // n_heads).transpose(0, 2, 1, 3)

    o = _dsa_attn(heads_major(q, DSA_HEADS), heads_major(k, DSA_KV_HEADS), heads_major(v, DSA_KV_HEADS),
                  heads_major(qi, DSA_IDX_HEADS), kiw[:, :, :DSA_IDX_DIM], wi)
    o = o.transpose(0, 2, 1, 3).reshape(b * s, DSA_HEADS * DSA_HEAD_DIM)
    return _proj_res(x.reshape(b * s, d), o, w_out).reshape(b, s, d)


def kernel(x, mem, positions, norm_mix, gm_w_in, gm_v_norm, gm_w_s, gm_b_s, gm_w_out, ssd_w_in, ssd_conv_w, ssd_conv_b, ssd_dt_bias, ssd_a_log, ssd_d, ssd_out_norm, ssd_w_out, dsa_w_in, dsa_q_norm, dsa_k_norm, dsa_kidx_norm, dsa_w_out, norm_xa, norm_mem, xa_w_q, xa_w_kv, xa_q_norm, xa_k_norm, xa_w_out, norm_ffn, ffn_w_up, ffn_w_down):
    b, s, d = x.shape
    depth = norm_mix.shape[0]
    mem_k, mem_v = _mem_kv(mem, norm_mem, xa_w_kv, xa_k_norm)
    for i in range(depth):
        kind, j = i % 3, i // 3
        if kind == 0:
            x = _gmlp(x.reshape(b * s, d), norm_mix[i], gm_w_in[j], gm_v_norm[j], gm_w_s[j], gm_b_s[j],
                      gm_w_out[j]).reshape(b, s, d)
        elif kind == 1:
            x = _ssd_mixer(x, norm_mix[i], ssd_w_in[j], ssd_conv_w[j], ssd_conv_b[j], ssd_dt_bias[j],
                           ssd_a_log[j], ssd_d[j], ssd_out_norm[j], ssd_w_out[j])
        else:
            x = _dsa_mixer(x, positions, norm_mix[i], dsa_w_in[j], dsa_q_norm[j], dsa_k_norm[j],
                           dsa_kidx_norm[j], dsa_w_out[j])
        x = _xattn(x, norm_xa[i], xa_w_q[i], xa_q_norm[i], mem_k[i], mem_v[i], xa_w_out[i])
        x = _ffn(x.reshape(b * s, d), norm_ffn[i], ffn_w_up[i], ffn_w_down[i]).reshape(b, s, d)
    return x
```

```python
import functools

import jax
import jax.numpy as jnp
from jax import lax
from jax.experimental import pallas as pl
from jax.experimental.pallas import tpu as pltpu

F32 = jnp.float32
BF16 = jnp.bfloat16
I32 = jnp.int32

EPS = 1e-6
ROPE_THETA = 500000.0
CHUNK = 64
GM_BLOCK = 128
GM_GROUPS = 8
SSD_HEAD_DIM = 64
SSD_GROUPS = 8
SSD_STATE = 128
SSD_CONV = 4
SSD_CHUNK = 128
DSA_HEADS = 16
DSA_KV_HEADS = 4
DSA_HEAD_DIM = 64
DSA_IDX_HEADS = 8
DSA_IDX_DIM = 64
DSA_TOPK_MAX = 256
DSA_QTILE = 256
XA_HEADS = 4
XA_HEAD_DIM = 128

LANES = 128
VMEM_LIMIT_BYTES = 56 * 1024 * 1024
NEG_BIG = -1e30
INT_MIN = -(2 ** 31)


def _params(*semantics):
    return pltpu.CompilerParams(dimension_semantics=semantics, vmem_limit_bytes=VMEM_LIMIT_BYTES)


def _const_spec(shape):
    nd = len(shape)
    return pl.BlockSpec(shape, lambda *_: (0,) * nd, pipeline_mode=pl.Buffered(1))


def _row_tile(n, want):
    t = min(n, want)
    assert n % t == 0, (n, t)
    return t


def _rms(x, gain):
    ms = jnp.mean(x * x, axis=-1, keepdims=True)
    return x * lax.rsqrt(ms + EPS) * gain


def _dot(a, b):
    return jnp.dot(a, b, preferred_element_type=F32)


def _dot_nt(a, b):
    return lax.dot_general(a, b, (((1,), (1,)), ((), ())), preferred_element_type=F32)


def _dot_tn(a, b):
    return lax.dot_general(a, b, (((0,), (0,)), ((), ())), preferred_element_type=F32)


def _split3(a):
    hi = a.astype(BF16)
    r1 = a - hi.astype(F32)
    mid = r1.astype(BF16)
    lo = (r1 - mid.astype(F32)).astype(BF16)
    return hi, mid, lo


def _dot_f32_lhs(a, b_exact):
    hi, mid, lo = _split3(a)
    return _dot(hi, b_exact) + _dot(mid, b_exact) + _dot(lo, b_exact)


def _dot_f32_rhs(a_exact, b):
    hi, mid, lo = _split3(b)
    return _dot(a_exact, hi) + _dot(a_exact, mid) + _dot(a_exact, lo)


def _mem_kv_body(mem_ref, g_ref, w_ref, kn_ref, k_ref, v_ref):
    mn = _rms(mem_ref[0], g_ref[0]).astype(BF16)
    kv = _dot(mn, w_ref[0])
    xa_w = XA_HEADS * XA_HEAD_DIM
    for h in range(XA_HEADS):
        cols = slice(h * XA_HEAD_DIM, (h + 1) * XA_HEAD_DIM)
        k_ref[0, 0, :, cols] = _rms(kv[:, cols], kn_ref[0]).astype(BF16)
    v_ref[0, 0] = kv[:, xa_w:].astype(BF16)


def _mem_kv(mem, norm_mem, w_kv, k_norm):
    depth, d, _ = w_kv.shape
    b, m, _ = mem.shape
    xa_w = XA_HEADS * XA_HEAD_DIM
    out = jax.ShapeDtypeStruct((depth, b, m, xa_w), BF16)
    return pl.pallas_call(
        _mem_kv_body,
        grid=(depth, b),
        in_specs=[
            pl.BlockSpec((1, m, d), lambda i, j: (j, 0, 0)),
            pl.BlockSpec((1, 1, d), lambda i, j: (i, 0, 0)),
            pl.BlockSpec((1, d, 2 * xa_w), lambda i, j: (i, 0, 0)),
            pl.BlockSpec((1, 1, XA_HEAD_DIM), lambda i, j: (i, 0, 0)),
        ],
        out_specs=[pl.BlockSpec((1, 1, m, xa_w), lambda i, j: (i, j, 0, 0))] * 2,
        out_shape=[out, out],
        compiler_params=_params("arbitrary", "arbitrary"),
        name="mem_kv",
    )(mem, norm_mem[:, None, :], w_kv.astype(BF16), k_norm[:, None, :])


def _xattn_body(x_ref, g_ref, wq_ref, qn_ref, k_ref, v_ref, wo_ref, o_ref):
    x = x_ref[0]
    xn = _rms(x, g_ref[...]).astype(BF16)
    q = _dot(xn, wq_ref[...])
    scale = XA_HEAD_DIM ** -0.5
    heads = []
    for h in range(XA_HEADS):
        cols = slice(h * XA_HEAD_DIM, (h + 1) * XA_HEAD_DIM)
        qh = _rms(q[:, cols], qn_ref[...]).astype(BF16)
        s = _dot_nt(qh, k_ref[0, :, cols]) * scale
        p = jnp.exp(s - jnp.max(s, axis=-1, keepdims=True))
        l = jnp.sum(p, axis=-1, keepdims=True)
        oh = _dot(p.astype(BF16), v_ref[0, :, cols]) / l
        heads.append(oh.astype(BF16))
    o = jnp.concatenate(heads, axis=-1)
    o_ref[0] = x + _dot(o, wo_ref[...])


def _xattn(x, gain, w_q, q_norm, k, v, w_out, tm=512):
    b, s, d = x.shape
    m = k.shape[1]
    xa_w = XA_HEADS * XA_HEAD_DIM
    tm = _row_tile(s, tm)
    return pl.pallas_call(
        _xattn_body,
        grid=(b, s // tm),
        in_specs=[
            pl.BlockSpec((1, tm, d), lambda i, j: (i, j, 0)),
            _const_spec((1, d)),
            _const_spec((d, xa_w)),
            _const_spec((1, XA_HEAD_DIM)),
            pl.BlockSpec((1, m, xa_w), lambda i, j: (i, 0, 0)),
            pl.BlockSpec((1, m, xa_w), lambda i, j: (i, 0, 0)),
            _const_spec((xa_w, d)),
        ],
        out_specs=pl.BlockSpec((1, tm, d), lambda i, j: (i, j, 0)),
        out_shape=jax.ShapeDtypeStruct(x.shape, F32),
        compiler_params=_params("arbitrary", "arbitrary"),
        name="xattn",
    )(x, gain[None, :], w_q.astype(BF16), q_norm[None, :], k, v, w_out.astype(BF16))


def _ffn_body(x_ref, g_ref, wu_ref, wd_ref, o_ref, *, hid_chunk):
    x = x_ref[...]
    xn = _rms(x, g_ref[...]).astype(BF16)
    acc = x
    for c in range(0, wu_ref.shape[1], hid_chunk):
        h = _dot(xn, wu_ref[:, c:c + hid_chunk])
        h = jnp.square(jnp.maximum(h, 0.0)).astype(BF16)
        acc = acc + _dot(h, wd_ref[c:c + hid_chunk, :])
    o_ref[...] = acc


def _ffn(x2, gain, w_up, w_down, tm=512, hid_chunk=1024):
    t, d = x2.shape
    hid = w_up.shape[1]
    tm = _row_tile(t, tm)
    return pl.pallas_call(
        functools.partial(_ffn_body, hid_chunk=hid_chunk),
        grid=(t // tm,),
        in_specs=[
            pl.BlockSpec((tm, d), lambda i: (i, 0)),
            _const_spec((1, d)),
            _const_spec((d, hid)),
            _const_spec((hid, d)),
        ],
        out_specs=pl.BlockSpec((tm, d), lambda i: (i, 0)),
        out_shape=jax.ShapeDtypeStruct(x2.shape, F32),
        compiler_params=_params("arbitrary"),
        name="ffn",
    )(x2, gain[None, :], w_up.astype(BF16), w_down.astype(BF16))


def _proj_res_body(x_ref, y_ref, w_ref, o_ref):
    o_ref[...] = x_ref[...] + _dot(y_ref[...], w_ref[...])


def _proj_res(x2, y2, w, tm=512):
    t, d = x2.shape
    k = y2.shape[1]
    tm = _row_tile(t, tm)
    return pl.pallas_call(
        _proj_res_body,
        grid=(t // tm,),
        in_specs=[
            pl.BlockSpec((tm, d), lambda i: (i, 0)),
            pl.BlockSpec((tm, k), lambda i: (i, 0)),
            _const_spec((k, d)),
        ],
        out_specs=pl.BlockSpec((tm, d), lambda i: (i, 0)),
        out_shape=jax.ShapeDtypeStruct(x2.shape, F32),
        compiler_params=_params("arbitrary"),
        name="proj_res",
    )(x2, y2, w.astype(BF16))


def _gmlp_body(x_ref, g_ref, win_ref, vn_ref, ws_ref, bs_ref, wout_ref, o_ref,
               xn_s, u_s, v_s, h_s, *, col_chunk):
    tm = x_ref.shape[0]
    hid = u_s.shape[1]
    gw = hid // GM_GROUPS
    x = x_ref[...]
    xn_s[...] = _rms(x, g_ref[...]).astype(BF16)
    for c in range(0, hid, col_chunk):
        u_s[:, c:c + col_chunk] = jax.nn.gelu(_dot(xn_s[...], win_ref[:, c:c + col_chunk]))
    ssq = jnp.zeros((tm, 1), F32)
    for c in range(0, hid, col_chunk):
        vc = jax.nn.gelu(_dot(xn_s[...], win_ref[:, hid + c:hid + c + col_chunk]))
        v_s[:, c:c + col_chunk] = vc
        ssq = ssq + jnp.sum(vc * vc, axis=-1, keepdims=True)
    inv = lax.rsqrt(ssq / hid + EPS)
    t_i = lax.broadcasted_iota(I32, (GM_BLOCK, GM_BLOCK), 0)
    s_i = lax.broadcasted_iota(I32, (GM_BLOCK, GM_BLOCK), 1)
    causal = (s_i // CHUNK) <= (t_i // CHUNK)
    for g in range(GM_GROUPS):
        cols = slice(g * gw, (g + 1) * gw)
        wsg = jnp.where(causal, ws_ref[g], 0.0).astype(BF16)
        bias = bs_ref[:, g:g + 1]
        for n in range(tm // GM_BLOCK):
            rows = slice(n * GM_BLOCK, (n + 1) * GM_BLOCK)
            vg = (v_s[rows, cols] * inv[rows] * vn_ref[:, cols]).astype(BF16)
            sg = _dot(wsg, vg) + bias
            h_s[rows, cols] = (u_s[rows, cols] * sg).astype(BF16)
    o_ref[...] = x + _dot(h_s[...], wout_ref[...])


def _gmlp(x2, gain, w_in, v_norm, w_s, b_s, w_out, tm=256, col_chunk=512):
    t, d = x2.shape
    hid = w_out.shape[0]
    tm = _row_tile(t, tm)
    assert tm % GM_BLOCK == 0
    return pl.pallas_call(
        functools.partial(_gmlp_body, col_chunk=col_chunk),
        grid=(t // tm,),
        in_specs=[
            pl.BlockSpec((tm, d), lambda i: (i, 0)),
            _const_spec((1, d)),
            _const_spec((d, 2 * hid)),
            _const_spec((1, hid)),
            _const_spec((GM_GROUPS, GM_BLOCK, GM_BLOCK)),
            _const_spec((GM_BLOCK, GM_GROUPS)),
            _const_spec((hid, d)),
        ],
        out_specs=pl.BlockSpec((tm, d), lambda i: (i, 0)),
        out_shape=jax.ShapeDtypeStruct(x2.shape, F32),
        scratch_shapes=[
            pltpu.VMEM((tm, d), BF16),
            pltpu.VMEM((tm, hid), F32),
            pltpu.VMEM((tm, hid), F32),
            pltpu.VMEM((tm, hid), BF16),
        ],
        compiler_params=_params("arbitrary"),
        name="gmlp",
    )(x2, gain[None, :], w_in.astype(BF16), v_norm[None, :], w_s, b_s.T, w_out.astype(BF16))


def _ssd_in_body(x_ref, g_ref, wz_ref, wx_ref, wdt_ref, z_ref, xbc_ref, dt_ref, *, col_chunk):
    xn = _rms(x_ref[...], g_ref[...]).astype(BF16)
    for c in range(0, wz_ref.shape[1], col_chunk):
        z_ref[:, c:c + col_chunk] = _dot(xn, wz_ref[:, c:c + col_chunk]).astype(BF16)
    for c in range(0, wx_ref.shape[1], col_chunk):
        xbc_ref[:, c:c + col_chunk] = _dot(xn, wx_ref[:, c:c + col_chunk]).astype(BF16)
    dt_ref[...] = _dot(xn, wdt_ref[...])


def _ssd_in(x2, gain, w_z, w_xbc, w_dt, tm=512, col_chunk=512):
    t, d = x2.shape
    tm = _row_tile(t, tm)
    nz, nx, ndt = w_z.shape[1], w_xbc.shape[1], w_dt.shape[1]
    return pl.pallas_call(
        functools.partial(_ssd_in_body, col_chunk=col_chunk),
        grid=(t // tm,),
        in_specs=[
            pl.BlockSpec((tm, d), lambda i: (i, 0)),
            _const_spec((1, d)),
            _const_spec((d, nz)),
            _const_spec((d, nx)),
            _const_spec((d, ndt)),
        ],
        out_specs=[
            pl.BlockSpec((tm, nz), lambda i: (i, 0)),
            pl.BlockSpec((tm, nx), lambda i: (i, 0)),
            pl.BlockSpec((tm, ndt), lambda i: (i, 0)),
        ],
        out_shape=[
            jax.ShapeDtypeStruct((t, nz), BF16),
            jax.ShapeDtypeStruct((t, nx), BF16),
            jax.ShapeDtypeStruct((t, ndt), F32),
        ],
        compiler_params=_params("arbitrary"),
        name="ssd_in",
    )(x2, gain[None, :], w_z, w_xbc, w_dt)


def _ssd_core_body(z_ref, xbc_ref, dt_ref, cw_ref, cb_ref, dtb_ref, alog_ref, dskip_ref, onorm_ref,
                   tri_ref, expand_ref, y_ref, state_s, ext_s):
    q = SSD_CHUNK
    n_state = SSD_STATE
    d_inner = z_ref.shape[2]
    heads_per_group = d_inner // SSD_HEAD_DIM // SSD_GROUPS
    gw = heads_per_group * SSD_HEAD_DIM
    halo = 8

    @pl.when(pl.program_id(1) == 0)
    def _():
        state_s[...] = jnp.zeros_like(state_s)
        ext_s[0:halo, :] = jnp.zeros((halo, ext_s.shape[1]), F32)

    ext_s[halo:halo + q, :] = xbc_ref[0].astype(F32)
    conv = cb_ref[...]
    for k in range(SSD_CONV):
        off = halo - (SSD_CONV - 1) + k
        conv = conv + cw_ref[k:k + 1, :] * ext_s[off:off + q, :]
    ext_s[0:halo, :] = ext_s[q:q + halo, :]
    xbc = conv * jax.nn.sigmoid(conv)
    xs = xbc[:, :d_inner]
    bm = xbc[:, d_inner:d_inner + SSD_GROUPS * n_state].astype(BF16)
    cm = xbc[:, d_inner + SSD_GROUPS * n_state:].astype(BF16)

    dt_raw = dt_ref[0] + dtb_ref[...]
    dt = jnp.maximum(dt_raw, 0.0) + jnp.log1p(jnp.exp(-jnp.abs(dt_raw)))
    a = dt * (-jnp.exp(alog_ref[...]))
    a_cum = _dot_f32_rhs(tri_ref[...], a)
    a_cum_t = a_cum.T
    expand = expand_ref[...]
    dt_e = _dot_f32_lhs(dt, expand)
    acum_e = _dot_f32_lhs(a_cum, expand)
    alast_e = acum_e[q - 1:q, :]
    xdt = xs * dt_e
    xw = (xdt * jnp.exp(alast_e - acum_e)).astype(BF16)
    xdt_b = xdt.astype(BF16)
    ea_e = jnp.exp(acum_e)
    chunk_decay_e = jnp.exp(alast_e)

    l_i = lax.broadcasted_iota(I32, (q, q), 0)
    s_i = lax.broadcasted_iota(I32, (q, q), 1)
    causal = l_i >= s_i
    lane_g = lax.broadcasted_iota(I32, (q, gw), 1) // SSD_HEAD_DIM

    zf = z_ref[0].astype(F32)
    for g in range(SSD_GROUPS):
        gcols = slice(g * gw, (g + 1) * gw)
        ncols = slice(g * n_state, (g + 1) * n_state)
        cg = cm[:, ncols]
        bg = bm[:, ncols]
        cb = _dot_nt(cg, bg)
        st = state_s[g]
        y_off = _dot(cg, st.astype(BF16)) * ea_e[:, gcols]
        ms = []
        xblk = []
        xg = xdt_b[:, gcols]
        for r in range(heads_per_group):
            h = g * heads_per_group + r
            seg = a_cum[:, h:h + 1] - a_cum_t[h:h + 1, :]
            decay = jnp.exp(jnp.where(causal, seg, -jnp.inf))
            ms.append((decay * cb).astype(BF16))
            xblk.append(jnp.where(lane_g == r, xg, jnp.zeros_like(xg)))
        y_diag = _dot(jnp.concatenate(ms, axis=1), jnp.concatenate(xblk, axis=0))
        y = y_diag + y_off + xs[:, gcols] * dskip_ref[:, gcols]
        state_s[g] = st * chunk_decay_e[:, gcols] + _dot_tn(bg, xw[:, gcols])
        zg = zf[:, gcols]
        gated = y * (zg * jax.nn.sigmoid(zg))
        gated = gated * lax.rsqrt(jnp.mean(gated * gated, axis=-1, keepdims=True) + EPS)
        y_ref[0, :, gcols] = (gated * onorm_ref[:, gcols]).astype(BF16)


def _ssd_core(z, xbc, dt, conv_w, conv_b, dt_bias, a_log, d_skip, out_norm):
    b, s, d_inner = z.shape
    conv_ch = xbc.shape[2]
    n_heads = d_inner // SSD_HEAD_DIM
    gw = d_inner // SSD_GROUPS
    q = SSD_CHUNK
    assert s % q == 0 and n_heads <= LANES and dt.shape[2] == LANES

    def pad_heads(v):
        return jnp.pad(v, (0, LANES - n_heads))[None, :]

    tri = (jnp.arange(q)[:, None] >= jnp.arange(q)[None, :]).astype(BF16)
    expand = (jnp.arange(LANES)[:, None] == (jnp.arange(d_inner) // SSD_HEAD_DIM)[None, :]).astype(BF16)
    return pl.pallas_call(
        _ssd_core_body,
        grid=(b, s // q),
        in_specs=[
            pl.BlockSpec((1, q, d_inner), lambda i, j: (i, j, 0)),
            pl.BlockSpec((1, q, conv_ch), lambda i, j: (i, j, 0)),
            pl.BlockSpec((1, q, LANES), lambda i, j: (i, j, 0)),
            _const_spec((SSD_CONV, conv_ch)),
            _const_spec((1, conv_ch)),
            _const_spec((1, LANES)),
            _const_spec((1, LANES)),
            _const_spec((1, d_inner)),
            _const_spec((1, d_inner)),
            _const_spec((q, q)),
            _const_spec((LANES, d_inner)),
        ],
        out_specs=pl.BlockSpec((1, q, d_inner), lambda i, j: (i, j, 0)),
        out_shape=jax.ShapeDtypeStruct((b, s, d_inner), BF16),
        scratch_shapes=[
            pltpu.VMEM((SSD_GROUPS, SSD_STATE, gw), F32),
            pltpu.VMEM((q + 8, conv_ch), F32),
        ],
        compiler_params=_params("arbitrary", "arbitrary"),
        name="ssd_core",
    )(z, xbc, dt, conv_w, conv_b[None, :], pad_heads(dt_bias), pad_heads(a_log),
      jnp.repeat(d_skip, SSD_HEAD_DIM)[None, :], out_norm[None, :], tri, expand)


def _ssd_mixer(x, gain, w_in, conv_w, conv_b, dt_bias, a_log, d_skip, out_norm, w_out):
    b, s, d = x.shape
    d_inner = w_out.shape[0]
    conv_ch = conv_w.shape[1]
    n_heads = d_inner // SSD_HEAD_DIM
    w_in = w_in.astype(BF16)
    w_z = w_in[:, :d_inner]
    w_xbc = w_in[:, d_inner:d_inner + conv_ch]
    w_dt = jnp.pad(w_in[:, d_inner + conv_ch:], ((0, 0), (0, LANES - n_heads)))
    x2 = x.reshape(b * s, d)
    z, xbc, dt = _ssd_in(x2, gain, w_z, w_xbc, w_dt)
    y = _ssd_core(z.reshape(b, s, d_inner), xbc.reshape(b, s, conv_ch), dt.reshape(b, s, LANES),
                  conv_w, conv_b, dt_bias, a_log, d_skip, out_norm)
    return _proj_res(x2, y.reshape(b * s, d_inner), w_out).reshape(b, s, d)


def _halves_rms(x, gain, lane):
    lo = lane < (LANES // 2)
    x2 = x * x
    s_lo = jnp.sum(jnp.where(lo, x2, 0.0), axis=-1, keepdims=True)
    s_hi = jnp.sum(jnp.where(lo, 0.0, x2), axis=-1, keepdims=True)
    ms = jnp.where(lo, s_lo, s_hi) * (2.0 / LANES)
    return x * lax.rsqrt(ms + EPS) * gain


def _rope(x, cos, sin_lo, sin_hi):
    half = DSA_HEAD_DIM // 8
    return x * cos + pltpu.roll(x, LANES - half, 1) * sin_lo + pltpu.roll(x, half, 1) * sin_hi


def _dsa_prep_body(x_ref, g_ref, w_ref, pos_ref, invf_ref, invfk_ref, qn_ref, kn_ref, kin_ref,
                   q_ref, k_ref, v_ref, qi_ref, ki_ref, wi_ref):
    tm = x_ref.shape[1]
    q_w = DSA_HEADS * DSA_HEAD_DIM
    kv_w = DSA_KV_HEADS * DSA_HEAD_DIM
    qi_w = DSA_IDX_HEADS * DSA_IDX_DIM
    half = DSA_HEAD_DIM // 8
    xn = _rms(x_ref[0], g_ref[...]).astype(BF16)
    pos = pos_ref[0].astype(F32)
    lane = lax.broadcasted_iota(I32, (tm, LANES), 1)
    j = lane % DSA_HEAD_DIM

    def tables(invf, jj):
        ang = pos * invf
        c, s = jnp.cos(ang), jnp.sin(ang)
        return c, jnp.where(jj < half, -s, 0.0), jnp.where(jj >= half, s, 0.0)

    cos, sin_lo, sin_hi = tables(invf_ref[...], j)

    def proj(c0, width):
        return _dot(xn, w_ref[:, c0:c0 + width])

    o1, o2, o3, o4 = q_w, q_w + kv_w, q_w + 2 * kv_w, q_w + 2 * kv_w + qi_w
    for c in range(0, q_w, LANES):
        blk = _halves_rms(proj(c, LANES), qn_ref[...], lane)
        q_ref[0, :, c:c + LANES] = (_rope(blk, cos, sin_lo, sin_hi) * (DSA_HEAD_DIM ** -0.5)).astype(BF16)
    for c in range(0, kv_w, LANES):
        blk = _halves_rms(proj(o1 + c, LANES), kn_ref[...], lane)
        k_ref[0, :, c:c + LANES] = _rope(blk, cos, sin_lo, sin_hi).astype(BF16)
    v_ref[0] = proj(o2, kv_w).astype(BF16)
    for c in range(0, qi_w, LANES):
        qi_ref[0, :, c:c + LANES] = _rope(proj(o3 + c, LANES), cos, sin_lo, sin_hi).astype(BF16)
    blk = proj(o4, LANES)
    is_ki = lane < DSA_IDX_DIM
    ms = jnp.sum(jnp.where(is_ki, blk * blk, 0.0), axis=-1, keepdims=True) * (1.0 / DSA_IDX_DIM)
    kin = blk * lax.rsqrt(ms + EPS) * kin_ref[...]
    cos_k, sin_lo_k, sin_hi_k = tables(invfk_ref[...], lane)
    ki_ref[0] = _rope(kin, cos_k, sin_lo_k, sin_hi_k).astype(BF16)
    wi_ref[0] = blk * (DSA_IDX_HEADS ** -0.5 * DSA_IDX_DIM ** -0.5)


def _dsa_prep(x, gain, w_in, positions, q_norm, k_norm, kidx_norm, tm=256):
    b, s, d = x.shape
    q_w = DSA_HEADS * DSA_HEAD_DIM
    kv_w = DSA_KV_HEADS * DSA_HEAD_DIM
    qi_w = DSA_IDX_HEADS * DSA_IDX_DIM
    n_in = w_in.shape[1]
    assert n_in == q_w + 2 * kv_w + qi_w + DSA_IDX_DIM + DSA_IDX_HEADS
    n_pad = q_w + 2 * kv_w + qi_w + LANES
    w = jnp.pad(w_in.astype(BF16), ((0, 0), (0, n_pad - n_in)))
    tm = _row_tile(s, tm)
    half = DSA_HEAD_DIM // 8
    inv_freq = ROPE_THETA ** (-jnp.arange(half, dtype=F32) / half)
    lane = jnp.arange(LANES)
    j = lane % DSA_HEAD_DIM
    invf = jnp.where(j < 2 * half, inv_freq[j % half], 0.0)[None, :]
    invf_k = jnp.where(lane < 2 * half, inv_freq[lane % half], 0.0)[None, :]
    kin = jnp.concatenate([kidx_norm, jnp.ones((LANES - DSA_IDX_DIM,), F32)])[None, :]

    def tok(width, dtype):
        return pl.BlockSpec((1, tm, width), lambda i, t: (i, t, 0)), jax.ShapeDtypeStruct((b, s, width), dtype)

    outs = [tok(q_w, BF16), tok(kv_w, BF16), tok(kv_w, BF16), tok(qi_w, BF16), tok(LANES, BF16), tok(LANES, F32)]
    return pl.pallas_call(
        _dsa_prep_body,
        grid=(b, s // tm),
        in_specs=[
            pl.BlockSpec((1, tm, d), lambda i, t: (i, t, 0)),
            _const_spec((1, d)),
            _const_spec((d, n_pad)),
            pl.BlockSpec((1, tm, 1), lambda i, t: (i, t, 0)),
            _const_spec((1, LANES)),
            _const_spec((1, LANES)),
            _const_spec((1, LANES)),
            _const_spec((1, LANES)),
            _const_spec((1, LANES)),
        ],
        out_specs=[o[0] for o in outs],
        out_shape=[o[1] for o in outs],
        compiler_params=_params("arbitrary", "arbitrary"),
        name="dsa_prep",
    )(x, gain[None, :], w, positions[:, :, None], invf, invf_k,
      jnp.tile(q_norm, 2)[None, :], jnp.tile(k_norm, 2)[None, :], kin)


def _dsa_attn_body(qt_ref, k_ref, vt_ref, qit_ref, ki_ref, wit_ref, o_ref,
                   key_s, mask_s, m_s, l_s, acc_s, s_s, *, topk):
    qb = qt_ref.shape[5]
    tk = qb
    i = pl.program_id(1)
    n_tiles = i + 1
    rep = DSA_HEADS // DSA_KV_HEADS
    krow = lax.broadcasted_iota(I32, (tk, qb), 0)
    qcol = lax.broadcasted_iota(I32, (tk, qb), 1)
    diag_ok = (krow // CHUNK) <= (qcol // CHUNK)

    def score_tile(t, carry):
        kt = ki_ref[0, pl.ds(pl.multiple_of(t * tk, tk), tk), :]
        score = jnp.zeros((tk, qb), F32)
        for h in range(DSA_IDX_HEADS):
            score = score + wit_ref[0, 0, h:h + 1, :] * jnp.maximum(_dot(kt, qit_ref[0, 0, h]), 0.0)
        score = score + 0.0
        bits = pltpu.bitcast(score, I32)
        key = bits ^ ((bits >> 31) & jnp.int32(0x7FFFFFFF))
        key_s[t] = jnp.where((t < i) | diag_ok, key, jnp.int32(INT_MIN))
        return carry

    lax.fori_loop(0, n_tiles, score_tile, 0)

    def count(pred_fn):
        def body(t, acc):
            hit = jnp.where(pred_fn(key_s[t]), 1, 0).astype(I32)
            return acc + jnp.sum(hit.reshape(tk // 8, 8, qb), axis=0)
        acc = lax.fori_loop(0, n_tiles, body, jnp.zeros((8, qb), I32))
        return jnp.sum(acc, axis=0, keepdims=True)

    def radix_step(b, thr):
        cand = thr + lax.shift_left(jnp.int32(1), 31 - b)
        cnt = count(lambda key: key >= cand)
        return jnp.where(cnt >= topk, cand, thr)

    thr = lax.fori_loop(0, 32, radix_step, jnp.full((1, qb), INT_MIN, I32))
    n_ge = count(lambda key: key >= thr)
    has_cut_tie = jnp.max(n_ge) > topk

    @pl.when(jnp.logical_not(has_cut_tie))
    def _():
        def mask_tile(t, carry):
            key = key_s[t]
            mask_s[t] = jnp.where((key >= thr) & (key != jnp.int32(INT_MIN)), 0.0, NEG_BIG)
            return carry
        lax.fori_loop(0, n_tiles, mask_tile, 0)

    @pl.when(has_cut_tie)
    def _():
        n_gt = count(lambda key: key > thr)
        need = (topk - n_gt).astype(F32)
        strict_lower = (qcol < krow).astype(BF16)
        ones = jnp.ones((tk, tk), BF16)

        def mask_tile(t, before):
            key = key_s[t]
            eq = key == thr
            eqb = jnp.where(eq, 1.0, 0.0).astype(BF16)
            rank = before + _dot(strict_lower, eqb)
            sel = (key > thr) | (eq & (rank < need))
            sel = sel & (key != jnp.int32(INT_MIN))
            mask_s[t] = jnp.where(sel, 0.0, NEG_BIG)
            return before + _dot(ones, eqb)

        lax.fori_loop(0, n_tiles, mask_tile, jnp.zeros((tk, qb), F32))

    m_s[...] = jnp.full(m_s.shape, NEG_BIG, F32)
    l_s[...] = jnp.zeros(l_s.shape, F32)
    acc_s[...] = jnp.zeros(acc_s.shape, F32)

    def logits(t, g, r, mask):
        kt = k_ref[0, g, pl.ds(pl.multiple_of(t * tk, tk), tk), :]
        return _dot(kt, qt_ref[0, 0, g, r]) + mask

    def max_tile(t, carry):
        mask = mask_s[t]
        for h in range(DSA_HEADS):
            s = logits(t, h // rep, h % rep, mask)
            m_s[h] = jnp.maximum(m_s[h], jnp.max(s.reshape(tk // 8, 8, qb), axis=0))
        return carry

    lax.fori_loop(0, n_tiles, max_tile, 0)
    for h in range(DSA_HEADS):
        m_s[h] = jnp.broadcast_to(jnp.max(m_s[h], axis=0, keepdims=True), (8, qb))

    def attn_tile(t, carry):
        mask = mask_s[t]
        n_slots = s_s.shape[0]
        ahead = n_slots - 1
        for h in range(ahead):
            s_s[h] = logits(t, h // rep, h % rep, mask)
        for h in range(DSA_HEADS):
            g = h // rep
            nxt = h + ahead
            if nxt < DSA_HEADS:
                s_s[nxt % n_slots] = logits(t, nxt // rep, nxt % rep, mask)
            p = jnp.exp(s_s[h % n_slots] - m_s[h, 0:1, :])
            l_s[h] = l_s[h] + jnp.sum(p.reshape(tk // 8, 8, qb), axis=0)
            acc_s[h] = acc_s[h] + _dot(vt_ref[0, g, t], p.astype(BF16))
        return carry

    lax.fori_loop(0, n_tiles, attn_tile, 0)
    for h in range(DSA_HEADS):
        l = jnp.sum(l_s[h], axis=0, keepdims=True)
        o_ref[0, 0, h // rep, h % rep] = (acc_s[h] / l).astype(BF16)


def _dsa_attn(qt, k, vt, qit, ki, wit, s):
    b, nq, kvh, rep, dh, qb = qt.shape
    topk = min(DSA_TOPK_MAX, s // 4)
    return pl.pallas_call(
        functools.partial(_dsa_attn_body, topk=topk),
        grid=(b, nq),
        in_specs=[
            pl.BlockSpec((1, 1, kvh, rep, dh, qb), lambda i, j: (i, j, 0, 0, 0, 0)),
            pl.BlockSpec((1, kvh, s, dh), lambda i, j: (i, 0, 0, 0)),
            pl.BlockSpec((1, kvh, nq, dh, qb), lambda i, j: (i, 0, 0, 0, 0)),
            pl.BlockSpec((1, 1, DSA_IDX_HEADS, DSA_IDX_DIM, qb), lambda i, j: (i, j, 0, 0, 0)),
            pl.BlockSpec((1, s, DSA_IDX_DIM), lambda i, j: (i, 0, 0)),
            pl.BlockSpec((1, 1, DSA_IDX_HEADS, qb), lambda i, j: (i, j, 0, 0)),
        ],
        out_specs=pl.BlockSpec((1, 1, kvh, rep, dh, qb), lambda i, j: (i, j, 0, 0, 0, 0)),
        out_shape=jax.ShapeDtypeStruct(qt.shape, BF16),
        scratch_shapes=[
            pltpu.VMEM((nq, qb, qb), I32),
            pltpu.VMEM((nq, qb, qb), F32),
            pltpu.VMEM((kvh * rep, 8, qb), F32),
            pltpu.VMEM((kvh * rep, 8, qb), F32),
            pltpu.VMEM((kvh * rep, dh, qb), F32),
            pltpu.VMEM((4, qb, qb), F32),
        ],
        compiler_params=_params("arbitrary", "arbitrary"),
        name="dsa_attn",
    )(qt, k, vt, qit, ki, wit)


def _dsa_mixer(x, positions, gain, w_in, q_norm, k_norm, kidx_norm, w_out, qb=DSA_QTILE):
    b, s, d = x.shape
    kvh, dh = DSA_KV_HEADS, DSA_HEAD_DIM
    rep = DSA_HEADS // kvh
    qb = _row_tile(s, qb)
    nq = s // qb
    q, k, v, qi, kiw, wi = _dsa_prep(x, gain, w_in, positions, q_norm, k_norm, kidx_norm)
    qt = q.reshape(b, nq, qb, kvh, rep, dh).transpose(0, 1, 3, 4, 5, 2)
    kh = k.reshape(b, s, kvh, dh).transpose(0, 2, 1, 3)
    vt = v.reshape(b, nq, qb, kvh, dh).transpose(0, 3, 1, 4, 2)
    qit = qi.reshape(b, nq, qb, DSA_IDX_HEADS, DSA_IDX_DIM).transpose(0, 1, 3, 4, 2)
    ki = kiw[:, :, :DSA_IDX_DIM]
    wit = wi[:, :, DSA_IDX_DIM:DSA_IDX_DIM + DSA_IDX_HEADS].reshape(b, nq, qb, DSA_IDX_HEADS).transpose(0, 1, 3, 2)
    ot = _dsa_attn(qt, kh, vt, qit, ki, wit, s)
    o = ot.transpose(0, 1, 5, 2, 3, 4).reshape(b * s, DSA_HEADS * dh)
    return _proj_res(x.reshape(b * s, d), o, w_out).reshape(b, s, d)


def kernel(x, mem, positions, norm_mix, gm_w_in, gm_v_norm, gm_w_s, gm_b_s, gm_w_out, ssd_w_in, ssd_conv_w, ssd_conv_b, ssd_dt_bias, ssd_a_log, ssd_d, ssd_out_norm, ssd_w_out, dsa_w_in, dsa_q_norm, dsa_k_norm, dsa_kidx_norm, dsa_w_out, norm_xa, norm_mem, xa_w_q, xa_w_kv, xa_q_norm, xa_k_norm, xa_w_out, norm_ffn, ffn_w_up, ffn_w_down):
    b, s, d = x.shape
    depth = norm_mix.shape[0]
    mem_k, mem_v = _mem_kv(mem, norm_mem, xa_w_kv, xa_k_norm)
    for i in range(depth):
        kind, j = i % 3, i // 3
        if kind == 0:
            x = _gmlp(x.reshape(b * s, d), norm_mix[i], gm_w_in[j], gm_v_norm[j], gm_w_s[j], gm_b_s[j],
                      gm_w_out[j]).reshape(b, s, d)
        elif kind == 1:
            x = _ssd_mixer(x, norm_mix[i], ssd_w_in[j], ssd_conv_w[j], ssd_conv_b[j], ssd_dt_bias[j],
                           ssd_a_log[j], ssd_d[j], ssd_out_norm[j], ssd_w_out[j])
        else:
            x = _dsa_mixer(x, positions, norm_mix[i], dsa_w_in[j], dsa_q_norm[j], dsa_k_norm[j],
                           dsa_kidx_norm[j], dsa_w_out[j])
        x = _xattn(x, norm_xa[i], xa_w_q[i], xa_q_norm[i], mem_k[i], mem_v[i], xa_w_out[i])
        x = _ffn(x.reshape(b * s, d), norm_ffn[i], ffn_w_up[i], ffn_w_down[i]).reshape(b, s, d)
    return x
```

```python
import functools

import jax
import jax.numpy as jnp
from jax import lax
from jax.experimental import pallas as pl
from jax.experimental.pallas import tpu as pltpu

F32 = jnp.float32
BF16 = jnp.bfloat16
I32 = jnp.int32

EPS = 1e-6
ROPE_THETA = 500000.0
CHUNK = 64
GM_BLOCK = 128
GM_GROUPS = 8
SSD_HEAD_DIM = 64
SSD_GROUPS = 8
SSD_STATE = 128
SSD_CONV = 4
SSD_CHUNK = 128
DSA_HEADS = 16
DSA_KV_HEADS = 4
DSA_HEAD_DIM = 64
DSA_IDX_HEADS = 8
DSA_IDX_DIM = 64
DSA_TOPK_MAX = 256
DSA_QTILE = 256
DSA_SAFE_LOGIT = 60.0
XA_HEADS = 4
XA_HEAD_DIM = 128

LANES = 128
VMEM_LIMIT_BYTES = 56 * 1024 * 1024
NEG_BIG = -1e30
INT_MIN = -(2 ** 31)


def _params(*semantics):
    return pltpu.CompilerParams(dimension_semantics=semantics, vmem_limit_bytes=VMEM_LIMIT_BYTES)


def _const_spec(shape):
    nd = len(shape)
    return pl.BlockSpec(shape, lambda *_: (0,) * nd, pipeline_mode=pl.Buffered(1))


def _row_tile(n, want):
    t = min(n, want)
    assert n % t == 0, (n, t)
    return t


def _rms(x, gain):
    ms = jnp.mean(x * x, axis=-1, keepdims=True)
    return x * lax.rsqrt(ms + EPS) * gain


def _dot(a, b):
    return jnp.dot(a, b, preferred_element_type=F32)


def _dot_nt(a, b):
    return lax.dot_general(a, b, (((1,), (1,)), ((), ())), preferred_element_type=F32)


def _dot_tn(a, b):
    return lax.dot_general(a, b, (((0,), (0,)), ((), ())), preferred_element_type=F32)


def _split3(a):
    hi = a.astype(BF16)
    r1 = a - hi.astype(F32)
    mid = r1.astype(BF16)
    lo = (r1 - mid.astype(F32)).astype(BF16)
    return hi, mid, lo


def _dot_f32_lhs(a, b_exact):
    hi, mid, lo = _split3(a)
    return _dot(hi, b_exact) + _dot(mid, b_exact) + _dot(lo, b_exact)


def _dot_f32_rhs(a_exact, b):
    hi, mid, lo = _split3(b)
    return _dot(a_exact, hi) + _dot(a_exact, mid) + _dot(a_exact, lo)


def _mem_kv_body(mem_ref, g_ref, w_ref, kn_ref, k_ref, v_ref):
    mn = _rms(mem_ref[0], g_ref[0]).astype(BF16)
    kv = _dot(mn, w_ref[0])
    xa_w = XA_HEADS * XA_HEAD_DIM
    for h in range(XA_HEADS):
        cols = slice(h * XA_HEAD_DIM, (h + 1) * XA_HEAD_DIM)
        k_ref[0, 0, :, cols] = _rms(kv[:, cols], kn_ref[0]).astype(BF16)
    v_ref[0, 0] = kv[:, xa_w:].astype(BF16)


def _mem_kv(mem, norm_mem, w_kv, k_norm):
    depth, d, _ = w_kv.shape
    b, m, _ = mem.shape
    xa_w = XA_HEADS * XA_HEAD_DIM
    out = jax.ShapeDtypeStruct((depth, b, m, xa_w), BF16)
    return pl.pallas_call(
        _mem_kv_body,
        grid=(depth, b),
        in_specs=[
            pl.BlockSpec((1, m, d), lambda i, j: (j, 0, 0)),
            pl.BlockSpec((1, 1, d), lambda i, j: (i, 0, 0)),
            pl.BlockSpec((1, d, 2 * xa_w), lambda i, j: (i, 0, 0)),
            pl.BlockSpec((1, 1, XA_HEAD_DIM), lambda i, j: (i, 0, 0)),
        ],
        out_specs=[pl.BlockSpec((1, 1, m, xa_w), lambda i, j: (i, j, 0, 0))] * 2,
        out_shape=[out, out],
        compiler_params=_params("arbitrary", "arbitrary"),
        name="mem_kv",
    )(mem, norm_mem[:, None, :], w_kv.astype(BF16), k_norm[:, None, :])


def _xattn_body(x_ref, g_ref, wq_ref, qn_ref, k_ref, v_ref, wo_ref, o_ref):
    x = x_ref[0]
    xn = _rms(x, g_ref[...]).astype(BF16)
    q = _dot(xn, wq_ref[...])
    scale = XA_HEAD_DIM ** -0.5
    heads = []
    for h in range(XA_HEADS):
        cols = slice(h * XA_HEAD_DIM, (h + 1) * XA_HEAD_DIM)
        qh = _rms(q[:, cols], qn_ref[...]).astype(BF16)
        s = _dot_nt(qh, k_ref[0, :, cols]) * scale
        p = jnp.exp(s - jnp.max(s, axis=-1, keepdims=True))
        l = jnp.sum(p, axis=-1, keepdims=True)
        oh = _dot(p.astype(BF16), v_ref[0, :, cols]) / l
        heads.append(oh.astype(BF16))
    o = jnp.concatenate(heads, axis=-1)
    o_ref[0] = x + _dot(o, wo_ref[...])


def _xattn(x, gain, w_q, q_norm, k, v, w_out, tm=512):
    b, s, d = x.shape
    m = k.shape[1]
    xa_w = XA_HEADS * XA_HEAD_DIM
    tm = _row_tile(s, tm)
    return pl.pallas_call(
        _xattn_body,
        grid=(b, s // tm),
        in_specs=[
            pl.BlockSpec((1, tm, d), lambda i, j: (i, j, 0)),
            _const_spec((1, d)),
            _const_spec((d, xa_w)),
            _const_spec((1, XA_HEAD_DIM)),
            pl.BlockSpec((1, m, xa_w), lambda i, j: (i, 0, 0)),
            pl.BlockSpec((1, m, xa_w), lambda i, j: (i, 0, 0)),
            _const_spec((xa_w, d)),
        ],
        out_specs=pl.BlockSpec((1, tm, d), lambda i, j: (i, j, 0)),
        out_shape=jax.ShapeDtypeStruct(x.shape, F32),
        compiler_params=_params("arbitrary", "arbitrary"),
        name="xattn",
    )(x, gain[None, :], w_q.astype(BF16), q_norm[None, :], k, v, w_out.astype(BF16))


def _ffn_body(x_ref, g_ref, wu_ref, wd_ref, o_ref, *, hid_chunk):
    x = x_ref[...]
    xn = _rms(x, g_ref[...]).astype(BF16)
    acc = x
    for c in range(0, wu_ref.shape[1], hid_chunk):
        h = _dot(xn, wu_ref[:, c:c + hid_chunk])
        h = jnp.square(jnp.maximum(h, 0.0)).astype(BF16)
        acc = acc + _dot(h, wd_ref[c:c + hid_chunk, :])
    o_ref[...] = acc


def _ffn(x2, gain, w_up, w_down, tm=512, hid_chunk=1024):
    t, d = x2.shape
    hid = w_up.shape[1]
    tm = _row_tile(t, tm)
    return pl.pallas_call(
        functools.partial(_ffn_body, hid_chunk=hid_chunk),
        grid=(t // tm,),
        in_specs=[
            pl.BlockSpec((tm, d), lambda i: (i, 0)),
            _const_spec((1, d)),
            _const_spec((d, hid)),
            _const_spec((hid, d)),
        ],
        out_specs=pl.BlockSpec((tm, d), lambda i: (i, 0)),
        out_shape=jax.ShapeDtypeStruct(x2.shape, F32),
        compiler_params=_params("arbitrary"),
        name="ffn",
    )(x2, gain[None, :], w_up.astype(BF16), w_down.astype(BF16))


def _proj_res_body(x_ref, y_ref, w_ref, o_ref):
    o_ref[...] = x_ref[...] + _dot(y_ref[...], w_ref[...])


def _proj_res(x2, y2, w, tm=512):
    t, d = x2.shape
    k = y2.shape[1]
    tm = _row_tile(t, tm)
    return pl.pallas_call(
        _proj_res_body,
        grid=(t // tm,),
        in_specs=[
            pl.BlockSpec((tm, d), lambda i: (i, 0)),
            pl.BlockSpec((tm, k), lambda i: (i, 0)),
            _const_spec((k, d)),
        ],
        out_specs=pl.BlockSpec((tm, d), lambda i: (i, 0)),
        out_shape=jax.ShapeDtypeStruct(x2.shape, F32),
        compiler_params=_params("arbitrary"),
        name="proj_res",
    )(x2, y2, w.astype(BF16))


def _gmlp_body(x_ref, g_ref, win_ref, vn_ref, ws_ref, bs_ref, wout_ref, o_ref,
               xn_s, u_s, v_s, h_s, *, col_chunk):
    tm = x_ref.shape[0]
    hid = u_s.shape[1]
    gw = hid // GM_GROUPS
    x = x_ref[...]
    xn_s[...] = _rms(x, g_ref[...]).astype(BF16)
    for c in range(0, hid, col_chunk):
        u_s[:, c:c + col_chunk] = jax.nn.gelu(_dot(xn_s[...], win_ref[:, c:c + col_chunk]))
    ssq = jnp.zeros((tm, 1), F32)
    for c in range(0, hid, col_chunk):
        vc = jax.nn.gelu(_dot(xn_s[...], win_ref[:, hid + c:hid + c + col_chunk]))
        v_s[:, c:c + col_chunk] = vc
        ssq = ssq + jnp.sum(vc * vc, axis=-1, keepdims=True)
    inv = lax.rsqrt(ssq / hid + EPS)
    t_i = lax.broadcasted_iota(I32, (GM_BLOCK, GM_BLOCK), 0)
    s_i = lax.broadcasted_iota(I32, (GM_BLOCK, GM_BLOCK), 1)
    causal = (s_i // CHUNK) <= (t_i // CHUNK)
    for g in range(GM_GROUPS):
        cols = slice(g * gw, (g + 1) * gw)
        wsg = jnp.where(causal, ws_ref[g], 0.0).astype(BF16)
        bias = bs_ref[:, g:g + 1]
        for n in range(tm // GM_BLOCK):
            rows = slice(n * GM_BLOCK, (n + 1) * GM_BLOCK)
            vg = (v_s[rows, cols] * inv[rows] * vn_ref[:, cols]).astype(BF16)
            sg = _dot(wsg, vg) + bias
            h_s[rows, cols] = (u_s[rows, cols] * sg).astype(BF16)
    o_ref[...] = x + _dot(h_s[...], wout_ref[...])


def _gmlp(x2, gain, w_in, v_norm, w_s, b_s, w_out, tm=256, col_chunk=512):
    t, d = x2.shape
    hid = w_out.shape[0]
    tm = _row_tile(t, tm)
    assert tm % GM_BLOCK == 0
    return pl.pallas_call(
        functools.partial(_gmlp_body, col_chunk=col_chunk),
        grid=(t // tm,),
        in_specs=[
            pl.BlockSpec((tm, d), lambda i: (i, 0)),
            _const_spec((1, d)),
            _const_spec((d, 2 * hid)),
            _const_spec((1, hid)),
            _const_spec((GM_GROUPS, GM_BLOCK, GM_BLOCK)),
            _const_spec((GM_BLOCK, GM_GROUPS)),
            _const_spec((hid, d)),
        ],
        out_specs=pl.BlockSpec((tm, d), lambda i: (i, 0)),
        out_shape=jax.ShapeDtypeStruct(x2.shape, F32),
        scratch_shapes=[
            pltpu.VMEM((tm, d), BF16),
            pltpu.VMEM((tm, hid), F32),
            pltpu.VMEM((tm, hid), F32),
            pltpu.VMEM((tm, hid), BF16),
        ],
        compiler_params=_params("arbitrary"),
        name="gmlp",
    )(x2, gain[None, :], w_in.astype(BF16), v_norm[None, :], w_s, b_s.T, w_out.astype(BF16))


def _ssd_in_body(x_ref, g_ref, wz_ref, wx_ref, wdt_ref, z_ref, xbc_ref, dt_ref, *, col_chunk):
    xn = _rms(x_ref[...], g_ref[...]).astype(BF16)
    for c in range(0, wz_ref.shape[1], col_chunk):
        z_ref[:, c:c + col_chunk] = _dot(xn, wz_ref[:, c:c + col_chunk]).astype(BF16)
    for c in range(0, wx_ref.shape[1], col_chunk):
        xbc_ref[:, c:c + col_chunk] = _dot(xn, wx_ref[:, c:c + col_chunk]).astype(BF16)
    dt_ref[...] = _dot(xn, wdt_ref[...])


def _ssd_in(x2, gain, w_z, w_xbc, w_dt, tm=512, col_chunk=512):
    t, d = x2.shape
    tm = _row_tile(t, tm)
    nz, nx, ndt = w_z.shape[1], w_xbc.shape[1], w_dt.shape[1]
    return pl.pallas_call(
        functools.partial(_ssd_in_body, col_chunk=col_chunk),
        grid=(t // tm,),
        in_specs=[
            pl.BlockSpec((tm, d), lambda i: (i, 0)),
            _const_spec((1, d)),
            _const_spec((d, nz)),
            _const_spec((d, nx)),
            _const_spec((d, ndt)),
        ],
        out_specs=[
            pl.BlockSpec((tm, nz), lambda i: (i, 0)),
            pl.BlockSpec((tm, nx), lambda i: (i, 0)),
            pl.BlockSpec((tm, ndt), lambda i: (i, 0)),
        ],
        out_shape=[
            jax.ShapeDtypeStruct((t, nz), BF16),
            jax.ShapeDtypeStruct((t, nx), BF16),
            jax.ShapeDtypeStruct((t, ndt), F32),
        ],
        compiler_params=_params("arbitrary"),
        name="ssd_in",
    )(x2, gain[None, :], w_z, w_xbc, w_dt)


def _ssd_core_body(z_ref, xbc_ref, dt_ref, cw_ref, cb_ref, dtb_ref, alog_ref, dskip_ref, onorm_ref,
                   tri_ref, expand_ref, y_ref, state_s, ext_s):
    q = SSD_CHUNK
    n_state = SSD_STATE
    d_inner = z_ref.shape[2]
    heads_per_group = d_inner // SSD_HEAD_DIM // SSD_GROUPS
    gw = heads_per_group * SSD_HEAD_DIM
    halo = 8

    @pl.when(pl.program_id(1) == 0)
    def _():
        state_s[...] = jnp.zeros_like(state_s)
        ext_s[0:halo, :] = jnp.zeros((halo, ext_s.shape[1]), F32)

    ext_s[halo:halo + q, :] = xbc_ref[0].astype(F32)
    conv = cb_ref[...]
    for k in range(SSD_CONV):
        off = halo - (SSD_CONV - 1) + k
        conv = conv + cw_ref[k:k + 1, :] * ext_s[off:off + q, :]
    ext_s[0:halo, :] = ext_s[q:q + halo, :]
    xbc = conv * jax.nn.sigmoid(conv)
    xs = xbc[:, :d_inner]
    bm = xbc[:, d_inner:d_inner + SSD_GROUPS * n_state].astype(BF16)
    cm = xbc[:, d_inner + SSD_GROUPS * n_state:].astype(BF16)

    dt_raw = dt_ref[0] + dtb_ref[...]
    dt = jnp.maximum(dt_raw, 0.0) + jnp.log1p(jnp.exp(-jnp.abs(dt_raw)))
    a = dt * (-jnp.exp(alog_ref[...]))
    a_cum = _dot_f32_rhs(tri_ref[...], a)
    a_cum_t = a_cum.T
    expand = expand_ref[...]
    dt_e = _dot_f32_lhs(dt, expand)
    acum_e = _dot_f32_lhs(a_cum, expand)
    alast_e = acum_e[q - 1:q, :]
    xdt = xs * dt_e
    xw = (xdt * jnp.exp(alast_e - acum_e)).astype(BF16)
    xdt_b = xdt.astype(BF16)
    ea_e = jnp.exp(acum_e)
    chunk_decay_e = jnp.exp(alast_e)

    l_i = lax.broadcasted_iota(I32, (q, q), 0)
    s_i = lax.broadcasted_iota(I32, (q, q), 1)
    causal = l_i >= s_i
    lane_g = lax.broadcasted_iota(I32, (q, gw), 1) // SSD_HEAD_DIM

    zf = z_ref[0].astype(F32)
    for g in range(SSD_GROUPS):
        gcols = slice(g * gw, (g + 1) * gw)
        ncols = slice(g * n_state, (g + 1) * n_state)
        cg = cm[:, ncols]
        bg = bm[:, ncols]
        cb = _dot_nt(cg, bg)
        st = state_s[g]
        y_off = _dot(cg, st.astype(BF16)) * ea_e[:, gcols]
        ms = []
        xblk = []
        xg = xdt_b[:, gcols]
        for r in range(heads_per_group):
            h = g * heads_per_group + r
            seg = a_cum[:, h:h + 1] - a_cum_t[h:h + 1, :]
            decay = jnp.exp(jnp.where(causal, seg, -jnp.inf))
            ms.append((decay * cb).astype(BF16))
            xblk.append(jnp.where(lane_g == r, xg, jnp.zeros_like(xg)))
        y_diag = _dot(jnp.concatenate(ms, axis=1), jnp.concatenate(xblk, axis=0))
        y = y_diag + y_off + xs[:, gcols] * dskip_ref[:, gcols]
        state_s[g] = st * chunk_decay_e[:, gcols] + _dot_tn(bg, xw[:, gcols])
        zg = zf[:, gcols]
        gated = y * (zg * jax.nn.sigmoid(zg))
        gated = gated * lax.rsqrt(jnp.mean(gated * gated, axis=-1, keepdims=True) + EPS)
        y_ref[0, :, gcols] = (gated * onorm_ref[:, gcols]).astype(BF16)


def _ssd_core(z, xbc, dt, conv_w, conv_b, dt_bias, a_log, d_skip, out_norm):
    b, s, d_inner = z.shape
    conv_ch = xbc.shape[2]
    n_heads = d_inner // SSD_HEAD_DIM
    gw = d_inner // SSD_GROUPS
    q = SSD_CHUNK
    assert s % q == 0 and n_heads <= LANES and dt.shape[2] == LANES

    def pad_heads(v):
        return jnp.pad(v, (0, LANES - n_heads))[None, :]

    tri = (jnp.arange(q)[:, None] >= jnp.arange(q)[None, :]).astype(BF16)
    expand = (jnp.arange(LANES)[:, None] == (jnp.arange(d_inner) // SSD_HEAD_DIM)[None, :]).astype(BF16)
    return pl.pallas_call(
        _ssd_core_body,
        grid=(b, s // q),
        in_specs=[
            pl.BlockSpec((1, q, d_inner), lambda i, j: (i, j, 0)),
            pl.BlockSpec((1, q, conv_ch), lambda i, j: (i, j, 0)),
            pl.BlockSpec((1, q, LANES), lambda i, j: (i, j, 0)),
            _const_spec((SSD_CONV, conv_ch)),
            _const_spec((1, conv_ch)),
            _const_spec((1, LANES)),
            _const_spec((1, LANES)),
            _const_spec((1, d_inner)),
            _const_spec((1, d_inner)),
            _const_spec((q, q)),
            _const_spec((LANES, d_inner)),
        ],
        out_specs=pl.BlockSpec((1, q, d_inner), lambda i, j: (i, j, 0)),
        out_shape=jax.ShapeDtypeStruct((b, s, d_inner), BF16),
        scratch_shapes=[
            pltpu.VMEM((SSD_GROUPS, SSD_STATE, gw), F32),
            pltpu.VMEM((q + 8, conv_ch), F32),
        ],
        compiler_params=_params("arbitrary", "arbitrary"),
        name="ssd_core",
    )(z, xbc, dt, conv_w, conv_b[None, :], pad_heads(dt_bias), pad_heads(a_log),
      jnp.repeat(d_skip, SSD_HEAD_DIM)[None, :], out_norm[None, :], tri, expand)


def _ssd_mixer(x, gain, w_in, conv_w, conv_b, dt_bias, a_log, d_skip, out_norm, w_out):
    b, s, d = x.shape
    d_inner = w_out.shape[0]
    conv_ch = conv_w.shape[1]
    n_heads = d_inner // SSD_HEAD_DIM
    w_in = w_in.astype(BF16)
    w_z = w_in[:, :d_inner]
    w_xbc = w_in[:, d_inner:d_inner + conv_ch]
    w_dt = jnp.pad(w_in[:, d_inner + conv_ch:], ((0, 0), (0, LANES - n_heads)))
    x2 = x.reshape(b * s, d)
    z, xbc, dt = _ssd_in(x2, gain, w_z, w_xbc, w_dt)
    y = _ssd_core(z.reshape(b, s, d_inner), xbc.reshape(b, s, conv_ch), dt.reshape(b, s, LANES),
                  conv_w, conv_b, dt_bias, a_log, d_skip, out_norm)
    return _proj_res(x2, y.reshape(b * s, d_inner), w_out).reshape(b, s, d)


def _halves_rms(x, gain, lane):
    lo = lane < (LANES // 2)
    x2 = x * x
    s_lo = jnp.sum(jnp.where(lo, x2, 0.0), axis=-1, keepdims=True)
    s_hi = jnp.sum(jnp.where(lo, 0.0, x2), axis=-1, keepdims=True)
    ms = jnp.where(lo, s_lo, s_hi) * (2.0 / LANES)
    return x * lax.rsqrt(ms + EPS) * gain


def _rope(x, cos, sin_lo, sin_hi):
    half = DSA_HEAD_DIM // 8
    return x * cos + pltpu.roll(x, LANES - half, 1) * sin_lo + pltpu.roll(x, half, 1) * sin_hi


def _dsa_prep_body(x_ref, g_ref, w_ref, pos_ref, invf_ref, invfk_ref, qn_ref, kn_ref, kin_ref,
                   qt_ref, k_ref, vt_ref, qit_ref, kiw_ref, wit_ref):
    tm = x_ref.shape[1]
    rep = DSA_HEADS // DSA_KV_HEADS
    heads_per_block = LANES // DSA_HEAD_DIM
    q_w = DSA_HEADS * DSA_HEAD_DIM
    kv_w = DSA_KV_HEADS * DSA_HEAD_DIM
    qi_w = DSA_IDX_HEADS * DSA_IDX_DIM
    half = DSA_HEAD_DIM // 8
    xn = _rms(x_ref[0], g_ref[...]).astype(BF16)
    pos = pos_ref[0].astype(F32)
    lane = lax.broadcasted_iota(I32, (tm, LANES), 1)
    j = lane % DSA_HEAD_DIM

    def tables(invf, jj):
        ang = pos * invf
        c, s = jnp.cos(ang), jnp.sin(ang)
        return c, jnp.where(jj < half, -s, 0.0), jnp.where(jj >= half, s, 0.0)

    cos, sin_lo, sin_hi = tables(invf_ref[...], j)

    def proj(c0, width):
        return _dot(xn, w_ref[:, c0:c0 + width])

    o1, o2, o3, o4 = q_w, q_w + kv_w, q_w + 2 * kv_w, q_w + 2 * kv_w + qi_w
    zeros = jnp.zeros((DSA_HEAD_DIM, tm), F32)
    for c in range(q_w // LANES):
        blk = _halves_rms(proj(c * LANES, LANES), qn_ref[...], lane)
        bt = (_rope(blk, cos, sin_lo, sin_hi) * (DSA_HEAD_DIM ** -0.5)).T
        for e in range(heads_per_block):
            h = c * heads_per_block + e
            rows = bt[e * DSA_HEAD_DIM:(e + 1) * DSA_HEAD_DIM]
            first_half = (h // rep) % heads_per_block == 0
            padded = jnp.concatenate([rows, zeros] if first_half else [zeros, rows], axis=0)
            qt_ref[0, 0, h] = padded.astype(BF16)
    for c in range(0, kv_w, LANES):
        blk = _halves_rms(proj(o1 + c, LANES), kn_ref[...], lane)
        k_ref[0, :, c:c + LANES] = _rope(blk, cos, sin_lo, sin_hi).astype(BF16)
    vt = proj(o2, kv_w).T
    for g in range(DSA_KV_HEADS):
        vt_ref[0, g, 0] = vt[g * DSA_HEAD_DIM:(g + 1) * DSA_HEAD_DIM].astype(BF16)
    for c in range(qi_w // LANES):
        bt = _rope(proj(o3 + c * LANES, LANES), cos, sin_lo, sin_hi).T
        for e in range(heads_per_block):
            qit_ref[0, 0, c * heads_per_block + e] = bt[e * DSA_IDX_DIM:(e + 1) * DSA_IDX_DIM].astype(BF16)
    blk = proj(o4, LANES)
    is_ki = lane < DSA_IDX_DIM
    ms = jnp.sum(jnp.where(is_ki, blk * blk, 0.0), axis=-1, keepdims=True) * (1.0 / DSA_IDX_DIM)
    kin = blk * lax.rsqrt(ms + EPS) * kin_ref[...]
    cos_k, sin_lo_k, sin_hi_k = tables(invfk_ref[...], lane)
    kiw_ref[0] = _rope(kin, cos_k, sin_lo_k, sin_hi_k).astype(BF16)
    wt = (blk * (DSA_IDX_HEADS ** -0.5 * DSA_IDX_DIM ** -0.5)).T
    wit_ref[0, 0] = wt[DSA_IDX_DIM:DSA_IDX_DIM + DSA_IDX_HEADS]


def _dsa_prep(x, gain, w_in, positions, q_norm, k_norm, kidx_norm, tm):
    b, s, d = x.shape
    q_w = DSA_HEADS * DSA_HEAD_DIM
    kv_w = DSA_KV_HEADS * DSA_HEAD_DIM
    qi_w = DSA_IDX_HEADS * DSA_IDX_DIM
    n_in = w_in.shape[1]
    assert n_in == q_w + 2 * kv_w + qi_w + DSA_IDX_DIM + DSA_IDX_HEADS
    n_pad = q_w + 2 * kv_w + qi_w + LANES
    w = jnp.pad(w_in.astype(BF16), ((0, 0), (0, n_pad - n_in)))
    assert s % tm == 0 and DSA_IDX_DIM + DSA_IDX_HEADS <= LANES and DSA_IDX_DIM % 8 == 0
    nq = s // tm
    half = DSA_HEAD_DIM // 8
    inv_freq = ROPE_THETA ** (-jnp.arange(half, dtype=F32) / half)
    lane = jnp.arange(LANES)
    j = lane % DSA_HEAD_DIM
    invf = jnp.where(j < 2 * half, inv_freq[j % half], 0.0)[None, :]
    invf_k = jnp.where(lane < 2 * half, inv_freq[lane % half], 0.0)[None, :]
    kin = jnp.concatenate([kidx_norm, jnp.ones((LANES - DSA_IDX_DIM,), F32)])[None, :]

    def tok(width, dtype):
        return pl.BlockSpec((1, tm, width), lambda i, t: (i, t, 0)), jax.ShapeDtypeStruct((b, s, width), dtype)

    def per_tile(shape, dtype):
        nd = len(shape)
        return (pl.BlockSpec((1, 1) + shape, lambda i, t: (i, t) + (0,) * nd),
                jax.ShapeDtypeStruct((b, nq) + shape, dtype))

    outs = [
        per_tile((DSA_HEADS, LANES, tm), BF16),
        tok(kv_w, BF16),
        (pl.BlockSpec((1, DSA_KV_HEADS, 1, DSA_HEAD_DIM, tm), lambda i, t: (i, 0, t, 0, 0)),
         jax.ShapeDtypeStruct((b, DSA_KV_HEADS, nq, DSA_HEAD_DIM, tm), BF16)),
        per_tile((DSA_IDX_HEADS, DSA_IDX_DIM, tm), BF16),
        tok(LANES, BF16),
        per_tile((DSA_IDX_HEADS, tm), F32),
    ]
    return pl.pallas_call(
        _dsa_prep_body,
        grid=(b, s // tm),
        in_specs=[
            pl.BlockSpec((1, tm, d), lambda i, t: (i, t, 0)),
            _const_spec((1, d)),
            _const_spec((d, n_pad)),
            pl.BlockSpec((1, tm, 1), lambda i, t: (i, t, 0)),
            _const_spec((1, LANES)),
            _const_spec((1, LANES)),
            _const_spec((1, LANES)),
            _const_spec((1, LANES)),
            _const_spec((1, LANES)),
        ],
        out_specs=[o[0] for o in outs],
        out_shape=[o[1] for o in outs],
        compiler_params=_params("arbitrary", "arbitrary"),
        name="dsa_prep",
    )(x, gain[None, :], w, positions[:, :, None], invf, invf_k,
      jnp.tile(q_norm, 2)[None, :], jnp.tile(k_norm, 2)[None, :], kin)


def _dsa_attn_body(bound_ref, qt_ref, k_ref, vt_ref, qit_ref, kiw_ref, wit_ref, x_ref, wo_ref, o_ref,
                   key_s, mask_s, m_s, l_s, acc_s, s_s, o_s, *, topk):
    qb = qt_ref.shape[4]
    tk = qb
    i = pl.program_id(1)
    n_tiles = i + 1
    rep = DSA_HEADS // DSA_KV_HEADS
    krow = lax.broadcasted_iota(I32, (tk, qb), 0)
    qcol = lax.broadcasted_iota(I32, (tk, qb), 1)
    diag_ok = (krow // CHUNK) <= (qcol // CHUNK)

    def score_tile(t, carry):
        kt = kiw_ref[0, pl.ds(pl.multiple_of(t * tk, tk), tk), 0:DSA_IDX_DIM]
        score = jnp.zeros((tk, qb), F32)
        for h in range(DSA_IDX_HEADS):
            score = score + wit_ref[0, 0, h:h + 1, :] * jnp.maximum(_dot(kt, qit_ref[0, 0, h]), 0.0)
        score = score + 0.0
        bits = pltpu.bitcast(score, I32)
        key = bits ^ ((bits >> 31) & jnp.int32(0x7FFFFFFF))
        key_s[t] = jnp.where((t < i) | diag_ok, key, jnp.int32(INT_MIN))
        return carry

    lax.fori_loop(0, n_tiles, score_tile, 0)

    def count(pred_fn):
        def body(t, acc):
            hit = jnp.where(pred_fn(key_s[t]), 1, 0).astype(I32)
            return acc + jnp.sum(hit.reshape(tk // 8, 8, qb), axis=0)
        acc = lax.fori_loop(0, n_tiles, body, jnp.zeros((8, qb), I32))
        return jnp.sum(acc, axis=0, keepdims=True)

    def radix_step(b, thr):
        cand = thr + lax.shift_left(jnp.int32(1), 31 - b)
        cnt = count(lambda key: key >= cand)
        return jnp.where(cnt >= topk, cand, thr)

    thr = lax.fori_loop(0, 32, radix_step, jnp.full((1, qb), INT_MIN, I32))
    n_ge = count(lambda key: key >= thr)
    has_cut_tie = jnp.max(n_ge) > topk

    @pl.when(jnp.logical_not(has_cut_tie))
    def _():
        def mask_tile(t, carry):
            key = key_s[t]
            mask_s[t] = jnp.where((key >= thr) & (key != jnp.int32(INT_MIN)), 0.0, NEG_BIG)
            return carry
        lax.fori_loop(0, n_tiles, mask_tile, 0)

    @pl.when(has_cut_tie)
    def _():
        n_gt = count(lambda key: key > thr)
        need = (topk - n_gt).astype(F32)
        strict_lower = (qcol < krow).astype(BF16)
        ones = jnp.ones((tk, tk), BF16)

        def mask_tile(t, before):
            key = key_s[t]
            eq = key == thr
            eqb = jnp.where(eq, 1.0, 0.0).astype(BF16)
            rank = before + _dot(strict_lower, eqb)
            sel = (key > thr) | (eq & (rank < need))
            sel = sel & (key != jnp.int32(INT_MIN))
            mask_s[t] = jnp.where(sel, 0.0, NEG_BIG)
            return before + _dot(ones, eqb)

        lax.fori_loop(0, n_tiles, mask_tile, jnp.zeros((tk, qb), F32))

    l_s[...] = jnp.zeros(l_s.shape, F32)
    acc_s[...] = jnp.zeros(acc_s.shape, F32)
    heads_per_block = LANES // DSA_HEAD_DIM

    def logits(t, h, mask):
        kblk = (h // rep) // heads_per_block
        kt = k_ref[0, pl.ds(pl.multiple_of(t * tk, tk), tk), kblk * LANES:(kblk + 1) * LANES]
        return _dot(kt, qt_ref[0, 0, h]) + mask

    bounded = bound_ref[0] <= DSA_SAFE_LOGIT

    @pl.when(bounded)
    def _():
        m_s[...] = jnp.zeros(m_s.shape, F32)

    @pl.when(jnp.logical_not(bounded))
    def _():
        m_s[...] = jnp.full(m_s.shape, NEG_BIG, F32)

        def max_tile(t, carry):
            mask = mask_s[t]
            for h in range(DSA_HEADS):
                s = logits(t, h, mask)
                m_s[h] = jnp.maximum(m_s[h], jnp.max(s.reshape(tk // 8, 8, qb), axis=0))
            return carry

        lax.fori_loop(0, n_tiles, max_tile, 0)
        for h in range(DSA_HEADS):
            m_s[h] = jnp.broadcast_to(jnp.max(m_s[h], axis=0, keepdims=True), (8, qb))

    def attn_tile(t, carry):
        mask = mask_s[t]
        n_slots = s_s.shape[0]
        ahead = n_slots - 1
        for h in range(ahead):
            s_s[h] = logits(t, h, mask)
        for h in range(DSA_HEADS):
            nxt = h + ahead
            if nxt < DSA_HEADS:
                s_s[nxt % n_slots] = logits(t, nxt, mask)
            p = jnp.exp(s_s[h % n_slots] - m_s[h, 0:1, :])
            l_s[h] = l_s[h] + jnp.sum(p.reshape(tk // 8, 8, qb), axis=0)
            acc_s[h] = acc_s[h] + _dot(vt_ref[0, h // rep, t], p.astype(BF16))
        return carry

    lax.fori_loop(0, n_tiles, attn_tile, 0)

    for c in range(DSA_HEADS // heads_per_block):
        pair = []
        for h in range(c * heads_per_block, (c + 1) * heads_per_block):
            pair.append(acc_s[h] / jnp.sum(l_s[h], axis=0, keepdims=True))
        o_s[:, c * LANES:(c + 1) * LANES] = jnp.concatenate(pair, axis=0).T.astype(BF16)
    o_ref[0] = x_ref[0] + _dot(o_s[...], wo_ref[...])


def _dsa_attn(bound, qt, k, vt, qit, kiw, wit, x, w_out):
    b, nq, n_heads, _, qb = qt.shape
    _, s, d = x.shape
    kvh, dh = vt.shape[1], vt.shape[3]
    topk = min(DSA_TOPK_MAX, s // 4)
    return pl.pallas_call(
        functools.partial(_dsa_attn_body, topk=topk),
        grid=(b, nq),
        in_specs=[
            pl.BlockSpec(memory_space=pltpu.SMEM),
            pl.BlockSpec((1, 1, n_heads, LANES, qb), lambda i, j: (i, j, 0, 0, 0)),
            pl.BlockSpec((1, s, kvh * dh), lambda i, j: (i, 0, 0)),
            pl.BlockSpec((1, kvh, nq, dh, qb), lambda i, j: (i, 0, 0, 0, 0)),
            pl.BlockSpec((1, 1, DSA_IDX_HEADS, DSA_IDX_DIM, qb), lambda i, j: (i, j, 0, 0, 0)),
            pl.BlockSpec((1, s, LANES), lambda i, j: (i, 0, 0)),
            pl.BlockSpec((1, 1, DSA_IDX_HEADS, qb), lambda i, j: (i, j, 0, 0)),
            pl.BlockSpec((1, qb, d), lambda i, j: (i, j, 0)),
            _const_spec((n_heads * dh, d)),
        ],
        out_specs=pl.BlockSpec((1, qb, d), lambda i, j: (i, j, 0)),
        out_shape=jax.ShapeDtypeStruct(x.shape, F32),
        scratch_shapes=[
            pltpu.VMEM((nq, qb, qb), I32),
            pltpu.VMEM((nq, qb, qb), F32),
            pltpu.VMEM((n_heads, 8, qb), F32),
            pltpu.VMEM((n_heads, 8, qb), F32),
            pltpu.VMEM((n_heads, dh, qb), F32),
            pltpu.VMEM((4, qb, qb), F32),
            pltpu.VMEM((qb, n_heads * dh), BF16),
        ],
        compiler_params=_params("arbitrary", "arbitrary"),
        name="dsa_attn",
    )(bound, qt, k, vt, qit, kiw, wit, x, w_out.astype(BF16))


def _dsa_mixer(x, positions, gain, w_in, q_norm, k_norm, kidx_norm, w_out, qb=DSA_QTILE):
    s = x.shape[1]
    qb = _row_tile(s, qb)
    qt, k, vt, qit, kiw, wit = _dsa_prep(x, gain, w_in, positions, q_norm, k_norm, kidx_norm, qb)
    bound = (DSA_HEAD_DIM ** 0.5) * jnp.max(jnp.abs(q_norm)) * jnp.max(jnp.abs(k_norm)) * (1.0 + 2.0 ** -6)
    return _dsa_attn(bound.reshape(1), qt, k, vt, qit, kiw, wit, x, w_out)


def kernel(x, mem, positions, norm_mix, gm_w_in, gm_v_norm, gm_w_s, gm_b_s, gm_w_out, ssd_w_in, ssd_conv_w, ssd_conv_b, ssd_dt_bias, ssd_a_log, ssd_d, ssd_out_norm, ssd_w_out, dsa_w_in, dsa_q_norm, dsa_k_norm, dsa_kidx_norm, dsa_w_out, norm_xa, norm_mem, xa_w_q, xa_w_kv, xa_q_norm, xa_k_norm, xa_w_out, norm_ffn, ffn_w_up, ffn_w_down):
    b, s, d = x.shape
    depth = norm_mix.shape[0]
    mem_k, mem_v = _mem_kv(mem, norm_mem, xa_w_kv, xa_k_norm)
    for i in range(depth):
        kind, j = i % 3, i // 3
        if kind == 0:
            x = _gmlp(x.reshape(b * s, d), norm_mix[i], gm_w_in[j], gm_v_norm[j], gm_w_s[j], gm_b_s[j],
                      gm_w_out[j]).reshape(b, s, d)
        elif kind == 1:
            x = _ssd_mixer(x, norm_mix[i], ssd_w_in[j], ssd_conv_w[j], ssd_conv_b[j], ssd_dt_bias[j],
                           ssd_a_log[j], ssd_d[j], ssd_out_norm[j], ssd_w_out[j])
        else:
            x = _dsa_mixer(x, positions, norm_mix[i], dsa_w_in[j], dsa_q_norm[j], dsa_k_norm[j],
                           dsa_kidx_norm[j], dsa_w_out[j])
        x = _xattn(x, norm_xa[i], xa_w_q[i], xa_q_norm[i], mem_k[i], mem_v[i], xa_w_out[i])
        x = _ffn(x.reshape(b * s, d), norm_ffn[i], ffn_w_up[i], ffn_w_down[i]).reshape(b, s, d)
    return x
```

```python
import functools

import jax
import jax.numpy as jnp
from jax import lax
from jax.experimental import pallas as pl
from jax.experimental.pallas import tpu as pltpu

F32 = jnp.float32
BF16 = jnp.bfloat16
I32 = jnp.int32

EPS = 1e-6
ROPE_THETA = 500000.0
CHUNK = 64
GM_BLOCK = 128
GM_GROUPS = 8
SSD_HEAD_DIM = 64
SSD_GROUPS = 8
SSD_STATE = 128
SSD_CONV = 4
SSD_CHUNK = 128
DSA_HEADS = 16
DSA_KV_HEADS = 4
DSA_HEAD_DIM = 64
DSA_IDX_HEADS = 8
DSA_IDX_DIM = 64
DSA_TOPK_MAX = 256
DSA_QTILE = 256
DSA_SAFE_LOGIT = 60.0
XA_HEADS = 4
XA_HEAD_DIM = 128

LANES = 128
VMEM_LIMIT_BYTES = 56 * 1024 * 1024
NEG_BIG = -1e30
INT_MIN = -(2 ** 31)
F32_MIN_NORMAL = 2.0 ** -126


def _params(*semantics):
    return pltpu.CompilerParams(dimension_semantics=semantics, vmem_limit_bytes=VMEM_LIMIT_BYTES)


def _const_spec(shape):
    nd = len(shape)
    return pl.BlockSpec(shape, lambda *_: (0,) * nd, pipeline_mode=pl.Buffered(1))


def _row_tile(n, want):
    t = min(n, want)
    assert n % t == 0, (n, t)
    return t


def _rms(x, gain):
    ms = jnp.mean(x * x, axis=-1, keepdims=True)
    return x * lax.rsqrt(ms + EPS) * gain


def _dot(a, b):
    return jnp.dot(a, b, preferred_element_type=F32)


def _dot_nt(a, b):
    return lax.dot_general(a, b, (((1,), (1,)), ((), ())), preferred_element_type=F32)


def _dot_tn(a, b):
    return lax.dot_general(a, b, (((0,), (0,)), ((), ())), preferred_element_type=F32)


def _split3(a):
    hi = a.astype(BF16)
    r1 = a - hi.astype(F32)
    mid = r1.astype(BF16)
    lo = (r1 - mid.astype(F32)).astype(BF16)
    return hi, mid, lo


def _dot_f32_lhs(a, b_exact):
    hi, mid, lo = _split3(a)
    return _dot(hi, b_exact) + _dot(mid, b_exact) + _dot(lo, b_exact)


def _dot_f32_rhs(a_exact, b):
    hi, mid, lo = _split3(b)
    return _dot(a_exact, hi) + _dot(a_exact, mid) + _dot(a_exact, lo)


def _mem_kv_body(mem_ref, g_ref, w_ref, kn_ref, k_ref, v_ref):
    mn = _rms(mem_ref[0], g_ref[0]).astype(BF16)
    kv = _dot(mn, w_ref[0])
    xa_w = XA_HEADS * XA_HEAD_DIM
    for h in range(XA_HEADS):
        cols = slice(h * XA_HEAD_DIM, (h + 1) * XA_HEAD_DIM)
        k_ref[0, 0, :, cols] = _rms(kv[:, cols], kn_ref[0]).astype(BF16)
    v_ref[0, 0] = kv[:, xa_w:].astype(BF16)


def _mem_kv(mem, norm_mem, w_kv, k_norm):
    depth, d, _ = w_kv.shape
    b, m, _ = mem.shape
    xa_w = XA_HEADS * XA_HEAD_DIM
    out = jax.ShapeDtypeStruct((depth, b, m, xa_w), BF16)
    return pl.pallas_call(
        _mem_kv_body,
        grid=(depth, b),
        in_specs=[
            pl.BlockSpec((1, m, d), lambda i, j: (j, 0, 0)),
            pl.BlockSpec((1, 1, d), lambda i, j: (i, 0, 0)),
            pl.BlockSpec((1, d, 2 * xa_w), lambda i, j: (i, 0, 0)),
            pl.BlockSpec((1, 1, XA_HEAD_DIM), lambda i, j: (i, 0, 0)),
        ],
        out_specs=[pl.BlockSpec((1, 1, m, xa_w), lambda i, j: (i, j, 0, 0))] * 2,
        out_shape=[out, out],
        compiler_params=_params("arbitrary", "arbitrary"),
        name="mem_kv",
    )(mem, norm_mem[:, None, :], w_kv.astype(BF16), k_norm[:, None, :])


def _xattn_ffn_body(x_ref, g_ref, wq_ref, qn_ref, k_ref, v_ref, wo_ref, gf_ref, wu_ref, wd_ref, o_ref,
                    *, hid_chunk):
    x = x_ref[0]
    xn = _rms(x, g_ref[...]).astype(BF16)
    q = _dot(xn, wq_ref[...])
    scale = XA_HEAD_DIM ** -0.5
    heads = []
    for h in range(XA_HEADS):
        cols = slice(h * XA_HEAD_DIM, (h + 1) * XA_HEAD_DIM)
        qh = _rms(q[:, cols], qn_ref[...]).astype(BF16)
        s = _dot_nt(qh, k_ref[0, :, cols]) * scale
        p = jnp.exp(s - jnp.max(s, axis=-1, keepdims=True))
        l = jnp.sum(p, axis=-1, keepdims=True)
        oh = _dot(p.astype(BF16), v_ref[0, :, cols]) / l
        heads.append(oh.astype(BF16))
    o = jnp.concatenate(heads, axis=-1)
    x = x + _dot(o, wo_ref[...])
    xn = _rms(x, gf_ref[...]).astype(BF16)
    acc = x
    for c in range(0, wu_ref.shape[1], hid_chunk):
        h = _dot(xn, wu_ref[:, c:c + hid_chunk])
        h = jnp.square(jnp.maximum(h, 0.0)).astype(BF16)
        acc = acc + _dot(h, wd_ref[c:c + hid_chunk, :])
    o_ref[0] = acc


def _xattn_ffn(x, gain_xa, w_q, q_norm, k, v, w_out, gain_ffn, w_up, w_down, tm=512, hid_chunk=1024):
    b, s, d = x.shape
    m = k.shape[1]
    xa_w = XA_HEADS * XA_HEAD_DIM
    hid = w_up.shape[1]
    tm = _row_tile(s, tm)
    return pl.pallas_call(
        functools.partial(_xattn_ffn_body, hid_chunk=hid_chunk),
        grid=(b, s // tm),
        in_specs=[
            pl.BlockSpec((1, tm, d), lambda i, j: (i, j, 0)),
            _const_spec((1, d)),
            _const_spec((d, xa_w)),
            _const_spec((1, XA_HEAD_DIM)),
            pl.BlockSpec((1, m, xa_w), lambda i, j: (i, 0, 0)),
            pl.BlockSpec((1, m, xa_w), lambda i, j: (i, 0, 0)),
            _const_spec((xa_w, d)),
            _const_spec((1, d)),
            _const_spec((d, hid)),
            _const_spec((hid, d)),
        ],
        out_specs=pl.BlockSpec((1, tm, d), lambda i, j: (i, j, 0)),
        out_shape=jax.ShapeDtypeStruct(x.shape, F32),
        compiler_params=_params("arbitrary", "arbitrary"),
        name="xattn_ffn",
    )(x, gain_xa[None, :], w_q.astype(BF16), q_norm[None, :], k, v, w_out.astype(BF16),
      gain_ffn[None, :], w_up.astype(BF16), w_down.astype(BF16))


def _proj_res_body(x_ref, y_ref, w_ref, o_ref):
    o_ref[...] = x_ref[...] + _dot(y_ref[...], w_ref[...])


def _proj_res(x2, y2, w, tm=512):
    t, d = x2.shape
    k = y2.shape[1]
    tm = _row_tile(t, tm)
    return pl.pallas_call(
        _proj_res_body,
        grid=(t // tm,),
        in_specs=[
            pl.BlockSpec((tm, d), lambda i: (i, 0)),
            pl.BlockSpec((tm, k), lambda i: (i, 0)),
            _const_spec((k, d)),
        ],
        out_specs=pl.BlockSpec((tm, d), lambda i: (i, 0)),
        out_shape=jax.ShapeDtypeStruct(x2.shape, F32),
        compiler_params=_params("arbitrary"),
        name="proj_res",
    )(x2, y2, w.astype(BF16))


def _gmlp_body(x_ref, g_ref, win_ref, vn_ref, ws_ref, bs_ref, wout_ref, o_ref,
               xn_s, u_s, v_s, h_s, *, col_chunk):
    tm = x_ref.shape[0]
    hid = u_s.shape[1]
    gw = hid // GM_GROUPS
    x = x_ref[...]
    xn_s[...] = _rms(x, g_ref[...]).astype(BF16)
    for c in range(0, hid, col_chunk):
        u_s[:, c:c + col_chunk] = jax.nn.gelu(_dot(xn_s[...], win_ref[:, c:c + col_chunk]))
    ssq = jnp.zeros((tm, 1), F32)
    for c in range(0, hid, col_chunk):
        vc = jax.nn.gelu(_dot(xn_s[...], win_ref[:, hid + c:hid + c + col_chunk]))
        v_s[:, c:c + col_chunk] = vc
        ssq = ssq + jnp.sum(vc * vc, axis=-1, keepdims=True)
    inv = lax.rsqrt(ssq / hid + EPS)
    t_i = lax.broadcasted_iota(I32, (GM_BLOCK, GM_BLOCK), 0)
    s_i = lax.broadcasted_iota(I32, (GM_BLOCK, GM_BLOCK), 1)
    causal = (s_i // CHUNK) <= (t_i // CHUNK)
    for g in range(GM_GROUPS):
        cols = slice(g * gw, (g + 1) * gw)
        wsg = jnp.where(causal, ws_ref[g], 0.0).astype(BF16)
        bias = bs_ref[:, g:g + 1]
        for n in range(tm // GM_BLOCK):
            rows = slice(n * GM_BLOCK, (n + 1) * GM_BLOCK)
            vg = (v_s[rows, cols] * inv[rows] * vn_ref[:, cols]).astype(BF16)
            sg = _dot(wsg, vg) + bias
            h_s[rows, cols] = (u_s[rows, cols] * sg).astype(BF16)
    o_ref[...] = x + _dot(h_s[...], wout_ref[...])


def _gmlp(x2, gain, w_in, v_norm, w_s, b_s, w_out, tm=512, col_chunk=512):
    t, d = x2.shape
    hid = w_out.shape[0]
    tm = _row_tile(t, tm)
    assert tm % GM_BLOCK == 0
    return pl.pallas_call(
        functools.partial(_gmlp_body, col_chunk=col_chunk),
        grid=(t // tm,),
        in_specs=[
            pl.BlockSpec((tm, d), lambda i: (i, 0)),
            _const_spec((1, d)),
            _const_spec((d, 2 * hid)),
            _const_spec((1, hid)),
            _const_spec((GM_GROUPS, GM_BLOCK, GM_BLOCK)),
            _const_spec((GM_BLOCK, GM_GROUPS)),
            _const_spec((hid, d)),
        ],
        out_specs=pl.BlockSpec((tm, d), lambda i: (i, 0)),
        out_shape=jax.ShapeDtypeStruct(x2.shape, F32),
        scratch_shapes=[
            pltpu.VMEM((tm, d), BF16),
            pltpu.VMEM((tm, hid), F32),
            pltpu.VMEM((tm, hid), F32),
            pltpu.VMEM((tm, hid), BF16),
        ],
        compiler_params=_params("arbitrary"),
        name="gmlp",
    )(x2, gain[None, :], w_in.astype(BF16), v_norm[None, :], w_s, b_s.T, w_out.astype(BF16))


def _ssd_in_body(x_ref, g_ref, wz_ref, wx_ref, wdt_ref, z_ref, xbc_ref, dt_ref, *, col_chunk):
    xn = _rms(x_ref[...], g_ref[...]).astype(BF16)
    for c in range(0, wz_ref.shape[1], col_chunk):
        z_ref[:, c:c + col_chunk] = _dot(xn, wz_ref[:, c:c + col_chunk]).astype(BF16)
    for c in range(0, wx_ref.shape[1], col_chunk):
        xbc_ref[:, c:c + col_chunk] = _dot(xn, wx_ref[:, c:c + col_chunk]).astype(BF16)
    dt_ref[...] = _dot(xn, wdt_ref[...])


def _ssd_in(x2, gain, w_z, w_xbc, w_dt, tm=512, col_chunk=512):
    t, d = x2.shape
    tm = _row_tile(t, tm)
    nz, nx, ndt = w_z.shape[1], w_xbc.shape[1], w_dt.shape[1]
    return pl.pallas_call(
        functools.partial(_ssd_in_body, col_chunk=col_chunk),
        grid=(t // tm,),
        in_specs=[
            pl.BlockSpec((tm, d), lambda i: (i, 0)),
            _const_spec((1, d)),
            _const_spec((d, nz)),
            _const_spec((d, nx)),
            _const_spec((d, ndt)),
        ],
        out_specs=[
            pl.BlockSpec((tm, nz), lambda i: (i, 0)),
            pl.BlockSpec((tm, nx), lambda i: (i, 0)),
            pl.BlockSpec((tm, ndt), lambda i: (i, 0)),
        ],
        out_shape=[
            jax.ShapeDtypeStruct((t, nz), BF16),
            jax.ShapeDtypeStruct((t, nx), BF16),
            jax.ShapeDtypeStruct((t, ndt), F32),
        ],
        compiler_params=_params("arbitrary"),
        name="ssd_in",
    )(x2, gain[None, :], w_z, w_xbc, w_dt)


def _ssd_core_body(z_ref, xbc_ref, dt_ref, cw_ref, cb_ref, dtb_ref, alog_ref, dskip_ref, onorm_ref,
                   tri_ref, expand_ref, y_ref, state_s, ext_s):
    q = SSD_CHUNK
    n_state = SSD_STATE
    d_inner = z_ref.shape[2]
    heads_per_group = d_inner // SSD_HEAD_DIM // SSD_GROUPS
    gw = heads_per_group * SSD_HEAD_DIM
    halo = 8

    @pl.when(pl.program_id(1) == 0)
    def _():
        state_s[...] = jnp.zeros_like(state_s)
        ext_s[0:halo, :] = jnp.zeros((halo, ext_s.shape[1]), F32)

    ext_s[halo:halo + q, :] = xbc_ref[0].astype(F32)
    conv = cb_ref[...]
    for k in range(SSD_CONV):
        off = halo - (SSD_CONV - 1) + k
        conv = conv + cw_ref[k:k + 1, :] * ext_s[off:off + q, :]
    ext_s[0:halo, :] = ext_s[q:q + halo, :]
    xbc = conv * jax.nn.sigmoid(conv)
    xs = xbc[:, :d_inner]
    bm = xbc[:, d_inner:d_inner + SSD_GROUPS * n_state].astype(BF16)
    cm = xbc[:, d_inner + SSD_GROUPS * n_state:].astype(BF16)

    dt_raw = dt_ref[0] + dtb_ref[...]
    dt = jnp.maximum(dt_raw, 0.0) + jnp.log1p(jnp.exp(-jnp.abs(dt_raw)))
    a = dt * (-jnp.exp(alog_ref[...]))
    a_cum = _dot_f32_rhs(tri_ref[...], a)
    a_cum_t = a_cum.T
    expand = expand_ref[...]
    dt_e = _dot_f32_lhs(dt, expand)
    acum_e = _dot_f32_lhs(a_cum, expand)
    alast_e = acum_e[q - 1:q, :]
    xdt = xs * dt_e
    xw = (xdt * jnp.exp(alast_e - acum_e)).astype(BF16)
    xdt_b = xdt.astype(BF16)
    ea_e = jnp.exp(acum_e)
    chunk_decay_e = jnp.exp(alast_e)

    l_i = lax.broadcasted_iota(I32, (q, q), 0)
    s_i = lax.broadcasted_iota(I32, (q, q), 1)
    causal = l_i >= s_i
    lane_g = lax.broadcasted_iota(I32, (q, gw), 1) // SSD_HEAD_DIM

    zf = z_ref[0].astype(F32)
    for g in range(SSD_GROUPS):
        gcols = slice(g * gw, (g + 1) * gw)
        ncols = slice(g * n_state, (g + 1) * n_state)
        cg = cm[:, ncols]
        bg = bm[:, ncols]
        cb = _dot_nt(cg, bg)
        st = state_s[g]
        y_off = _dot(cg, st.astype(BF16)) * ea_e[:, gcols]
        ms = []
        xblk = []
        xg = xdt_b[:, gcols]
        for r in range(heads_per_group):
            h = g * heads_per_group + r
            seg = a_cum[:, h:h + 1] - a_cum_t[h:h + 1, :]
            decay = jnp.exp(jnp.where(causal, seg, -jnp.inf))
            ms.append((decay * cb).astype(BF16))
            xblk.append(jnp.where(lane_g == r, xg, jnp.zeros_like(xg)))
        y_diag = _dot(jnp.concatenate(ms, axis=1), jnp.concatenate(xblk, axis=0))
        y = y_diag + y_off + xs[:, gcols] * dskip_ref[:, gcols]
        state_s[g] = st * chunk_decay_e[:, gcols] + _dot_tn(bg, xw[:, gcols])
        zg = zf[:, gcols]
        gated = y * (zg * jax.nn.sigmoid(zg))
        gated = gated * lax.rsqrt(jnp.mean(gated * gated, axis=-1, keepdims=True) + EPS)
        y_ref[0, :, gcols] = (gated * onorm_ref[:, gcols]).astype(BF16)


def _ssd_core(z, xbc, dt, conv_w, conv_b, dt_bias, a_log, d_skip, out_norm):
    b, s, d_inner = z.shape
    conv_ch = xbc.shape[2]
    n_heads = d_inner // SSD_HEAD_DIM
    gw = d_inner // SSD_GROUPS
    q = SSD_CHUNK
    assert s % q == 0 and n_heads <= LANES and dt.shape[2] == LANES

    def pad_heads(v):
        return jnp.pad(v, (0, LANES - n_heads))[None, :]

    tri = (jnp.arange(q)[:, None] >= jnp.arange(q)[None, :]).astype(BF16)
    expand = (jnp.arange(LANES)[:, None] == (jnp.arange(d_inner) // SSD_HEAD_DIM)[None, :]).astype(BF16)
    return pl.pallas_call(
        _ssd_core_body,
        grid=(b, s // q),
        in_specs=[
            pl.BlockSpec((1, q, d_inner), lambda i, j: (i, j, 0)),
            pl.BlockSpec((1, q, conv_ch), lambda i, j: (i, j, 0)),
            pl.BlockSpec((1, q, LANES), lambda i, j: (i, j, 0)),
            _const_spec((SSD_CONV, conv_ch)),
            _const_spec((1, conv_ch)),
            _const_spec((1, LANES)),
            _const_spec((1, LANES)),
            _const_spec((1, d_inner)),
            _const_spec((1, d_inner)),
            _const_spec((q, q)),
            _const_spec((LANES, d_inner)),
        ],
        out_specs=pl.BlockSpec((1, q, d_inner), lambda i, j: (i, j, 0)),
        out_shape=jax.ShapeDtypeStruct((b, s, d_inner), BF16),
        scratch_shapes=[
            pltpu.VMEM((SSD_GROUPS, SSD_STATE, gw), F32),
            pltpu.VMEM((q + 8, conv_ch), F32),
        ],
        compiler_params=_params("arbitrary", "arbitrary"),
        name="ssd_core",
    )(z, xbc, dt, conv_w, conv_b[None, :], pad_heads(dt_bias), pad_heads(a_log),
      jnp.repeat(d_skip, SSD_HEAD_DIM)[None, :], out_norm[None, :], tri, expand)


def _ssd_mixer(x, gain, w_in, conv_w, conv_b, dt_bias, a_log, d_skip, out_norm, w_out):
    b, s, d = x.shape
    d_inner = w_out.shape[0]
    conv_ch = conv_w.shape[1]
    n_heads = d_inner // SSD_HEAD_DIM
    w_in = w_in.astype(BF16)
    w_z = w_in[:, :d_inner]
    w_xbc = w_in[:, d_inner:d_inner + conv_ch]
    w_dt = jnp.pad(w_in[:, d_inner + conv_ch:], ((0, 0), (0, LANES - n_heads)))
    x2 = x.reshape(b * s, d)
    z, xbc, dt = _ssd_in(x2, gain, w_z, w_xbc, w_dt)
    y = _ssd_core(z.reshape(b, s, d_inner), xbc.reshape(b, s, conv_ch), dt.reshape(b, s, LANES),
                  conv_w, conv_b, dt_bias, a_log, d_skip, out_norm)
    return _proj_res(x2, y.reshape(b * s, d_inner), w_out).reshape(b, s, d)


def _halves_rms(x, gain, lane):
    lo = lane < (LANES // 2)
    x2 = x * x
    s_lo = jnp.sum(jnp.where(lo, x2, 0.0), axis=-1, keepdims=True)
    s_hi = jnp.sum(jnp.where(lo, 0.0, x2), axis=-1, keepdims=True)
    ms = jnp.where(lo, s_lo, s_hi) * (2.0 / LANES)
    return x * lax.rsqrt(ms + EPS) * gain


def _rope(x, cos, sin_lo, sin_hi):
    half = DSA_HEAD_DIM // 8
    return x * cos + pltpu.roll(x, LANES - half, 1) * sin_lo + pltpu.roll(x, half, 1) * sin_hi


def _dsa_prep_body(x_ref, g_ref, w_ref, pos_ref, invf_ref, qn_ref, kn_ref, kin_ref,
                   qt_ref, k_ref, vt_ref, qit_ref, kiw_ref, wit_ref):
    tm = x_ref.shape[1]
    rep = DSA_HEADS // DSA_KV_HEADS
    heads_per_block = LANES // DSA_HEAD_DIM
    q_w = DSA_HEADS * DSA_HEAD_DIM
    kv_w = DSA_KV_HEADS * DSA_HEAD_DIM
    qi_w = DSA_IDX_HEADS * DSA_IDX_DIM
    half = DSA_HEAD_DIM // 8
    xn = _rms(x_ref[0], g_ref[...]).astype(BF16)
    pos = pos_ref[0].astype(F32)
    lane = lax.broadcasted_iota(I32, (tm, LANES), 1)
    j = lane % DSA_HEAD_DIM

    ang = pos * invf_ref[...]
    cos, sin = jnp.cos(ang), jnp.sin(ang)
    sin_lo, sin_hi = jnp.where(j < half, -sin, 0.0), jnp.where(j >= half, sin, 0.0)

    def proj(c0, width):
        return _dot(xn, w_ref[:, c0:c0 + width])

    o1, o2, o3, o4 = q_w, q_w + kv_w, q_w + 2 * kv_w, q_w + 2 * kv_w + qi_w
    zeros = jnp.zeros((DSA_HEAD_DIM, tm), F32)
    for c in range(q_w // LANES):
        blk = _halves_rms(proj(c * LANES, LANES), qn_ref[...], lane)
        bt = (_rope(blk, cos, sin_lo, sin_hi) * (DSA_HEAD_DIM ** -0.5)).T
        for e in range(heads_per_block):
            h = c * heads_per_block + e
            rows = bt[e * DSA_HEAD_DIM:(e + 1) * DSA_HEAD_DIM]
            first_half = (h // rep) % heads_per_block == 0
            padded = jnp.concatenate([rows, zeros] if first_half else [zeros, rows], axis=0)
            qt_ref[0, 0, h] = padded.astype(BF16)
    for c in range(0, kv_w, LANES):
        blk = _halves_rms(proj(o1 + c, LANES), kn_ref[...], lane)
        k_ref[0, :, c:c + LANES] = _rope(blk, cos, sin_lo, sin_hi).astype(BF16)
    vt = proj(o2, kv_w).T
    for g in range(DSA_KV_HEADS):
        vt_ref[0, g, 0] = vt[g * DSA_HEAD_DIM:(g + 1) * DSA_HEAD_DIM].astype(BF16)
    for c in range(qi_w // LANES):
        bt = _rope(proj(o3 + c * LANES, LANES), cos, sin_lo, sin_hi).T
        for e in range(heads_per_block):
            qit_ref[0, 0, c * heads_per_block + e] = bt[e * DSA_IDX_DIM:(e + 1) * DSA_IDX_DIM].astype(BF16)
    blk = proj(o4, LANES)
    is_ki = lane < DSA_IDX_DIM
    ms = jnp.sum(jnp.where(is_ki, blk * blk, 0.0), axis=-1, keepdims=True) * (1.0 / DSA_IDX_DIM)
    kin = blk * lax.rsqrt(ms + EPS) * kin_ref[...]
    kiw_ref[0] = _rope(kin, jnp.where(is_ki, cos, 1.0), jnp.where(is_ki, sin_lo, 0.0),
                       jnp.where(is_ki, sin_hi, 0.0)).astype(BF16)
    wt = (blk * (DSA_IDX_HEADS ** -0.5 * DSA_IDX_DIM ** -0.5)).T
    wit_ref[0, 0] = wt[DSA_IDX_DIM:DSA_IDX_DIM + DSA_IDX_HEADS]


def _dsa_prep(x, gain, w_in, positions, q_norm, k_norm, kidx_norm, tm):
    b, s, d = x.shape
    q_w = DSA_HEADS * DSA_HEAD_DIM
    kv_w = DSA_KV_HEADS * DSA_HEAD_DIM
    qi_w = DSA_IDX_HEADS * DSA_IDX_DIM
    n_in = w_in.shape[1]
    assert n_in == q_w + 2 * kv_w + qi_w + DSA_IDX_DIM + DSA_IDX_HEADS
    n_pad = q_w + 2 * kv_w + qi_w + LANES
    w = jnp.pad(w_in.astype(BF16), ((0, 0), (0, n_pad - n_in)))
    assert s % tm == 0 and DSA_IDX_DIM + DSA_IDX_HEADS <= LANES and DSA_IDX_DIM % 8 == 0
    nq = s // tm
    half = DSA_HEAD_DIM // 8
    inv_freq = ROPE_THETA ** (-jnp.arange(half, dtype=F32) / half)
    lane = jnp.arange(LANES)
    j = lane % DSA_HEAD_DIM
    invf = jnp.where(j < 2 * half, inv_freq[j % half], 0.0)[None, :]
    kin = jnp.concatenate([kidx_norm, jnp.ones((LANES - DSA_IDX_DIM,), F32)])[None, :]

    def tok(width, dtype):
        return pl.BlockSpec((1, tm, width), lambda i, t: (i, t, 0)), jax.ShapeDtypeStruct((b, s, width), dtype)

    def per_tile(shape, dtype):
        nd = len(shape)
        return (pl.BlockSpec((1, 1) + shape, lambda i, t: (i, t) + (0,) * nd),
                jax.ShapeDtypeStruct((b, nq) + shape, dtype))

    outs = [
        per_tile((DSA_HEADS, LANES, tm), BF16),
        tok(kv_w, BF16),
        (pl.BlockSpec((1, DSA_KV_HEADS, 1, DSA_HEAD_DIM, tm), lambda i, t: (i, 0, t, 0, 0)),
         jax.ShapeDtypeStruct((b, DSA_KV_HEADS, nq, DSA_HEAD_DIM, tm), BF16)),
        per_tile((DSA_IDX_HEADS, DSA_IDX_DIM, tm), BF16),
        tok(LANES, BF16),
        per_tile((DSA_IDX_HEADS, tm), F32),
    ]
    return pl.pallas_call(
        _dsa_prep_body,
        grid=(b, s // tm),
        in_specs=[
            pl.BlockSpec((1, tm, d), lambda i, t: (i, t, 0)),
            _const_spec((1, d)),
            _const_spec((d, n_pad)),
            pl.BlockSpec((1, tm, 1), lambda i, t: (i, t, 0)),
            _const_spec((1, LANES)),
            _const_spec((1, LANES)),
            _const_spec((1, LANES)),
            _const_spec((1, LANES)),
        ],
        out_specs=[o[0] for o in outs],
        out_shape=[o[1] for o in outs],
        compiler_params=_params("arbitrary", "arbitrary"),
        name="dsa_prep",
    )(x, gain[None, :], w, positions[:, :, None], invf,
      jnp.tile(q_norm, 2)[None, :], jnp.tile(k_norm, 2)[None, :], kin)


def _dsa_attn_body(bound_ref, qt_ref, k_ref, vt_ref, qit_ref, kiw_ref, wit_ref, x_ref, wo_ref, o_ref,
                   key_s, top_s, mask_s, m_s, l_s, acc_s, s_s, o_s, *, topk):
    qb = qt_ref.shape[4]
    tk = qb
    i = pl.program_id(1)
    n_tiles = i + 1
    rep = DSA_HEADS // DSA_KV_HEADS
    krow = lax.broadcasted_iota(I32, (tk, qb), 0)
    qcol = lax.broadcasted_iota(I32, (tk, qb), 1)
    diag_ok = (krow // CHUNK) <= (qcol // CHUNK)

    def score_tile(t, carry):
        kt = kiw_ref[0, pl.ds(pl.multiple_of(t * tk, tk), tk), 0:DSA_IDX_DIM]
        score = jnp.zeros((tk, qb), F32)
        for h in range(DSA_IDX_HEADS):
            score = score + wit_ref[0, 0, h:h + 1, :] * jnp.maximum(_dot(kt, qit_ref[0, 0, h]), 0.0)
        score = jnp.where(jnp.abs(score) < F32_MIN_NORMAL, 0.0, score)
        bits = pltpu.bitcast(score, I32)
        key = bits ^ ((bits >> 31) & jnp.int32(0x7FFFFFFF))
        admissible = (t < i) | diag_ok
        key_s[t] = jnp.where(admissible, key, jnp.int32(INT_MIN))
        top = pltpu.bitcast(bits & jnp.int32(-(1 << 16)), F32)
        top_s[t] = jnp.where(admissible, top, -jnp.inf).astype(BF16)
        return carry

    lax.fori_loop(0, n_tiles, score_tile, 0)

    def count(pred_fn):
        def body(t, acc):
            hit = jnp.where(pred_fn(key_s[t]), 1, 0).astype(I32)
            return acc + jnp.sum(hit.reshape(tk // 8, 8, qb), axis=0)
        acc = lax.fori_loop(0, n_tiles, body, jnp.zeros((8, qb), I32))
        return jnp.sum(acc, axis=0, keepdims=True)

    pack = 16

    def count_top(cand_b):
        def body(t, acc):
            hit = jnp.where(top_s[t] >= cand_b, jnp.ones((), BF16), jnp.zeros((), BF16))
            parts = [hit[r * pack:(r + 1) * pack] for r in range(tk // pack)]
            while len(parts) > 1:
                parts = [a + b for a, b in zip(parts[0::2], parts[1::2])]
            return acc + parts[0]
        acc = lax.fori_loop(0, n_tiles, body, jnp.zeros((pack, qb), BF16))
        return jnp.sum(acc.astype(F32), axis=0, keepdims=True)

    key16_neg_inf, key16_pos_inf = -32641, 32640
    key16_min_normal = 128

    def radix_top_step(b, thr16):
        cand = thr16 + lax.shift_left(jnp.int32(1), 15 - b)
        c = jnp.clip(cand, key16_neg_inf, key16_pos_inf)
        c = jnp.where((c >= 1) & (c < key16_min_normal), key16_min_normal, c)
        c = jnp.where((c <= -2) & (c >= -key16_min_normal), -1, c)
        pattern = c ^ ((c >> 15) & jnp.int32(0x7FFF))
        cand_b = pltpu.bitcast(lax.shift_left(pattern, 16), F32).astype(BF16)
        return jnp.where(count_top(cand_b) >= topk, cand, thr16)

    thr16 = lax.fori_loop(0, 16, radix_top_step, jnp.full((1, qb), -(1 << 15), I32))

    def radix_step(b, thr):
        cand = thr + lax.shift_left(jnp.int32(1), 31 - b)
        cnt = count(lambda key: key >= cand)
        return jnp.where(cnt >= topk, cand, thr)

    thr = lax.fori_loop(16, 32, radix_step, lax.shift_left(thr16, 16))
    n_ge = count(lambda key: key >= thr)
    has_cut_tie = jnp.max(n_ge) > topk

    @pl.when(jnp.logical_not(has_cut_tie))
    def _():
        def mask_tile(t, carry):
            key = key_s[t]
            mask_s[t] = jnp.where((key >= thr) & (key != jnp.int32(INT_MIN)), 0.0, NEG_BIG)
            return carry
        lax.fori_loop(0, n_tiles, mask_tile, 0)

    @pl.when(has_cut_tie)
    def _():
        n_gt = count(lambda key: key > thr)
        need = (topk - n_gt).astype(F32)
        strict_lower = (qcol < krow).astype(BF16)
        ones = jnp.ones((tk, tk), BF16)

        def mask_tile(t, before):
            key = key_s[t]
            eq = key == thr
            eqb = jnp.where(eq, 1.0, 0.0).astype(BF16)
            rank = before + _dot(strict_lower, eqb)
            sel = (key > thr) | (eq & (rank < need))
            sel = sel & (key != jnp.int32(INT_MIN))
            mask_s[t] = jnp.where(sel, 0.0, NEG_BIG)
            return before + _dot(ones, eqb)

        lax.fori_loop(0, n_tiles, mask_tile, jnp.zeros((tk, qb), F32))

    l_s[...] = jnp.zeros(l_s.shape, F32)
    acc_s[...] = jnp.zeros(acc_s.shape, F32)
    heads_per_block = LANES // DSA_HEAD_DIM

    def logits(t, h, mask):
        kblk = (h // rep) // heads_per_block
        kt = k_ref[0, pl.ds(pl.multiple_of(t * tk, tk), tk), kblk * LANES:(kblk + 1) * LANES]
        return _dot(kt, qt_ref[0, 0, h]) + mask

    def exp_sum_pv(use_offset):
        def attn_tile(t, carry):
            mask = mask_s[t]
            n_slots = s_s.shape[0]
            ahead = n_slots - 1
            for h in range(ahead):
                s_s[h] = logits(t, h, mask)
            for h in range(DSA_HEADS):
                nxt = h + ahead
                if nxt < DSA_HEADS:
                    s_s[nxt % n_slots] = logits(t, nxt, mask)
                s = s_s[h % n_slots]
                p = jnp.exp(s - m_s[h, 0:1, :] if use_offset else s)
                l_s[h] = l_s[h] + jnp.sum(p.reshape(tk // 8, 8, qb), axis=0)
                acc_s[h] = acc_s[h] + _dot(vt_ref[0, h // rep, t], p.astype(BF16))
            return carry

        lax.fori_loop(0, n_tiles, attn_tile, 0)

    bounded = bound_ref[0] <= DSA_SAFE_LOGIT

    @pl.when(bounded)
    def _():
        exp_sum_pv(use_offset=False)

    @pl.when(jnp.logical_not(bounded))
    def _():
        m_s[...] = jnp.full(m_s.shape, NEG_BIG, F32)

        def max_tile(t, carry):
            mask = mask_s[t]
            for h in range(DSA_HEADS):
                s = logits(t, h, mask)
                m_s[h] = jnp.maximum(m_s[h], jnp.max(s.reshape(tk // 8, 8, qb), axis=0))
            return carry

        lax.fori_loop(0, n_tiles, max_tile, 0)
        for h in range(DSA_HEADS):
            m_s[h] = jnp.broadcast_to(jnp.max(m_s[h], axis=0, keepdims=True), (8, qb))
        exp_sum_pv(use_offset=True)

    for c in range(DSA_HEADS // heads_per_block):
        pair = []
        for h in range(c * heads_per_block, (c + 1) * heads_per_block):
            pair.append(acc_s[h] / jnp.sum(l_s[h], axis=0, keepdims=True))
        o_s[:, c * LANES:(c + 1) * LANES] = jnp.concatenate(pair, axis=0).T.astype(BF16)
    o_ref[0] = x_ref[0] + _dot(o_s[...], wo_ref[...])


def _dsa_attn(bound, qt, k, vt, qit, kiw, wit, x, w_out):
    b, nq, n_heads, _, qb = qt.shape
    _, s, d = x.shape
    kvh, dh = vt.shape[1], vt.shape[3]
    topk = min(DSA_TOPK_MAX, s // 4)
    return pl.pallas_call(
        functools.partial(_dsa_attn_body, topk=topk),
        grid=(b, nq),
        in_specs=[
            pl.BlockSpec(memory_space=pltpu.SMEM),
            pl.BlockSpec((1, 1, n_heads, LANES, qb), lambda i, j: (i, j, 0, 0, 0)),
            pl.BlockSpec((1, s, kvh * dh), lambda i, j: (i, 0, 0)),
            pl.BlockSpec((1, kvh, nq, dh, qb), lambda i, j: (i, 0, 0, 0, 0)),
            pl.BlockSpec((1, 1, DSA_IDX_HEADS, DSA_IDX_DIM, qb), lambda i, j: (i, j, 0, 0, 0)),
            pl.BlockSpec((1, s, LANES), lambda i, j: (i, 0, 0)),
            pl.BlockSpec((1, 1, DSA_IDX_HEADS, qb), lambda i, j: (i, j, 0, 0)),
            pl.BlockSpec((1, qb, d), lambda i, j: (i, j, 0)),
            _const_spec((n_heads * dh, d)),
        ],
        out_specs=pl.BlockSpec((1, qb, d), lambda i, j: (i, j, 0)),
        out_shape=jax.ShapeDtypeStruct(x.shape, F32),
        scratch_shapes=[
            pltpu.VMEM((nq, qb, qb), I32),
            pltpu.VMEM((nq, qb, qb), BF16),
            pltpu.VMEM((nq, qb, qb), F32),
            pltpu.VMEM((n_heads, 8, qb), F32),
            pltpu.VMEM((n_heads, 8, qb), F32),
            pltpu.VMEM((n_heads, dh, qb), F32),
            pltpu.VMEM((4, qb, qb), F32),
            pltpu.VMEM((qb, n_heads * dh), BF16),
        ],
        compiler_params=_params("arbitrary", "arbitrary"),
        name="dsa_attn",
    )(bound, qt, k, vt, qit, kiw, wit, x, w_out.astype(BF16))


def _dsa_mixer(x, positions, gain, w_in, q_norm, k_norm, kidx_norm, w_out, qb=DSA_QTILE):
    s = x.shape[1]
    qb = _row_tile(s, qb)
    qt, k, vt, qit, kiw, wit = _dsa_prep(x, gain, w_in, positions, q_norm, k_norm, kidx_norm, qb)
    bound = (DSA_HEAD_DIM ** 0.5) * jnp.max(jnp.abs(q_norm)) * jnp.max(jnp.abs(k_norm)) * (1.0 + 2.0 ** -6)
    return _dsa_attn(bound.reshape(1), qt, k, vt, qit, kiw, wit, x, w_out)


def kernel(x, mem, positions, norm_mix, gm_w_in, gm_v_norm, gm_w_s, gm_b_s, gm_w_out, ssd_w_in, ssd_conv_w, ssd_conv_b, ssd_dt_bias, ssd_a_log, ssd_d, ssd_out_norm, ssd_w_out, dsa_w_in, dsa_q_norm, dsa_k_norm, dsa_kidx_norm, dsa_w_out, norm_xa, norm_mem, xa_w_q, xa_w_kv, xa_q_norm, xa_k_norm, xa_w_out, norm_ffn, ffn_w_up, ffn_w_down):
    b, s, d = x.shape
    depth = norm_mix.shape[0]
    mem_k, mem_v = _mem_kv(mem, norm_mem, xa_w_kv, xa_k_norm)
    for i in range(depth):
        kind, j = i % 3, i // 3
        if kind == 0:
            x = _gmlp(x.reshape(b * s, d), norm_mix[i], gm_w_in[j], gm_v_norm[j], gm_w_s[j], gm_b_s[j],
                      gm_w_out[j]).reshape(b, s, d)
        elif kind == 1:
            x = _ssd_mixer(x, norm_mix[i], ssd_w_in[j], ssd_conv_w[j], ssd_conv_b[j], ssd_dt_bias[j],
                           ssd_a_log[j], ssd_d[j], ssd_out_norm[j], ssd_w_out[j])
        else:
            x = _dsa_mixer(x, positions, norm_mix[i], dsa_w_in[j], dsa_q_norm[j], dsa_k_norm[j],
                           dsa_kidx_norm[j], dsa_w_out[j])
        x = _xattn_ffn(x, norm_xa[i], xa_w_q[i], xa_q_norm[i], mem_k[i], mem_v[i], xa_w_out[i],
                       norm_ffn[i], ffn_w_up[i], ffn_w_down[i])
    return x
```

```python
import functools

import jax
import jax.numpy as jnp
from jax import lax
from jax.experimental import pallas as pl
from jax.experimental.pallas import tpu as pltpu

F32 = jnp.float32
BF16 = jnp.bfloat16
I32 = jnp.int32

EPS = 1e-6
ROPE_THETA = 500000.0
CHUNK = 64
GM_BLOCK = 128
GM_GROUPS = 8
SSD_HEAD_DIM = 64
SSD_GROUPS = 8
SSD_STATE = 128
SSD_CONV = 4
SSD_CHUNK = 128
DSA_HEADS = 16
DSA_KV_HEADS = 4
DSA_HEAD_DIM = 64
DSA_IDX_HEADS = 8
DSA_IDX_DIM = 64
DSA_TOPK_MAX = 256
DSA_QTILE = 256
DSA_SAFE_LOGIT = 60.0
XA_HEADS = 4
XA_HEAD_DIM = 128

LANES = 128
VMEM_LIMIT_BYTES = 56 * 1024 * 1024
NEG_BIG = -1e30
INT_MIN = -(2 ** 31)
F32_MIN_NORMAL = 2.0 ** -126


def _params(*semantics):
    return pltpu.CompilerParams(dimension_semantics=semantics, vmem_limit_bytes=VMEM_LIMIT_BYTES)


def _const_spec(shape):
    nd = len(shape)
    return pl.BlockSpec(shape, lambda *_: (0,) * nd, pipeline_mode=pl.Buffered(1))


def _row_tile(n, want):
    t = min(n, want)
    assert n % t == 0, (n, t)
    return t


def _rms(x, gain):
    ms = jnp.mean(x * x, axis=-1, keepdims=True)
    return x * lax.rsqrt(ms + EPS) * gain


def _dot(a, b):
    return jnp.dot(a, b, preferred_element_type=F32)


def _dot_nt(a, b):
    return lax.dot_general(a, b, (((1,), (1,)), ((), ())), preferred_element_type=F32)


def _dot_tn(a, b):
    return lax.dot_general(a, b, (((0,), (0,)), ((), ())), preferred_element_type=F32)


def _split3(a):
    hi = a.astype(BF16)
    r1 = a - hi.astype(F32)
    mid = r1.astype(BF16)
    lo = (r1 - mid.astype(F32)).astype(BF16)
    return hi, mid, lo


def _dot_f32_lhs(a, b_exact):
    hi, mid, lo = _split3(a)
    return _dot(hi, b_exact) + _dot(mid, b_exact) + _dot(lo, b_exact)


def _dot_f32_rhs(a_exact, b):
    hi, mid, lo = _split3(b)
    return _dot(a_exact, hi) + _dot(a_exact, mid) + _dot(a_exact, lo)


def _mem_kv_body(mem_ref, g_ref, w_ref, kn_ref, k_ref, v_ref):
    mn = _rms(mem_ref[0], g_ref[0]).astype(BF16)
    kv = _dot(mn, w_ref[0])
    xa_w = XA_HEADS * XA_HEAD_DIM
    for h in range(XA_HEADS):
        cols = slice(h * XA_HEAD_DIM, (h + 1) * XA_HEAD_DIM)
        k_ref[0, 0, :, cols] = _rms(kv[:, cols], kn_ref[0]).astype(BF16)
    v_ref[0, 0] = kv[:, xa_w:].astype(BF16)


def _mem_kv(mem, norm_mem, w_kv, k_norm):
    depth, d, _ = w_kv.shape
    b, m, _ = mem.shape
    xa_w = XA_HEADS * XA_HEAD_DIM
    out = jax.ShapeDtypeStruct((depth, b, m, xa_w), BF16)
    return pl.pallas_call(
        _mem_kv_body,
        grid=(depth, b),
        in_specs=[
            pl.BlockSpec((1, m, d), lambda i, j: (j, 0, 0)),
            pl.BlockSpec((1, 1, d), lambda i, j: (i, 0, 0)),
            pl.BlockSpec((1, d, 2 * xa_w), lambda i, j: (i, 0, 0)),
            pl.BlockSpec((1, 1, XA_HEAD_DIM), lambda i, j: (i, 0, 0)),
        ],
        out_specs=[pl.BlockSpec((1, 1, m, xa_w), lambda i, j: (i, j, 0, 0))] * 2,
        out_shape=[out, out],
        compiler_params=_params("arbitrary", "arbitrary"),
        name="mem_kv",
    )(mem, norm_mem[:, None, :], w_kv.astype(BF16), k_norm[:, None, :])


def _xattn_ffn_body(x_ref, g_ref, wq_ref, qn_ref, k_ref, v_ref, wo_ref, gf_ref, wu_ref, wd_ref, o_ref,
                    *, hid_chunk):
    x = x_ref[0]
    xn = _rms(x, g_ref[...]).astype(BF16)
    q = _dot(xn, wq_ref[...])
    scale = XA_HEAD_DIM ** -0.5
    heads = []
    for h in range(XA_HEADS):
        cols = slice(h * XA_HEAD_DIM, (h + 1) * XA_HEAD_DIM)
        qh = _rms(q[:, cols], qn_ref[...]).astype(BF16)
        s = _dot_nt(qh, k_ref[0, :, cols]) * scale
        p = jnp.exp(s - jnp.max(s, axis=-1, keepdims=True))
        l = jnp.sum(p, axis=-1, keepdims=True)
        oh = _dot(p.astype(BF16), v_ref[0, :, cols]) / l
        heads.append(oh.astype(BF16))
    o = jnp.concatenate(heads, axis=-1)
    x = x + _dot(o, wo_ref[...])
    xn = _rms(x, gf_ref[...]).astype(BF16)
    acc = x
    for c in range(0, wu_ref.shape[1], hid_chunk):
        h = _dot(xn, wu_ref[:, c:c + hid_chunk])
        h = jnp.square(jnp.maximum(h, 0.0)).astype(BF16)
        acc = acc + _dot(h, wd_ref[c:c + hid_chunk, :])
    o_ref[0] = acc


def _xattn_ffn(x, gain_xa, w_q, q_norm, k, v, w_out, gain_ffn, w_up, w_down, tm=512, hid_chunk=1024):
    b, s, d = x.shape
    m = k.shape[1]
    xa_w = XA_HEADS * XA_HEAD_DIM
    hid = w_up.shape[1]
    tm = _row_tile(s, tm)
    return pl.pallas_call(
        functools.partial(_xattn_ffn_body, hid_chunk=hid_chunk),
        grid=(b, s // tm),
        in_specs=[
            pl.BlockSpec((1, tm, d), lambda i, j: (i, j, 0)),
            _const_spec((1, d)),
            _const_spec((d, xa_w)),
            _const_spec((1, XA_HEAD_DIM)),
            pl.BlockSpec((1, m, xa_w), lambda i, j: (i, 0, 0)),
            pl.BlockSpec((1, m, xa_w), lambda i, j: (i, 0, 0)),
            _const_spec((xa_w, d)),
            _const_spec((1, d)),
            _const_spec((d, hid)),
            _const_spec((hid, d)),
        ],
        out_specs=pl.BlockSpec((1, tm, d), lambda i, j: (i, j, 0)),
        out_shape=jax.ShapeDtypeStruct(x.shape, F32),
        compiler_params=_params("arbitrary", "arbitrary"),
        name="xattn_ffn",
    )(x, gain_xa[None, :], w_q.astype(BF16), q_norm[None, :], k, v, w_out.astype(BF16),
      gain_ffn[None, :], w_up.astype(BF16), w_down.astype(BF16))


def _proj_res_body(x_ref, y_ref, w_ref, o_ref):
    o_ref[...] = x_ref[...] + _dot(y_ref[...], w_ref[...])


def _proj_res(x2, y2, w, tm=512):
    t, d = x2.shape
    k = y2.shape[1]
    tm = _row_tile(t, tm)
    return pl.pallas_call(
        _proj_res_body,
        grid=(t // tm,),
        in_specs=[
            pl.BlockSpec((tm, d), lambda i: (i, 0)),
            pl.BlockSpec((tm, k), lambda i: (i, 0)),
            _const_spec((k, d)),
        ],
        out_specs=pl.BlockSpec((tm, d), lambda i: (i, 0)),
        out_shape=jax.ShapeDtypeStruct(x2.shape, F32),
        compiler_params=_params("arbitrary"),
        name="proj_res",
    )(x2, y2, w.astype(BF16))


def _gmlp_body(x_ref, g_ref, win_ref, vn_ref, ws_ref, bs_ref, wout_ref, o_ref,
               xn_s, u_s, v_s, h_s, *, col_chunk):
    tm = x_ref.shape[0]
    hid = u_s.shape[1]
    gw = hid // GM_GROUPS
    x = x_ref[...]
    xn_s[...] = _rms(x, g_ref[...]).astype(BF16)
    for c in range(0, hid, col_chunk):
        u_s[:, c:c + col_chunk] = jax.nn.gelu(_dot(xn_s[...], win_ref[:, c:c + col_chunk]))
    ssq = jnp.zeros((tm, 1), F32)
    for c in range(0, hid, col_chunk):
        vc = jax.nn.gelu(_dot(xn_s[...], win_ref[:, hid + c:hid + c + col_chunk]))
        v_s[:, c:c + col_chunk] = vc
        ssq = ssq + jnp.sum(vc * vc, axis=-1, keepdims=True)
    inv = lax.rsqrt(ssq / hid + EPS)
    t_i = lax.broadcasted_iota(I32, (GM_BLOCK, GM_BLOCK), 0)
    s_i = lax.broadcasted_iota(I32, (GM_BLOCK, GM_BLOCK), 1)
    causal = (s_i // CHUNK) <= (t_i // CHUNK)
    for g in range(GM_GROUPS):
        cols = slice(g * gw, (g + 1) * gw)
        wsg = jnp.where(causal, ws_ref[g], 0.0).astype(BF16)
        bias = bs_ref[:, g:g + 1]
        for n in range(tm // GM_BLOCK):
            rows = slice(n * GM_BLOCK, (n + 1) * GM_BLOCK)
            vg = (v_s[rows, cols] * inv[rows] * vn_ref[:, cols]).astype(BF16)
            sg = _dot(wsg, vg) + bias
            h_s[rows, cols] = (u_s[rows, cols] * sg).astype(BF16)
    o_ref[...] = x + _dot(h_s[...], wout_ref[...])


def _gmlp(x2, gain, w_in, v_norm, w_s, b_s, w_out, tm=512, col_chunk=512):
    t, d = x2.shape
    hid = w_out.shape[0]
    tm = _row_tile(t, tm)
    assert tm % GM_BLOCK == 0
    return pl.pallas_call(
        functools.partial(_gmlp_body, col_chunk=col_chunk),
        grid=(t // tm,),
        in_specs=[
            pl.BlockSpec((tm, d), lambda i: (i, 0)),
            _const_spec((1, d)),
            _const_spec((d, 2 * hid)),
            _const_spec((1, hid)),
            _const_spec((GM_GROUPS, GM_BLOCK, GM_BLOCK)),
            _const_spec((GM_BLOCK, GM_GROUPS)),
            _const_spec((hid, d)),
        ],
        out_specs=pl.BlockSpec((tm, d), lambda i: (i, 0)),
        out_shape=jax.ShapeDtypeStruct(x2.shape, F32),
        scratch_shapes=[
            pltpu.VMEM((tm, d), BF16),
            pltpu.VMEM((tm, hid), F32),
            pltpu.VMEM((tm, hid), F32),
            pltpu.VMEM((tm, hid), BF16),
        ],
        compiler_params=_params("arbitrary"),
        name="gmlp",
    )(x2, gain[None, :], w_in.astype(BF16), v_norm[None, :], w_s, b_s.T, w_out.astype(BF16))


def _ssd_in_body(x_ref, g_ref, wz_ref, wx_ref, wdt_ref, z_ref, xbc_ref, dt_ref, *, col_chunk):
    xn = _rms(x_ref[...], g_ref[...]).astype(BF16)
    for c in range(0, wz_ref.shape[1], col_chunk):
        z_ref[:, c:c + col_chunk] = _dot(xn, wz_ref[:, c:c + col_chunk]).astype(BF16)
    for c in range(0, wx_ref.shape[1], col_chunk):
        xbc_ref[:, c:c + col_chunk] = _dot(xn, wx_ref[:, c:c + col_chunk]).astype(BF16)
    dt_ref[...] = _dot(xn, wdt_ref[...])


def _ssd_in(x2, gain, w_z, w_xbc, w_dt, tm=512, col_chunk=512):
    t, d = x2.shape
    tm = _row_tile(t, tm)
    nz, nx, ndt = w_z.shape[1], w_xbc.shape[1], w_dt.shape[1]
    return pl.pallas_call(
        functools.partial(_ssd_in_body, col_chunk=col_chunk),
        grid=(t // tm,),
        in_specs=[
            pl.BlockSpec((tm, d), lambda i: (i, 0)),
            _const_spec((1, d)),
            _const_spec((d, nz)),
            _const_spec((d, nx)),
            _const_spec((d, ndt)),
        ],
        out_specs=[
            pl.BlockSpec((tm, nz), lambda i: (i, 0)),
            pl.BlockSpec((tm, nx), lambda i: (i, 0)),
            pl.BlockSpec((tm, ndt), lambda i: (i, 0)),
        ],
        out_shape=[
            jax.ShapeDtypeStruct((t, nz), BF16),
            jax.ShapeDtypeStruct((t, nx), BF16),
            jax.ShapeDtypeStruct((t, ndt), F32),
        ],
        compiler_params=_params("arbitrary"),
        name="ssd_in",
    )(x2, gain[None, :], w_z, w_xbc, w_dt)


def _ssd_core_body(z_ref, xbc_ref, dt_ref, cw_ref, cb_ref, dtb_ref, alog_ref, dskip_ref, onorm_ref,
                   tri_ref, expand_ref, shift_ref, y_ref, state_s, tail_s, *, conv_chunk):
    q = SSD_CHUNK
    n_state = SSD_STATE
    d_inner = z_ref.shape[2]
    conv_ch = xbc_ref.shape[2]
    heads_per_group = d_inner // SSD_HEAD_DIM // SSD_GROUPS
    gw = heads_per_group * SSD_HEAD_DIM
    tail = tail_s.shape[0]

    @pl.when(pl.program_id(1) == 0)
    def _():
        state_s[...] = jnp.zeros_like(state_s)
        tail_s[...] = jnp.zeros_like(tail_s)

    xb = xbc_ref[0]
    x_ext = jnp.concatenate([tail_s[...], xb], axis=0)
    tail_s[...] = xb[q - tail:, :]
    pieces = []
    for c0 in range(0, conv_ch, conv_chunk):
        cols = slice(c0, c0 + conv_chunk)
        conv = cb_ref[:, cols] + cw_ref[SSD_CONV - 1:SSD_CONV, cols] * xb[:, cols].astype(F32)
        for k in range(SSD_CONV - 1):
            conv = conv + cw_ref[k:k + 1, cols] * _dot(shift_ref[k], x_ext[:, cols])
        pieces.append(conv * jax.nn.sigmoid(conv))
    xbc = jnp.concatenate(pieces, axis=1)
    xs = xbc[:, :d_inner]
    bm = xbc[:, d_inner:d_inner + SSD_GROUPS * n_state].astype(BF16)
    cm = xbc[:, d_inner + SSD_GROUPS * n_state:].astype(BF16)

    dt_raw = dt_ref[0] + dtb_ref[...]
    dt = jnp.maximum(dt_raw, 0.0) + jnp.log1p(jnp.exp(-jnp.abs(dt_raw)))
    a = dt * (-jnp.exp(alog_ref[...]))
    a_cum = _dot_f32_rhs(tri_ref[...], a)
    a_cum_t = a_cum.T
    expand = expand_ref[...]
    dt_e = _dot_f32_lhs(dt, expand)
    acum_e = _dot_f32_lhs(a_cum, expand)
    alast_e = acum_e[q - 1:q, :]
    xdt = xs * dt_e
    xw = (xdt * jnp.exp(alast_e - acum_e)).astype(BF16)
    xdt_b = xdt.astype(BF16)
    ea_e = jnp.exp(acum_e)
    chunk_decay_e = jnp.exp(alast_e)

    l_i = lax.broadcasted_iota(I32, (q, q), 0)
    s_i = lax.broadcasted_iota(I32, (q, q), 1)
    causal = l_i >= s_i
    lane_g = lax.broadcasted_iota(I32, (q, gw), 1) // SSD_HEAD_DIM

    zf = z_ref[0].astype(F32)
    for g in range(SSD_GROUPS):
        gcols = slice(g * gw, (g + 1) * gw)
        ncols = slice(g * n_state, (g + 1) * n_state)
        cg = cm[:, ncols]
        bg = bm[:, ncols]
        cb = _dot_nt(cg, bg)
        st = state_s[g]
        y_off = _dot(cg, st.astype(BF16)) * ea_e[:, gcols]
        ms = []
        xblk = []
        xg = xdt_b[:, gcols]
        for r in range(heads_per_group):
            h = g * heads_per_group + r
            seg = a_cum[:, h:h + 1] - a_cum_t[h:h + 1, :]
            decay = jnp.exp(jnp.where(causal, seg, -jnp.inf))
            ms.append((decay * cb).astype(BF16))
            xblk.append(jnp.where(lane_g == r, xg, jnp.zeros_like(xg)))
        y_diag = _dot(jnp.concatenate(ms, axis=1), jnp.concatenate(xblk, axis=0))
        y = y_diag + y_off + xs[:, gcols] * dskip_ref[:, gcols]
        state_s[g] = st * chunk_decay_e[:, gcols] + _dot_tn(bg, xw[:, gcols])
        zg = zf[:, gcols]
        gated = y * (zg * jax.nn.sigmoid(zg))
        gated = gated * lax.rsqrt(jnp.mean(gated * gated, axis=-1, keepdims=True) + EPS)
        y_ref[0, :, gcols] = (gated * onorm_ref[:, gcols]).astype(BF16)


def _ssd_core(z, xbc, dt, conv_w, conv_b, dt_bias, a_log, d_skip, out_norm, conv_chunk=1024):
    b, s, d_inner = z.shape
    conv_ch = xbc.shape[2]
    n_heads = d_inner // SSD_HEAD_DIM
    gw = d_inner // SSD_GROUPS
    q = SSD_CHUNK
    tail = 16
    assert s % q == 0 and n_heads <= LANES and dt.shape[2] == LANES and SSD_CONV - 1 <= tail

    def pad_heads(v):
        return jnp.pad(v, (0, LANES - n_heads))[None, :]

    tri = (jnp.arange(q)[:, None] >= jnp.arange(q)[None, :]).astype(BF16)
    expand = (jnp.arange(LANES)[:, None] == (jnp.arange(d_inner) // SSD_HEAD_DIM)[None, :]).astype(BF16)
    src = tail + jnp.arange(q)[None, :, None] - (SSD_CONV - 1 - jnp.arange(SSD_CONV - 1))[:, None, None]
    shift = (jnp.arange(tail + q)[None, None, :] == src).astype(BF16)
    return pl.pallas_call(
        functools.partial(_ssd_core_body, conv_chunk=conv_chunk),
        grid=(b, s // q),
        in_specs=[
            pl.BlockSpec((1, q, d_inner), lambda i, j: (i, j, 0)),
            pl.BlockSpec((1, q, conv_ch), lambda i, j: (i, j, 0)),
            pl.BlockSpec((1, q, LANES), lambda i, j: (i, j, 0)),
            _const_spec((SSD_CONV, conv_ch)),
            _const_spec((1, conv_ch)),
            _const_spec((1, LANES)),
            _const_spec((1, LANES)),
            _const_spec((1, d_inner)),
            _const_spec((1, d_inner)),
            _const_spec((q, q)),
            _const_spec((LANES, d_inner)),
            _const_spec((SSD_CONV - 1, q, tail + q)),
        ],
        out_specs=pl.BlockSpec((1, q, d_inner), lambda i, j: (i, j, 0)),
        out_shape=jax.ShapeDtypeStruct((b, s, d_inner), BF16),
        scratch_shapes=[
            pltpu.VMEM((SSD_GROUPS, SSD_STATE, gw), F32),
            pltpu.VMEM((tail, conv_ch), BF16),
        ],
        compiler_params=_params("arbitrary", "arbitrary"),
        name="ssd_core",
    )(z, xbc, dt, conv_w, conv_b[None, :], pad_heads(dt_bias), pad_heads(a_log),
      jnp.repeat(d_skip, SSD_HEAD_DIM)[None, :], out_norm[None, :], tri, expand, shift)


def _ssd_mixer(x, gain, w_in, conv_w, conv_b, dt_bias, a_log, d_skip, out_norm, w_out):
    b, s, d = x.shape
    d_inner = w_out.shape[0]
    conv_ch = conv_w.shape[1]
    n_heads = d_inner // SSD_HEAD_DIM
    w_in = w_in.astype(BF16)
    w_z = w_in[:, :d_inner]
    w_xbc = w_in[:, d_inner:d_inner + conv_ch]
    w_dt = jnp.pad(w_in[:, d_inner + conv_ch:], ((0, 0), (0, LANES - n_heads)))
    x2 = x.reshape(b * s, d)
    z, xbc, dt = _ssd_in(x2, gain, w_z, w_xbc, w_dt)
    y = _ssd_core(z.reshape(b, s, d_inner), xbc.reshape(b, s, conv_ch), dt.reshape(b, s, LANES),
                  conv_w, conv_b, dt_bias, a_log, d_skip, out_norm)
    return _proj_res(x2, y.reshape(b * s, d_inner), w_out).reshape(b, s, d)


def _halves_rms(x, gain, lane):
    lo = lane < (LANES // 2)
    x2 = x * x
    s_lo = jnp.sum(jnp.where(lo, x2, 0.0), axis=-1, keepdims=True)
    s_hi = jnp.sum(jnp.where(lo, 0.0, x2), axis=-1, keepdims=True)
    ms = jnp.where(lo, s_lo, s_hi) * (2.0 / LANES)
    return x * lax.rsqrt(ms + EPS) * gain


def _rope(x, cos, sin_lo, sin_hi):
    half = DSA_HEAD_DIM // 8
    return x * cos + pltpu.roll(x, LANES - half, 1) * sin_lo + pltpu.roll(x, half, 1) * sin_hi


def _dsa_prep_body(x_ref, g_ref, w_ref, pos_ref, invf_ref, qn_ref, kn_ref, kin_ref,
                   qt_ref, k_ref, vt_ref, qit_ref, kiw_ref, wit_ref):
    tm = x_ref.shape[1]
    rep = DSA_HEADS // DSA_KV_HEADS
    heads_per_block = LANES // DSA_HEAD_DIM
    q_w = DSA_HEADS * DSA_HEAD_DIM
    kv_w = DSA_KV_HEADS * DSA_HEAD_DIM
    qi_w = DSA_IDX_HEADS * DSA_IDX_DIM
    half = DSA_HEAD_DIM // 8
    xn = _rms(x_ref[0], g_ref[...]).astype(BF16)
    pos = pos_ref[0].astype(F32)
    lane = lax.broadcasted_iota(I32, (tm, LANES), 1)
    j = lane % DSA_HEAD_DIM

    ang = pos * invf_ref[...]
    cos, sin = jnp.cos(ang), jnp.sin(ang)
    sin_lo, sin_hi = jnp.where(j < half, -sin, 0.0), jnp.where(j >= half, sin, 0.0)

    def proj(c0, width):
        return _dot(xn, w_ref[:, c0:c0 + width])

    o1, o2, o3, o4 = q_w, q_w + kv_w, q_w + 2 * kv_w, q_w + 2 * kv_w + qi_w
    zeros = jnp.zeros((DSA_HEAD_DIM, tm), F32)
    for c in range(q_w // LANES):
        blk = _halves_rms(proj(c * LANES, LANES), qn_ref[...], lane)
        bt = (_rope(blk, cos, sin_lo, sin_hi) * (DSA_HEAD_DIM ** -0.5)).T
        for e in range(heads_per_block):
            h = c * heads_per_block + e
            rows = bt[e * DSA_HEAD_DIM:(e + 1) * DSA_HEAD_DIM]
            first_half = (h // rep) % heads_per_block == 0
            padded = jnp.concatenate([rows, zeros] if first_half else [zeros, rows], axis=0)
            qt_ref[0, 0, h] = padded.astype(BF16)
    for c in range(0, kv_w, LANES):
        blk = _halves_rms(proj(o1 + c, LANES), kn_ref[...], lane)
        k_ref[0, :, c:c + LANES] = _rope(blk, cos, sin_lo, sin_hi).astype(BF16)
    vt = proj(o2, kv_w).T
    for g in range(DSA_KV_HEADS):
        vt_ref[0, g, 0] = vt[g * DSA_HEAD_DIM:(g + 1) * DSA_HEAD_DIM].astype(BF16)
    for c in range(qi_w // LANES):
        bt = _rope(proj(o3 + c * LANES, LANES), cos, sin_lo, sin_hi).T
        for e in range(heads_per_block):
            qit_ref[0, 0, c * heads_per_block + e] = bt[e * DSA_IDX_DIM:(e + 1) * DSA_IDX_DIM].astype(BF16)
    blk = proj(o4, LANES)
    is_ki = lane < DSA_IDX_DIM
    ms = jnp.sum(jnp.where(is_ki, blk * blk, 0.0), axis=-1, keepdims=True) * (1.0 / DSA_IDX_DIM)
    kin = blk * lax.rsqrt(ms + EPS) * kin_ref[...]
    kiw_ref[0] = _rope(kin, jnp.where(is_ki, cos, 1.0), jnp.where(is_ki, sin_lo, 0.0),
                       jnp.where(is_ki, sin_hi, 0.0)).astype(BF16)
    wt = (blk * (DSA_IDX_HEADS ** -0.5 * DSA_IDX_DIM ** -0.5)).T
    wit_ref[0, 0] = wt[DSA_IDX_DIM:DSA_IDX_DIM + DSA_IDX_HEADS]


def _dsa_prep(x, gain, w_in, positions, q_norm, k_norm, kidx_norm, tm):
    b, s, d = x.shape
    q_w = DSA_HEADS * DSA_HEAD_DIM
    kv_w = DSA_KV_HEADS * DSA_HEAD_DIM
    qi_w = DSA_IDX_HEADS * DSA_IDX_DIM
    n_in = w_in.shape[1]
    assert n_in == q_w + 2 * kv_w + qi_w + DSA_IDX_DIM + DSA_IDX_HEADS
    n_pad = q_w + 2 * kv_w + qi_w + LANES
    w = jnp.pad(w_in.astype(BF16), ((0, 0), (0, n_pad - n_in)))
    assert s % tm == 0 and DSA_IDX_DIM + DSA_IDX_HEADS <= LANES and DSA_IDX_DIM % 8 == 0
    nq = s // tm
    half = DSA_HEAD_DIM // 8
    inv_freq = ROPE_THETA ** (-jnp.arange(half, dtype=F32) / half)
    lane = jnp.arange(LANES)
    j = lane % DSA_HEAD_DIM
    invf = jnp.where(j < 2 * half, inv_freq[j % half], 0.0)[None, :]
    kin = jnp.concatenate([kidx_norm, jnp.ones((LANES - DSA_IDX_DIM,), F32)])[None, :]

    def tok(width, dtype):
        return pl.BlockSpec((1, tm, width), lambda i, t: (i, t, 0)), jax.ShapeDtypeStruct((b, s, width), dtype)

    def per_tile(shape, dtype):
        nd = len(shape)
        return (pl.BlockSpec((1, 1) + shape, lambda i, t: (i, t) + (0,) * nd),
                jax.ShapeDtypeStruct((b, nq) + shape, dtype))

    outs = [
        per_tile((DSA_HEADS, LANES, tm), BF16),
        tok(kv_w, BF16),
        (pl.BlockSpec((1, DSA_KV_HEADS, 1, DSA_HEAD_DIM, tm), lambda i, t: (i, 0, t, 0, 0)),
         jax.ShapeDtypeStruct((b, DSA_KV_HEADS, nq, DSA_HEAD_DIM, tm), BF16)),
        per_tile((DSA_IDX_HEADS, DSA_IDX_DIM, tm), BF16),
        tok(LANES, BF16),
        per_tile((DSA_IDX_HEADS, tm), F32),
    ]
    return pl.pallas_call(
        _dsa_prep_body,
        grid=(b, s // tm),
        in_specs=[
            pl.BlockSpec((1, tm, d), lambda i, t: (i, t, 0)),
            _const_spec((1, d)),
            _const_spec((d, n_pad)),
            pl.BlockSpec((1, tm, 1), lambda i, t: (i, t, 0)),
            _const_spec((1, LANES)),
            _const_spec((1, LANES)),
            _const_spec((1, LANES)),
            _const_spec((1, LANES)),
        ],
        out_specs=[o[0] for o in outs],
        out_shape=[o[1] for o in outs],
        compiler_params=_params("arbitrary", "arbitrary"),
        name="dsa_prep",
    )(x, gain[None, :], w, positions[:, :, None], invf,
      jnp.tile(q_norm, 2)[None, :], jnp.tile(k_norm, 2)[None, :], kin)


def _dsa_attn_body(bound_ref, qt_ref, k_ref, vt_ref, qit_ref, kiw_ref, wit_ref, x_ref, wo_ref, o_ref,
                   key_s, top_s, mask_s, m_s, l_s, acc_s, s_s, o_s, *, topk):
    qb = qt_ref.shape[4]
    tk = qb
    i = pl.program_id(1)
    n_tiles = i + 1
    rep = DSA_HEADS // DSA_KV_HEADS
    krow = lax.broadcasted_iota(I32, (tk, qb), 0)
    qcol = lax.broadcasted_iota(I32, (tk, qb), 1)
    diag_ok = (krow // CHUNK) <= (qcol // CHUNK)

    def score_tile(t, carry):
        kt = kiw_ref[0, pl.ds(pl.multiple_of(t * tk, tk), tk), 0:DSA_IDX_DIM]
        score = jnp.zeros((tk, qb), F32)
        for h in range(DSA_IDX_HEADS):
            score = score + wit_ref[0, 0, h:h + 1, :] * jnp.maximum(_dot(kt, qit_ref[0, 0, h]), 0.0)
        score = jnp.where(jnp.abs(score) < F32_MIN_NORMAL, 0.0, score)
        bits = pltpu.bitcast(score, I32)
        key = bits ^ ((bits >> 31) & jnp.int32(0x7FFFFFFF))
        admissible = (t < i) | diag_ok
        key_s[t] = jnp.where(admissible, key, jnp.int32(INT_MIN))
        top = pltpu.bitcast(bits & jnp.int32(-(1 << 16)), F32)
        top_s[t] = jnp.where(admissible, top, -jnp.inf).astype(BF16)
        return carry

    lax.fori_loop(0, n_tiles, score_tile, 0)

    def count(pred_fn):
        def body(t, acc):
            hit = jnp.where(pred_fn(key_s[t]), 1, 0).astype(I32)
            return acc + jnp.sum(hit.reshape(tk // 8, 8, qb), axis=0)
        acc = lax.fori_loop(0, n_tiles, body, jnp.zeros((8, qb), I32))
        return jnp.sum(acc, axis=0, keepdims=True)

    pack = 16

    def count_top(cand_b):
        def body(t, acc):
            hit = jnp.where(top_s[t] >= cand_b, jnp.ones((), BF16), jnp.zeros((), BF16))
            parts = [hit[r * pack:(r + 1) * pack] for r in range(tk // pack)]
            while len(parts) > 1:
                parts = [a + b for a, b in zip(parts[0::2], parts[1::2])]
            return acc + parts[0]
        acc = lax.fori_loop(0, n_tiles, body, jnp.zeros((pack, qb), BF16))
        return jnp.sum(acc.astype(F32), axis=0, keepdims=True)

    key16_neg_inf, key16_pos_inf = -32641, 32640
    key16_min_normal = 128

    def radix_top_step(b, thr16):
        cand = thr16 + lax.shift_left(jnp.int32(1), 15 - b)
        c = jnp.clip(cand, key16_neg_inf, key16_pos_inf)
        c = jnp.where((c >= 1) & (c < key16_min_normal), key16_min_normal, c)
        c = jnp.where((c <= -2) & (c >= -key16_min_normal), -1, c)
        pattern = c ^ ((c >> 15) & jnp.int32(0x7FFF))
        cand_b = pltpu.bitcast(lax.shift_left(pattern, 16), F32).astype(BF16)
        return jnp.where(count_top(cand_b) >= topk, cand, thr16)

    thr16 = lax.fori_loop(0, 16, radix_top_step, jnp.full((1, qb), -(1 << 15), I32))

    def radix_step(b, thr):
        cand = thr + lax.shift_left(jnp.int32(1), 31 - b)
        cnt = count(lambda key: key >= cand)
        return jnp.where(cnt >= topk, cand, thr)

    thr = lax.fori_loop(16, 32, radix_step, lax.shift_left(thr16, 16))
    n_ge = count(lambda key: key >= thr)
    has_cut_tie = jnp.max(n_ge) > topk

    @pl.when(jnp.logical_not(has_cut_tie))
    def _():
        def mask_tile(t, carry):
            key = key_s[t]
            mask_s[t] = jnp.where((key >= thr) & (key != jnp.int32(INT_MIN)), 0.0, NEG_BIG)
            return carry
        lax.fori_loop(0, n_tiles, mask_tile, 0)

    @pl.when(has_cut_tie)
    def _():
        n_gt = count(lambda key: key > thr)
        need = (topk - n_gt).astype(F32)
        strict_lower = (qcol < krow).astype(BF16)
        ones = jnp.ones((tk, tk), BF16)

        def mask_tile(t, before):
            key = key_s[t]
            eq = key == thr
            eqb = jnp.where(eq, 1.0, 0.0).astype(BF16)
            rank = before + _dot(strict_lower, eqb)
            sel = (key > thr) | (eq & (rank < need))
            sel = sel & (key != jnp.int32(INT_MIN))
            mask_s[t] = jnp.where(sel, 0.0, NEG_BIG)
            return before + _dot(ones, eqb)

        lax.fori_loop(0, n_tiles, mask_tile, jnp.zeros((tk, qb), F32))

    l_s[...] = jnp.zeros(l_s.shape, F32)
    acc_s[...] = jnp.zeros(acc_s.shape, F32)
    heads_per_block = LANES // DSA_HEAD_DIM

    def logits(t, h, mask):
        kblk = (h // rep) // heads_per_block
        start = t * tk if isinstance(t, int) else pl.multiple_of(t * tk, tk)
        kt = k_ref[0, pl.ds(start, tk), kblk * LANES:(kblk + 1) * LANES]
        return _dot(kt, qt_ref[0, 0, h]) + mask

    def exp_sum_pv(use_offset):
        n_slots = s_s.shape[0]
        ahead = n_slots - 1
        assert DSA_HEADS % n_slots == 0
        for h in range(ahead):
            s_s[h] = logits(0, h, mask_s[0])

        def attn_tile(t, carry):
            t_next = jnp.minimum(t + 1, n_tiles - 1)
            mask, mask_next = mask_s[t], mask_s[t_next]
            for h in range(DSA_HEADS):
                nxt = h + ahead
                if nxt < DSA_HEADS:
                    s_s[nxt % n_slots] = logits(t, nxt, mask)
                else:
                    s_s[nxt % n_slots] = logits(t_next, nxt - DSA_HEADS, mask_next)
                s = s_s[h % n_slots]
                p = jnp.exp(s - m_s[h, 0:1, :] if use_offset else s)
                l_s[h] = l_s[h] + jnp.sum(p.reshape(tk // 8, 8, qb), axis=0)
                acc_s[h] = acc_s[h] + _dot(vt_ref[0, h // rep, t], p.astype(BF16))
            return carry

        lax.fori_loop(0, n_tiles, attn_tile, 0)

    bounded = bound_ref[0] <= DSA_SAFE_LOGIT

    @pl.when(bounded)
    def _():
        exp_sum_pv(use_offset=False)

    @pl.when(jnp.logical_not(bounded))
    def _():
        m_s[...] = jnp.full(m_s.shape, NEG_BIG, F32)

        def max_tile(t, carry):
            mask = mask_s[t]
            for h in range(DSA_HEADS):
                s = logits(t, h, mask)
                m_s[h] = jnp.maximum(m_s[h], jnp.max(s.reshape(tk // 8, 8, qb), axis=0))
            return carry

        lax.fori_loop(0, n_tiles, max_tile, 0)
        for h in range(DSA_HEADS):
            m_s[h] = jnp.broadcast_to(jnp.max(m_s[h], axis=0, keepdims=True), (8, qb))
        exp_sum_pv(use_offset=True)

    for c in range(DSA_HEADS // heads_per_block):
        pair = []
        for h in range(c * heads_per_block, (c + 1) * heads_per_block):
            pair.append(acc_s[h] / jnp.sum(l_s[h], axis=0, keepdims=True))
        o_s[:, c * LANES:(c + 1) * LANES] = jnp.concatenate(pair, axis=0).T.astype(BF16)
    o_ref[0] = x_ref[0] + _dot(o_s[...], wo_ref[...])


def _dsa_attn(bound, qt, k, vt, qit, kiw, wit, x, w_out):
    b, nq, n_heads, _, qb = qt.shape
    _, s, d = x.shape
    kvh, dh = vt.shape[1], vt.shape[3]
    topk = min(DSA_TOPK_MAX, s // 4)
    return pl.pallas_call(
        functools.partial(_dsa_attn_body, topk=topk),
        grid=(b, nq),
        in_specs=[
            pl.BlockSpec(memory_space=pltpu.SMEM),
            pl.BlockSpec((1, 1, n_heads, LANES, qb), lambda i, j: (i, j, 0, 0, 0)),
            pl.BlockSpec((1, s, kvh * dh), lambda i, j: (i, 0, 0)),
            pl.BlockSpec((1, kvh, nq, dh, qb), lambda i, j: (i, 0, 0, 0, 0)),
            pl.BlockSpec((1, 1, DSA_IDX_HEADS, DSA_IDX_DIM, qb), lambda i, j: (i, j, 0, 0, 0)),
            pl.BlockSpec((1, s, LANES), lambda i, j: (i, 0, 0)),
            pl.BlockSpec((1, 1, DSA_IDX_HEADS, qb), lambda i, j: (i, j, 0, 0)),
            pl.BlockSpec((1, qb, d), lambda i, j: (i, j, 0)),
            _const_spec((n_heads * dh, d)),
        ],
        out_specs=pl.BlockSpec((1, qb, d), lambda i, j: (i, j, 0)),
        out_shape=jax.ShapeDtypeStruct(x.shape, F32),
        scratch_shapes=[
            pltpu.VMEM((nq, qb, qb), I32),
            pltpu.VMEM((nq, qb, qb), BF16),
            pltpu.VMEM((nq, qb, qb), F32),
            pltpu.VMEM((n_heads, 8, qb), F32),
            pltpu.VMEM((n_heads, 8, qb), F32),
            pltpu.VMEM((n_heads, dh, qb), F32),
            pltpu.VMEM((4, qb, qb), F32),
            pltpu.VMEM((qb, n_heads * dh), BF16),
        ],
        compiler_params=_params("arbitrary", "arbitrary"),
        name="dsa_attn",
    )(bound, qt, k, vt, qit, kiw, wit, x, w_out.astype(BF16))


def _dsa_mixer(x, positions, gain, w_in, q_norm, k_norm, kidx_norm, w_out, qb=DSA_QTILE):
    s = x.shape[1]
    qb = _row_tile(s, qb)
    qt, k, vt, qit, kiw, wit = _dsa_prep(x, gain, w_in, positions, q_norm, k_norm, kidx_norm, qb)
    bound = (DSA_HEAD_DIM ** 0.5) * jnp.max(jnp.abs(q_norm)) * jnp.max(jnp.abs(k_norm)) * (1.0 + 2.0 ** -6)
    return _dsa_attn(bound.reshape(1), qt, k, vt, qit, kiw, wit, x, w_out)


def kernel(x, mem, positions, norm_mix, gm_w_in, gm_v_norm, gm_w_s, gm_b_s, gm_w_out, ssd_w_in, ssd_conv_w, ssd_conv_b, ssd_dt_bias, ssd_a_log, ssd_d, ssd_out_norm, ssd_w_out, dsa_w_in, dsa_q_norm, dsa_k_norm, dsa_kidx_norm, dsa_w_out, norm_xa, norm_mem, xa_w_q, xa_w_kv, xa_q_norm, xa_k_norm, xa_w_out, norm_ffn, ffn_w_up, ffn_w_down):
    b, s, d = x.shape
    depth = norm_mix.shape[0]
    mem_k, mem_v = _mem_kv(mem, norm_mem, xa_w_kv, xa_k_norm)
    for i in range(depth):
        kind, j = i % 3, i // 3
        if kind == 0:
            x = _gmlp(x.reshape(b * s, d), norm_mix[i], gm_w_in[j], gm_v_norm[j], gm_w_s[j], gm_b_s[j],
                      gm_w_out[j]).reshape(b, s, d)
        elif kind == 1:
            x = _ssd_mixer(x, norm_mix[i], ssd_w_in[j], ssd_conv_w[j], ssd_conv_b[j], ssd_dt_bias[j],
                           ssd_a_log[j], ssd_d[j], ssd_out_norm[j], ssd_w_out[j])
        else:
            x = _dsa_mixer(x, positions, norm_mix[i], dsa_w_in[j], dsa_q_norm[j], dsa_k_norm[j],
                           dsa_kidx_norm[j], dsa_w_out[j])
        x = _xattn_ffn(x, norm_xa[i], xa_w_q[i], xa_q_norm[i], mem_k[i], mem_v[i], xa_w_out[i],
                       norm_ffn[i], ffn_w_up[i], ffn_w_down[i])
    return x
```

```python
import functools

import jax
import jax.numpy as jnp
from jax import lax
from jax.experimental import pallas as pl
from jax.experimental.pallas import tpu as pltpu

F32 = jnp.float32
BF16 = jnp.bfloat16
I32 = jnp.int32

EPS = 1e-6
ROPE_THETA = 500000.0
CHUNK = 64
GM_BLOCK = 128
GM_GROUPS = 8
SSD_HEAD_DIM = 64
SSD_GROUPS = 8
SSD_STATE = 128
SSD_CONV = 4
SSD_CHUNK = 128
DSA_HEADS = 16
DSA_KV_HEADS = 4
DSA_HEAD_DIM = 64
DSA_IDX_HEADS = 8
DSA_IDX_DIM = 64
DSA_TOPK_MAX = 256
DSA_QTILE = 256
DSA_SAFE_LOGIT = 60.0
XA_HEADS = 4
XA_HEAD_DIM = 128

LANES = 128
VMEM_LIMIT_BYTES = 56 * 1024 * 1024
NEG_BIG = -1e30
INT_MIN = -(2 ** 31)
F32_MIN_NORMAL = 2.0 ** -126


def _params(*semantics):
    return pltpu.CompilerParams(dimension_semantics=semantics, vmem_limit_bytes=VMEM_LIMIT_BYTES)


def _const_spec(shape):
    nd = len(shape)
    return pl.BlockSpec(shape, lambda *_: (0,) * nd, pipeline_mode=pl.Buffered(1))


def _row_tile(n, want):
    t = min(n, want)
    assert n % t == 0, (n, t)
    return t


def _rms(x, gain):
    ms = jnp.mean(x * x, axis=-1, keepdims=True)
    return x * lax.rsqrt(ms + EPS) * gain


def _dot(a, b):
    return jnp.dot(a, b, preferred_element_type=F32)


def _dot_nt(a, b):
    return lax.dot_general(a, b, (((1,), (1,)), ((), ())), preferred_element_type=F32)


def _dot_tn(a, b):
    return lax.dot_general(a, b, (((0,), (0,)), ((), ())), preferred_element_type=F32)


def _split3(a):
    hi = a.astype(BF16)
    r1 = a - hi.astype(F32)
    mid = r1.astype(BF16)
    lo = (r1 - mid.astype(F32)).astype(BF16)
    return hi, mid, lo


def _dot_f32_lhs(a, b_exact):
    hi, mid, lo = _split3(a)
    return _dot(hi, b_exact) + _dot(mid, b_exact) + _dot(lo, b_exact)


def _dot_f32_rhs(a_exact, b):
    hi, mid, lo = _split3(b)
    return _dot(a_exact, hi) + _dot(a_exact, mid) + _dot(a_exact, lo)


def _mem_kv_body(mem_ref, g_ref, w_ref, kn_ref, k_ref, v_ref):
    mn = _rms(mem_ref[0], g_ref[0]).astype(BF16)
    kv = _dot(mn, w_ref[0])
    xa_w = XA_HEADS * XA_HEAD_DIM
    for h in range(XA_HEADS):
        cols = slice(h * XA_HEAD_DIM, (h + 1) * XA_HEAD_DIM)
        k_ref[0, 0, :, cols] = _rms(kv[:, cols], kn_ref[0]).astype(BF16)
    v_ref[0, 0] = kv[:, xa_w:].astype(BF16)


def _mem_kv(mem, norm_mem, w_kv, k_norm):
    depth, d, _ = w_kv.shape
    b, m, _ = mem.shape
    xa_w = XA_HEADS * XA_HEAD_DIM
    out = jax.ShapeDtypeStruct((depth, b, m, xa_w), BF16)
    return pl.pallas_call(
        _mem_kv_body,
        grid=(depth, b),
        in_specs=[
            pl.BlockSpec((1, m, d), lambda i, j: (j, 0, 0)),
            pl.BlockSpec((1, 1, d), lambda i, j: (i, 0, 0)),
            pl.BlockSpec((1, d, 2 * xa_w), lambda i, j: (i, 0, 0)),
            pl.BlockSpec((1, 1, XA_HEAD_DIM), lambda i, j: (i, 0, 0)),
        ],
        out_specs=[pl.BlockSpec((1, 1, m, xa_w), lambda i, j: (i, j, 0, 0))] * 2,
        out_shape=[out, out],
        compiler_params=_params("arbitrary", "arbitrary"),
        name="mem_kv",
    )(mem, norm_mem[:, None, :], w_kv.astype(BF16), k_norm[:, None, :])


def _xattn_ffn_body(x_ref, g_ref, wq_ref, qn_ref, k_ref, v_ref, wo_ref, gf_ref, wu_ref, wd_ref, o_ref,
                    *, hid_chunk):
    x = x_ref[0]
    xn = _rms(x, g_ref[...]).astype(BF16)
    q = _dot(xn, wq_ref[...])
    scale = XA_HEAD_DIM ** -0.5
    heads = []
    for h in range(XA_HEADS):
        cols = slice(h * XA_HEAD_DIM, (h + 1) * XA_HEAD_DIM)
        qh = _rms(q[:, cols], qn_ref[...]).astype(BF16)
        s = _dot_nt(qh, k_ref[0, :, cols]) * scale
        p = jnp.exp(s - jnp.max(s, axis=-1, keepdims=True))
        l = jnp.sum(p, axis=-1, keepdims=True)
        oh = _dot(p.astype(BF16), v_ref[0, :, cols]) / l
        heads.append(oh.astype(BF16))
    o = jnp.concatenate(heads, axis=-1)
    x = x + _dot(o, wo_ref[...])
    xn = _rms(x, gf_ref[...]).astype(BF16)
    acc = x
    for c in range(0, wu_ref.shape[1], hid_chunk):
        h = _dot(xn, wu_ref[:, c:c + hid_chunk])
        h = jnp.square(jnp.maximum(h, 0.0)).astype(BF16)
        acc = acc + _dot(h, wd_ref[c:c + hid_chunk, :])
    o_ref[0] = acc


def _xattn_ffn(x, gain_xa, w_q, q_norm, k, v, w_out, gain_ffn, w_up, w_down, tm=512, hid_chunk=1024):
    b, s, d = x.shape
    m = k.shape[1]
    xa_w = XA_HEADS * XA_HEAD_DIM
    hid = w_up.shape[1]
    tm = _row_tile(s, tm)
    return pl.pallas_call(
        functools.partial(_xattn_ffn_body, hid_chunk=hid_chunk),
        grid=(b, s // tm),
        in_specs=[
            pl.BlockSpec((1, tm, d), lambda i, j: (i, j, 0)),
            _const_spec((1, d)),
            _const_spec((d, xa_w)),
            _const_spec((1, XA_HEAD_DIM)),
            pl.BlockSpec((1, m, xa_w), lambda i, j: (i, 0, 0)),
            pl.BlockSpec((1, m, xa_w), lambda i, j: (i, 0, 0)),
            _const_spec((xa_w, d)),
            _const_spec((1, d)),
            _const_spec((d, hid)),
            _const_spec((hid, d)),
        ],
        out_specs=pl.BlockSpec((1, tm, d), lambda i, j: (i, j, 0)),
        out_shape=jax.ShapeDtypeStruct(x.shape, F32),
        compiler_params=_params("arbitrary", "arbitrary"),
        name="xattn_ffn",
    )(x, gain_xa[None, :], w_q.astype(BF16), q_norm[None, :], k, v, w_out.astype(BF16),
      gain_ffn[None, :], w_up.astype(BF16), w_down.astype(BF16))


def _proj_res_body(x_ref, y_ref, w_ref, o_ref):
    o_ref[...] = x_ref[...] + _dot(y_ref[...], w_ref[...])


def _proj_res(x2, y2, w, tm=512):
    t, d = x2.shape
    k = y2.shape[1]
    tm = _row_tile(t, tm)
    return pl.pallas_call(
        _proj_res_body,
        grid=(t // tm,),
        in_specs=[
            pl.BlockSpec((tm, d), lambda i: (i, 0)),
            pl.BlockSpec((tm, k), lambda i: (i, 0)),
            _const_spec((k, d)),
        ],
        out_specs=pl.BlockSpec((tm, d), lambda i: (i, 0)),
        out_shape=jax.ShapeDtypeStruct(x2.shape, F32),
        compiler_params=_params("arbitrary"),
        name="proj_res",
    )(x2, y2, w.astype(BF16))


def _gmlp_body(x_ref, g_ref, win_ref, vn_ref, ws_ref, bs_ref, wout_ref, o_ref,
               xn_s, u_s, v_s, h_s, *, col_chunk):
    tm = x_ref.shape[0]
    hid = u_s.shape[1]
    gw = hid // GM_GROUPS
    x = x_ref[...]
    xn_s[...] = _rms(x, g_ref[...]).astype(BF16)
    for c in range(0, hid, col_chunk):
        u_s[:, c:c + col_chunk] = jax.nn.gelu(_dot(xn_s[...], win_ref[:, c:c + col_chunk]))
    ssq = jnp.zeros((tm, 1), F32)
    for c in range(0, hid, col_chunk):
        vc = jax.nn.gelu(_dot(xn_s[...], win_ref[:, hid + c:hid + c + col_chunk]))
        v_s[:, c:c + col_chunk] = vc
        ssq = ssq + jnp.sum(vc * vc, axis=-1, keepdims=True)
    inv = lax.rsqrt(ssq / hid + EPS)
    t_i = lax.broadcasted_iota(I32, (GM_BLOCK, GM_BLOCK), 0)
    s_i = lax.broadcasted_iota(I32, (GM_BLOCK, GM_BLOCK), 1)
    causal = (s_i // CHUNK) <= (t_i // CHUNK)
    for g in range(GM_GROUPS):
        cols = slice(g * gw, (g + 1) * gw)
        wsg = jnp.where(causal, ws_ref[g], 0.0).astype(BF16)
        bias = bs_ref[:, g:g + 1]
        for n in range(tm // GM_BLOCK):
            rows = slice(n * GM_BLOCK, (n + 1) * GM_BLOCK)
            vg = (v_s[rows, cols] * inv[rows] * vn_ref[:, cols]).astype(BF16)
            sg = _dot(wsg, vg) + bias
            h_s[rows, cols] = (u_s[rows, cols] * sg).astype(BF16)
    o_ref[...] = x + _dot(h_s[...], wout_ref[...])


def _gmlp(x2, gain, w_in, v_norm, w_s, b_s, w_out, tm=512, col_chunk=1024):
    t, d = x2.shape
    hid = w_out.shape[0]
    tm = _row_tile(t, tm)
    assert tm % GM_BLOCK == 0
    return pl.pallas_call(
        functools.partial(_gmlp_body, col_chunk=col_chunk),
        grid=(t // tm,),
        in_specs=[
            pl.BlockSpec((tm, d), lambda i: (i, 0)),
            _const_spec((1, d)),
            _const_spec((d, 2 * hid)),
            _const_spec((1, hid)),
            _const_spec((GM_GROUPS, GM_BLOCK, GM_BLOCK)),
            _const_spec((GM_BLOCK, GM_GROUPS)),
            _const_spec((hid, d)),
        ],
        out_specs=pl.BlockSpec((tm, d), lambda i: (i, 0)),
        out_shape=jax.ShapeDtypeStruct(x2.shape, F32),
        scratch_shapes=[
            pltpu.VMEM((tm, d), BF16),
            pltpu.VMEM((tm, hid), F32),
            pltpu.VMEM((tm, hid), F32),
            pltpu.VMEM((tm, hid), BF16),
        ],
        compiler_params=_params("arbitrary"),
        name="gmlp",
    )(x2, gain[None, :], w_in.astype(BF16), v_norm[None, :], w_s, b_s.T, w_out.astype(BF16))


def _ssd_in_body(x_ref, g_ref, wz_ref, wx_ref, wdt_ref, z_ref, xbc_ref, dt_ref, *, col_chunk):
    xn = _rms(x_ref[...], g_ref[...]).astype(BF16)
    for c in range(0, wz_ref.shape[1], col_chunk):
        z_ref[:, c:c + col_chunk] = _dot(xn, wz_ref[:, c:c + col_chunk]).astype(BF16)
    for c in range(0, wx_ref.shape[1], col_chunk):
        xbc_ref[:, c:c + col_chunk] = _dot(xn, wx_ref[:, c:c + col_chunk]).astype(BF16)
    dt_ref[...] = _dot(xn, wdt_ref[...])


def _ssd_in(x2, gain, w_z, w_xbc, w_dt, tm=512, col_chunk=1024):
    t, d = x2.shape
    tm = _row_tile(t, tm)
    nz, nx, ndt = w_z.shape[1], w_xbc.shape[1], w_dt.shape[1]
    return pl.pallas_call(
        functools.partial(_ssd_in_body, col_chunk=col_chunk),
        grid=(t // tm,),
        in_specs=[
            pl.BlockSpec((tm, d), lambda i: (i, 0)),
            _const_spec((1, d)),
            _const_spec((d, nz)),
            _const_spec((d, nx)),
            _const_spec((d, ndt)),
        ],
        out_specs=[
            pl.BlockSpec((tm, nz), lambda i: (i, 0)),
            pl.BlockSpec((tm, nx), lambda i: (i, 0)),
            pl.BlockSpec((tm, ndt), lambda i: (i, 0)),
        ],
        out_shape=[
            jax.ShapeDtypeStruct((t, nz), BF16),
            jax.ShapeDtypeStruct((t, nx), BF16),
            jax.ShapeDtypeStruct((t, ndt), F32),
        ],
        compiler_params=_params("arbitrary"),
        name="ssd_in",
    )(x2, gain[None, :], w_z, w_xbc, w_dt)


def _ssd_core_body(z_ref, xbc_ref, dt_ref, cw_ref, cb_ref, dtb_ref, alog_ref, dskip_ref, onorm_ref,
                   tri_ref, expand_ref, shift_ref, y_ref, state_s, tail_s, *, conv_chunk):
    q = SSD_CHUNK
    n_state = SSD_STATE
    d_inner = z_ref.shape[2]
    conv_ch = xbc_ref.shape[2]
    heads_per_group = d_inner // SSD_HEAD_DIM // SSD_GROUPS
    gw = heads_per_group * SSD_HEAD_DIM
    tail = tail_s.shape[0]

    @pl.when(pl.program_id(1) == 0)
    def _():
        state_s[...] = jnp.zeros_like(state_s)
        tail_s[...] = jnp.zeros_like(tail_s)

    xb = xbc_ref[0]
    x_ext = jnp.concatenate([tail_s[...], xb], axis=0)
    tail_s[...] = xb[q - tail:, :]
    pieces = []
    for c0 in range(0, conv_ch, conv_chunk):
        cols = slice(c0, c0 + conv_chunk)
        conv = cb_ref[:, cols] + cw_ref[SSD_CONV - 1:SSD_CONV, cols] * xb[:, cols].astype(F32)
        for k in range(SSD_CONV - 1):
            conv = conv + cw_ref[k:k + 1, cols] * _dot(shift_ref[k], x_ext[:, cols])
        pieces.append(conv * jax.nn.sigmoid(conv))
    xbc = jnp.concatenate(pieces, axis=1)
    xs = xbc[:, :d_inner]
    bm = xbc[:, d_inner:d_inner + SSD_GROUPS * n_state].astype(BF16)
    cm = xbc[:, d_inner + SSD_GROUPS * n_state:].astype(BF16)

    dt_raw = dt_ref[0] + dtb_ref[...]
    dt = jnp.maximum(dt_raw, 0.0) + jnp.log1p(jnp.exp(-jnp.abs(dt_raw)))
    a = dt * (-jnp.exp(alog_ref[...]))
    a_cum = _dot_f32_rhs(tri_ref[...], a)
    a_cum_t = a_cum.T
    expand = expand_ref[...]
    dt_e = _dot_f32_lhs(dt, expand)
    acum_e = _dot_f32_lhs(a_cum, expand)
    alast_e = acum_e[q - 1:q, :]
    xdt = xs * dt_e
    xw = (xdt * jnp.exp(alast_e - acum_e)).astype(BF16)
    xdt_b = xdt.astype(BF16)
    ea_e = jnp.exp(acum_e)
    chunk_decay_e = jnp.exp(alast_e)

    l_i = lax.broadcasted_iota(I32, (q, q), 0)
    s_i = lax.broadcasted_iota(I32, (q, q), 1)
    causal = l_i >= s_i
    lane_g = lax.broadcasted_iota(I32, (q, gw), 1) // SSD_HEAD_DIM

    zf = z_ref[0].astype(F32)
    for g in range(SSD_GROUPS):
        gcols = slice(g * gw, (g + 1) * gw)
        ncols = slice(g * n_state, (g + 1) * n_state)
        cg = cm[:, ncols]
        bg = bm[:, ncols]
        cb = _dot_nt(cg, bg)
        st = state_s[g]
        y_off = _dot(cg, st.astype(BF16)) * ea_e[:, gcols]
        ms = []
        xblk = []
        xg = xdt_b[:, gcols]
        for r in range(heads_per_group):
            h = g * heads_per_group + r
            seg = a_cum[:, h:h + 1] - a_cum_t[h:h + 1, :]
            decay = jnp.exp(jnp.where(causal, seg, -jnp.inf))
            ms.append((decay * cb).astype(BF16))
            xblk.append(jnp.where(lane_g == r, xg, jnp.zeros_like(xg)))
        y_diag = _dot(jnp.concatenate(ms, axis=1), jnp.concatenate(xblk, axis=0))
        y = y_diag + y_off + xs[:, gcols] * dskip_ref[:, gcols]
        state_s[g] = st * chunk_decay_e[:, gcols] + _dot_tn(bg, xw[:, gcols])
        zg = zf[:, gcols]
        gated = y * (zg * jax.nn.sigmoid(zg))
        gated = gated * lax.rsqrt(jnp.mean(gated * gated, axis=-1, keepdims=True) + EPS)
        y_ref[0, :, gcols] = (gated * onorm_ref[:, gcols]).astype(BF16)


def _ssd_core(z, xbc, dt, conv_w, conv_b, dt_bias, a_log, d_skip, out_norm, conv_chunk=1024):
    b, s, d_inner = z.shape
    conv_ch = xbc.shape[2]
    n_heads = d_inner // SSD_HEAD_DIM
    gw = d_inner // SSD_GROUPS
    q = SSD_CHUNK
    tail = 16
    assert s % q == 0 and n_heads <= LANES and dt.shape[2] == LANES and SSD_CONV - 1 <= tail

    def pad_heads(v):
        return jnp.pad(v, (0, LANES - n_heads))[None, :]

    tri = (jnp.arange(q)[:, None] >= jnp.arange(q)[None, :]).astype(BF16)
    expand = (jnp.arange(LANES)[:, None] == (jnp.arange(d_inner) // SSD_HEAD_DIM)[None, :]).astype(BF16)
    src = tail + jnp.arange(q)[None, :, None] - (SSD_CONV - 1 - jnp.arange(SSD_CONV - 1))[:, None, None]
    shift = (jnp.arange(tail + q)[None, None, :] == src).astype(BF16)
    return pl.pallas_call(
        functools.partial(_ssd_core_body, conv_chunk=conv_chunk),
        grid=(b, s // q),
        in_specs=[
            pl.BlockSpec((1, q, d_inner), lambda i, j: (i, j, 0)),
            pl.BlockSpec((1, q, conv_ch), lambda i, j: (i, j, 0)),
            pl.BlockSpec((1, q, LANES), lambda i, j: (i, j, 0)),
            _const_spec((SSD_CONV, conv_ch)),
            _const_spec((1, conv_ch)),
            _const_spec((1, LANES)),
            _const_spec((1, LANES)),
            _const_spec((1, d_inner)),
            _const_spec((1, d_inner)),
            _const_spec((q, q)),
            _const_spec((LANES, d_inner)),
            _const_spec((SSD_CONV - 1, q, tail + q)),
        ],
        out_specs=pl.BlockSpec((1, q, d_inner), lambda i, j: (i, j, 0)),
        out_shape=jax.ShapeDtypeStruct((b, s, d_inner), BF16),
        scratch_shapes=[
            pltpu.VMEM((SSD_GROUPS, SSD_STATE, gw), F32),
            pltpu.VMEM((tail, conv_ch), BF16),
        ],
        compiler_params=_params("arbitrary", "arbitrary"),
        name="ssd_core",
    )(z, xbc, dt, conv_w, conv_b[None, :], pad_heads(dt_bias), pad_heads(a_log),
      jnp.repeat(d_skip, SSD_HEAD_DIM)[None, :], out_norm[None, :], tri, expand, shift)


def _ssd_mixer(x, gain, w_in, conv_w, conv_b, dt_bias, a_log, d_skip, out_norm, w_out):
    b, s, d = x.shape
    d_inner = w_out.shape[0]
    conv_ch = conv_w.shape[1]
    n_heads = d_inner // SSD_HEAD_DIM
    w_in = w_in.astype(BF16)
    w_z = w_in[:, :d_inner]
    w_xbc = w_in[:, d_inner:d_inner + conv_ch]
    w_dt = jnp.pad(w_in[:, d_inner + conv_ch:], ((0, 0), (0, LANES - n_heads)))
    x2 = x.reshape(b * s, d)
    z, xbc, dt = _ssd_in(x2, gain, w_z, w_xbc, w_dt)
    y = _ssd_core(z.reshape(b, s, d_inner), xbc.reshape(b, s, conv_ch), dt.reshape(b, s, LANES),
                  conv_w, conv_b, dt_bias, a_log, d_skip, out_norm)
    return _proj_res(x2, y.reshape(b * s, d_inner), w_out).reshape(b, s, d)


def _halves_rms(x, gain, lane):
    lo = lane < (LANES // 2)
    x2 = x * x
    s_lo = jnp.sum(jnp.where(lo, x2, 0.0), axis=-1, keepdims=True)
    s_hi = jnp.sum(jnp.where(lo, 0.0, x2), axis=-1, keepdims=True)
    ms = jnp.where(lo, s_lo, s_hi) * (2.0 / LANES)
    return x * lax.rsqrt(ms + EPS) * gain


def _rope(x, cos, sin_lo, sin_hi):
    half = DSA_HEAD_DIM // 8
    return x * cos + pltpu.roll(x, LANES - half, 1) * sin_lo + pltpu.roll(x, half, 1) * sin_hi


def _dsa_prep_body(x_ref, g_ref, w_ref, pos_ref, invf_ref, qn_ref, kn_ref, kin_ref,
                   qt_ref, k_ref, vt_ref, qit_ref, kiw_ref, wit_ref):
    tm = x_ref.shape[1]
    rep = DSA_HEADS // DSA_KV_HEADS
    heads_per_block = LANES // DSA_HEAD_DIM
    q_w = DSA_HEADS * DSA_HEAD_DIM
    kv_w = DSA_KV_HEADS * DSA_HEAD_DIM
    qi_w = DSA_IDX_HEADS * DSA_IDX_DIM
    half = DSA_HEAD_DIM // 8
    xn = _rms(x_ref[0], g_ref[...]).astype(BF16)
    pos = pos_ref[0].astype(F32)
    lane = lax.broadcasted_iota(I32, (tm, LANES), 1)
    j = lane % DSA_HEAD_DIM

    ang = pos * invf_ref[...]
    cos, sin = jnp.cos(ang), jnp.sin(ang)
    sin_lo, sin_hi = jnp.where(j < half, -sin, 0.0), jnp.where(j >= half, sin, 0.0)

    def proj(c0, width):
        return _dot(xn, w_ref[:, c0:c0 + width])

    o1, o2, o3, o4 = q_w, q_w + kv_w, q_w + 2 * kv_w, q_w + 2 * kv_w + qi_w
    zeros = jnp.zeros((DSA_HEAD_DIM, tm), F32)
    for c in range(q_w // LANES):
        blk = _halves_rms(proj(c * LANES, LANES), qn_ref[...], lane)
        bt = (_rope(blk, cos, sin_lo, sin_hi) * (DSA_HEAD_DIM ** -0.5)).T
        for e in range(heads_per_block):
            h = c * heads_per_block + e
            rows = bt[e * DSA_HEAD_DIM:(e + 1) * DSA_HEAD_DIM]
            first_half = (h // rep) % heads_per_block == 0
            padded = jnp.concatenate([rows, zeros] if first_half else [zeros, rows], axis=0)
            qt_ref[0, 0, h] = padded.astype(BF16)
    for c in range(0, kv_w, LANES):
        blk = _halves_rms(proj(o1 + c, LANES), kn_ref[...], lane)
        k_ref[0, :, c:c + LANES] = _rope(blk, cos, sin_lo, sin_hi).astype(BF16)
    vt = proj(o2, kv_w).T
    for g in range(DSA_KV_HEADS):
        vt_ref[0, g, 0] = vt[g * DSA_HEAD_DIM:(g + 1) * DSA_HEAD_DIM].astype(BF16)
    for c in range(qi_w // LANES):
        bt = _rope(proj(o3 + c * LANES, LANES), cos, sin_lo, sin_hi).T
        for e in range(heads_per_block):
            qit_ref[0, 0, c * heads_per_block + e] = bt[e * DSA_IDX_DIM:(e + 1) * DSA_IDX_DIM].astype(BF16)
    blk = proj(o4, LANES)
    is_ki = lane < DSA_IDX_DIM
    ms = jnp.sum(jnp.where(is_ki, blk * blk, 0.0), axis=-1, keepdims=True) * (1.0 / DSA_IDX_DIM)
    kin = blk * lax.rsqrt(ms + EPS) * kin_ref[...]
    kiw_ref[0] = _rope(kin, jnp.where(is_ki, cos, 1.0), jnp.where(is_ki, sin_lo, 0.0),
                       jnp.where(is_ki, sin_hi, 0.0)).astype(BF16)
    wt = (blk * (DSA_IDX_HEADS ** -0.5 * DSA_IDX_DIM ** -0.5)).T
    wit_ref[0, 0] = wt[DSA_IDX_DIM:DSA_IDX_DIM + DSA_IDX_HEADS]


def _dsa_prep(x, gain, w_in, positions, q_norm, k_norm, kidx_norm, tm):
    b, s, d = x.shape
    q_w = DSA_HEADS * DSA_HEAD_DIM
    kv_w = DSA_KV_HEADS * DSA_HEAD_DIM
    qi_w = DSA_IDX_HEADS * DSA_IDX_DIM
    n_in = w_in.shape[1]
    assert n_in == q_w + 2 * kv_w + qi_w + DSA_IDX_DIM + DSA_IDX_HEADS
    n_pad = q_w + 2 * kv_w + qi_w + LANES
    w = jnp.pad(w_in.astype(BF16), ((0, 0), (0, n_pad - n_in)))
    assert s % tm == 0 and DSA_IDX_DIM + DSA_IDX_HEADS <= LANES and DSA_IDX_DIM % 8 == 0
    nq = s // tm
    half = DSA_HEAD_DIM // 8
    inv_freq = ROPE_THETA ** (-jnp.arange(half, dtype=F32) / half)
    lane = jnp.arange(LANES)
    j = lane % DSA_HEAD_DIM
    invf = jnp.where(j < 2 * half, inv_freq[j % half], 0.0)[None, :]
    kin = jnp.concatenate([kidx_norm, jnp.ones((LANES - DSA_IDX_DIM,), F32)])[None, :]

    def tok(width, dtype):
        return pl.BlockSpec((1, tm, width), lambda i, t: (i, t, 0)), jax.ShapeDtypeStruct((b, s, width), dtype)

    def per_tile(shape, dtype):
        nd = len(shape)
        return (pl.BlockSpec((1, 1) + shape, lambda i, t: (i, t) + (0,) * nd),
                jax.ShapeDtypeStruct((b, nq) + shape, dtype))

    outs = [
        per_tile((DSA_HEADS, LANES, tm), BF16),
        tok(kv_w, BF16),
        (pl.BlockSpec((1, DSA_KV_HEADS, 1, DSA_HEAD_DIM, tm), lambda i, t: (i, 0, t, 0, 0)),
         jax.ShapeDtypeStruct((b, DSA_KV_HEADS, nq, DSA_HEAD_DIM, tm), BF16)),
        per_tile((DSA_IDX_HEADS, DSA_IDX_DIM, tm), BF16),
        tok(LANES, BF16),
        per_tile((DSA_IDX_HEADS, tm), F32),
    ]
    return pl.pallas_call(
        _dsa_prep_body,
        grid=(b, s // tm),
        in_specs=[
            pl.BlockSpec((1, tm, d), lambda i, t: (i, t, 0)),
            _const_spec((1, d)),
            _const_spec((d, n_pad)),
            pl.BlockSpec((1, tm, 1), lambda i, t: (i, t, 0)),
            _const_spec((1, LANES)),
            _const_spec((1, LANES)),
            _const_spec((1, LANES)),
            _const_spec((1, LANES)),
        ],
        out_specs=[o[0] for o in outs],
        out_shape=[o[1] for o in outs],
        compiler_params=_params("arbitrary", "arbitrary"),
        name="dsa_prep",
    )(x, gain[None, :], w, positions[:, :, None], invf,
      jnp.tile(q_norm, 2)[None, :], jnp.tile(k_norm, 2)[None, :], kin)


def _dsa_attn_body(bound_ref, qt_ref, k_ref, vt_ref, qit_ref, kiw_ref, wit_ref, x_ref, wo_ref, o_ref,
                   key_s, top_s, mask_s, m_s, l_s, acc_s, s_s, o_s, *, topk):
    qb = qt_ref.shape[4]
    tk = qb
    i = pl.program_id(1)
    n_tiles = i + 1
    rep = DSA_HEADS // DSA_KV_HEADS
    krow = lax.broadcasted_iota(I32, (tk, qb), 0)
    qcol = lax.broadcasted_iota(I32, (tk, qb), 1)
    diag_ok = (krow // CHUNK) <= (qcol // CHUNK)

    def score_tile(t):
        kt = kiw_ref[0, pl.ds(pl.multiple_of(t * tk, tk), tk), 0:DSA_IDX_DIM]
        score = jnp.zeros((tk, qb), F32)
        for h in range(DSA_IDX_HEADS):
            score = score + wit_ref[0, 0, h:h + 1, :] * jnp.maximum(_dot(kt, qit_ref[0, 0, h]), 0.0)
        score = jnp.where(jnp.abs(score) < F32_MIN_NORMAL, 0.0, score)
        bits = pltpu.bitcast(score, I32)
        key = bits ^ ((bits >> 31) & jnp.int32(0x7FFFFFFF))
        admissible = (t < i) | diag_ok
        key_s[t] = jnp.where(admissible, key, jnp.int32(INT_MIN))
        top = pltpu.bitcast(bits & jnp.int32(-(1 << 16)), F32)
        top_s[t] = jnp.where(admissible, top, -jnp.inf).astype(BF16)

    def score_tile_pair(pair, carry):
        score_tile(2 * pair)
        score_tile(jnp.minimum(2 * pair + 1, n_tiles - 1))
        return carry

    lax.fori_loop(0, (n_tiles + 1) // 2, score_tile_pair, 0)

    def count(pred_fn):
        def body(t, acc):
            hit = jnp.where(pred_fn(key_s[t]), 1, 0).astype(I32)
            return acc + jnp.sum(hit.reshape(tk // 8, 8, qb), axis=0)
        acc = lax.fori_loop(0, n_tiles, body, jnp.zeros((8, qb), I32))
        return jnp.sum(acc, axis=0, keepdims=True)

    pack = 16

    def count_top(cand_b):
        def body(t, acc):
            hit = jnp.where(top_s[t] >= cand_b, jnp.ones((), BF16), jnp.zeros((), BF16))
            parts = [hit[r * pack:(r + 1) * pack] for r in range(tk // pack)]
            while len(parts) > 1:
                parts = [a + b for a, b in zip(parts[0::2], parts[1::2])]
            return acc + parts[0]
        acc = lax.fori_loop(0, n_tiles, body, jnp.zeros((pack, qb), BF16))
        return jnp.sum(acc.astype(F32), axis=0, keepdims=True)

    key16_neg_inf, key16_pos_inf = -32641, 32640
    key16_min_normal = 128

    def radix_top_step(b, thr16):
        cand = thr16 + lax.shift_left(jnp.int32(1), 15 - b)
        c = jnp.clip(cand, key16_neg_inf, key16_pos_inf)
        c = jnp.where((c >= 1) & (c < key16_min_normal), key16_min_normal, c)
        c = jnp.where((c <= -2) & (c >= -key16_min_normal), -1, c)
        pattern = c ^ ((c >> 15) & jnp.int32(0x7FFF))
        cand_b = pltpu.bitcast(lax.shift_left(pattern, 16), F32).astype(BF16)
        return jnp.where(count_top(cand_b) >= topk, cand, thr16)

    thr16 = lax.fori_loop(0, 16, radix_top_step, jnp.full((1, qb), -(1 << 15), I32))

    def radix_step(b, thr):
        cand = thr + lax.shift_left(jnp.int32(1), 31 - b)
        cnt = count(lambda key: key >= cand)
        return jnp.where(cnt >= topk, cand, thr)

    thr = lax.fori_loop(16, 32, radix_step, lax.shift_left(thr16, 16))
    n_ge = count(lambda key: key >= thr)
    has_cut_tie = jnp.max(n_ge) > topk

    @pl.when(jnp.logical_not(has_cut_tie))
    def _():
        def mask_tile(t, carry):
            key = key_s[t]
            mask_s[t] = jnp.where((key >= thr) & (key != jnp.int32(INT_MIN)), 0.0, NEG_BIG)
            return carry
        lax.fori_loop(0, n_tiles, mask_tile, 0)

    @pl.when(has_cut_tie)
    def _():
        n_gt = count(lambda key: key > thr)
        need = (topk - n_gt).astype(F32)
        strict_lower = (qcol < krow).astype(BF16)
        ones = jnp.ones((tk, tk), BF16)

        def mask_tile(t, before):
            key = key_s[t]
            eq = key == thr
            eqb = jnp.where(eq, 1.0, 0.0).astype(BF16)
            rank = before + _dot(strict_lower, eqb)
            sel = (key > thr) | (eq & (rank < need))
            sel = sel & (key != jnp.int32(INT_MIN))
            mask_s[t] = jnp.where(sel, 0.0, NEG_BIG)
            return before + _dot(ones, eqb)

        lax.fori_loop(0, n_tiles, mask_tile, jnp.zeros((tk, qb), F32))

    l_s[...] = jnp.zeros(l_s.shape, F32)
    acc_s[...] = jnp.zeros(acc_s.shape, F32)
    heads_per_block = LANES // DSA_HEAD_DIM

    def logits(t, h, mask):
        kblk = (h // rep) // heads_per_block
        start = t * tk if isinstance(t, int) else pl.multiple_of(t * tk, tk)
        kt = k_ref[0, pl.ds(start, tk), kblk * LANES:(kblk + 1) * LANES]
        return _dot(kt, qt_ref[0, 0, h]) + mask

    def exp_sum_pv(use_offset):
        n_slots = s_s.shape[0]
        ahead = n_slots - 1
        assert DSA_HEADS % n_slots == 0
        for h in range(ahead):
            s_s[h] = logits(0, h, mask_s[0])

        def attn_tile(t, carry):
            t_next = jnp.minimum(t + 1, n_tiles - 1)
            mask, mask_next = mask_s[t], mask_s[t_next]
            for h in range(DSA_HEADS):
                nxt = h + ahead
                if nxt < DSA_HEADS:
                    s_s[nxt % n_slots] = logits(t, nxt, mask)
                else:
                    s_s[nxt % n_slots] = logits(t_next, nxt - DSA_HEADS, mask_next)
                s = s_s[h % n_slots]
                p = jnp.exp(s - m_s[h, 0:1, :] if use_offset else s)
                l_s[h] = l_s[h] + jnp.sum(p.reshape(tk // 8, 8, qb), axis=0)
                acc_s[h] = acc_s[h] + _dot(vt_ref[0, h // rep, t], p.astype(BF16))
            return carry

        lax.fori_loop(0, n_tiles, attn_tile, 0)

    bounded = bound_ref[0] <= DSA_SAFE_LOGIT

    @pl.when(bounded)
    def _():
        exp_sum_pv(use_offset=False)

    @pl.when(jnp.logical_not(bounded))
    def _():
        m_s[...] = jnp.full(m_s.shape, NEG_BIG, F32)

        def max_tile(t, carry):
            mask = mask_s[t]
            for h in range(DSA_HEADS):
                s = logits(t, h, mask)
                m_s[h] = jnp.maximum(m_s[h], jnp.max(s.reshape(tk // 8, 8, qb), axis=0))
            return carry

        lax.fori_loop(0, n_tiles, max_tile, 0)
        for h in range(DSA_HEADS):
            m_s[h] = jnp.broadcast_to(jnp.max(m_s[h], axis=0, keepdims=True), (8, qb))
        exp_sum_pv(use_offset=True)

    for c in range(DSA_HEADS // heads_per_block):
        pair = []
        for h in range(c * heads_per_block, (c + 1) * heads_per_block):
            pair.append(acc_s[h] / jnp.sum(l_s[h], axis=0, keepdims=True))
        o_s[:, c * LANES:(c + 1) * LANES] = jnp.concatenate(pair, axis=0).T.astype(BF16)
    o_ref[0] = x_ref[0] + _dot(o_s[...], wo_ref[...])


def _dsa_attn(bound, qt, k, vt, qit, kiw, wit, x, w_out):
    b, nq, n_heads, _, qb = qt.shape
    _, s, d = x.shape
    kvh, dh = vt.shape[1], vt.shape[3]
    topk = min(DSA_TOPK_MAX, s // 4)
    return pl.pallas_call(
        functools.partial(_dsa_attn_body, topk=topk),
        grid=(b, nq),
        in_specs=[
            pl.BlockSpec(memory_space=pltpu.SMEM),
            pl.BlockSpec((1, 1, n_heads, LANES, qb), lambda i, j: (i, j, 0, 0, 0)),
            pl.BlockSpec((1, s, kvh * dh), lambda i, j: (i, 0, 0)),
            pl.BlockSpec((1, kvh, nq, dh, qb), lambda i, j: (i, 0, 0, 0, 0)),
            pl.BlockSpec((1, 1, DSA_IDX_HEADS, DSA_IDX_DIM, qb), lambda i, j: (i, j, 0, 0, 0)),
            pl.BlockSpec((1, s, LANES), lambda i, j: (i, 0, 0)),
            pl.BlockSpec((1, 1, DSA_IDX_HEADS, qb), lambda i, j: (i, j, 0, 0)),
            pl.BlockSpec((1, qb, d), lambda i, j: (i, j, 0)),
            _const_spec((n_heads * dh, d)),
        ],
        out_specs=pl.BlockSpec((1, qb, d), lambda i, j: (i, j, 0)),
        out_shape=jax.ShapeDtypeStruct(x.shape, F32),
        scratch_shapes=[
            pltpu.VMEM((nq, qb, qb), I32),
            pltpu.VMEM((nq, qb, qb), BF16),
            pltpu.VMEM((nq, qb, qb), F32),
            pltpu.VMEM((n_heads, 8, qb), F32),
            pltpu.VMEM((n_heads, 8, qb), F32),
            pltpu.VMEM((n_heads, dh, qb), F32),
            pltpu.VMEM((8, qb, qb), F32),
            pltpu.VMEM((qb, n_heads * dh), BF16),
        ],
        compiler_params=_params("arbitrary", "arbitrary"),
        name="dsa_attn",
    )(bound, qt, k, vt, qit, kiw, wit, x, w_out.astype(BF16))


def _dsa_mixer(x, positions, gain, w_in, q_norm, k_norm, kidx_norm, w_out, qb=DSA_QTILE):
    s = x.shape[1]
    qb = _row_tile(s, qb)
    qt, k, vt, qit, kiw, wit = _dsa_prep(x, gain, w_in, positions, q_norm, k_norm, kidx_norm, qb)
    bound = (DSA_HEAD_DIM ** 0.5) * jnp.max(jnp.abs(q_norm)) * jnp.max(jnp.abs(k_norm)) * (1.0 + 2.0 ** -6)
    return _dsa_attn(bound.reshape(1), qt, k, vt, qit, kiw, wit, x, w_out)


def kernel(x, mem, positions, norm_mix, gm_w_in, gm_v_norm, gm_w_s, gm_b_s, gm_w_out, ssd_w_in, ssd_conv_w, ssd_conv_b, ssd_dt_bias, ssd_a_log, ssd_d, ssd_out_norm, ssd_w_out, dsa_w_in, dsa_q_norm, dsa_k_norm, dsa_kidx_norm, dsa_w_out, norm_xa, norm_mem, xa_w_q, xa_w_kv, xa_q_norm, xa_k_norm, xa_w_out, norm_ffn, ffn_w_up, ffn_w_down):
    b, s, d = x.shape
    depth = norm_mix.shape[0]
    mem_k, mem_v = _mem_kv(mem, norm_mem, xa_w_kv, xa_k_norm)
    for i in range(depth):
        kind, j = i % 3, i // 3
        if kind == 0:
            x = _gmlp(x.reshape(b * s, d), norm_mix[i], gm_w_in[j], gm_v_norm[j], gm_w_s[j], gm_b_s[j],
                      gm_w_out[j]).reshape(b, s, d)
        elif kind == 1:
            x = _ssd_mixer(x, norm_mix[i], ssd_w_in[j], ssd_conv_w[j], ssd_conv_b[j], ssd_dt_bias[j],
                           ssd_a_log[j], ssd_d[j], ssd_out_norm[j], ssd_w_out[j])
        else:
            x = _dsa_mixer(x, positions, norm_mix[i], dsa_w_in[j], dsa_q_norm[j], dsa_k_norm[j],
                           dsa_kidx_norm[j], dsa_w_out[j])
        x = _xattn_ffn(x, norm_xa[i], xa_w_q[i], xa_q_norm[i], mem_k[i], mem_v[i], xa_w_out[i],
                       norm_ffn[i], ffn_w_up[i], ffn_w_down[i])
    return x
```

```python
import functools

import jax
import jax.numpy as jnp
from jax import lax
from jax.experimental import pallas as pl
from jax.experimental.pallas import tpu as pltpu

F32 = jnp.float32
BF16 = jnp.bfloat16
I32 = jnp.int32

EPS = 1e-6
ROPE_THETA = 500000.0
CHUNK = 64
GM_BLOCK = 128
GM_GROUPS = 8
SSD_HEAD_DIM = 64
SSD_GROUPS = 8
SSD_STATE = 128
SSD_CONV = 4
SSD_CHUNK = 128
DSA_HEADS = 16
DSA_KV_HEADS = 4
DSA_HEAD_DIM = 64
DSA_IDX_HEADS = 8
DSA_IDX_DIM = 64
DSA_TOPK_MAX = 256
DSA_QTILE = 256
DSA_SAFE_LOGIT = 60.0
XA_HEADS = 4
XA_HEAD_DIM = 128

LANES = 128
VMEM_LIMIT_BYTES = 56 * 1024 * 1024
NEG_BIG = -1e30
INT_MIN = -(2 ** 31)
F32_MIN_NORMAL = 2.0 ** -126


def _params(*semantics):
    return pltpu.CompilerParams(dimension_semantics=semantics, vmem_limit_bytes=VMEM_LIMIT_BYTES)


def _const_spec(shape):
    nd = len(shape)
    return pl.BlockSpec(shape, lambda *_: (0,) * nd, pipeline_mode=pl.Buffered(1))


def _row_tile(n, want):
    t = min(n, want)
    assert n % t == 0, (n, t)
    return t


def _rms(x, gain):
    ms = jnp.mean(x * x, axis=-1, keepdims=True)
    return x * lax.rsqrt(ms + EPS) * gain


def _dot(a, b):
    return jnp.dot(a, b, preferred_element_type=F32)


def _dot_nt(a, b):
    return lax.dot_general(a, b, (((1,), (1,)), ((), ())), preferred_element_type=F32)


def _dot_tn(a, b):
    return lax.dot_general(a, b, (((0,), (0,)), ((), ())), preferred_element_type=F32)


def _split3(a):
    hi = a.astype(BF16)
    r1 = a - hi.astype(F32)
    mid = r1.astype(BF16)
    lo = (r1 - mid.astype(F32)).astype(BF16)
    return hi, mid, lo


def _dot_f32_lhs(a, b_exact):
    hi, mid, lo = _split3(a)
    return _dot(hi, b_exact) + _dot(mid, b_exact) + _dot(lo, b_exact)


def _dot_f32_rhs(a_exact, b):
    hi, mid, lo = _split3(b)
    return _dot(a_exact, hi) + _dot(a_exact, mid) + _dot(a_exact, lo)


def _mem_kv_body(mem_ref, g_ref, w_ref, kn_ref, k_ref, v_ref):
    mn = _rms(mem_ref[0], g_ref[0]).astype(BF16)
    kv = _dot(mn, w_ref[0])
    xa_w = XA_HEADS * XA_HEAD_DIM
    for h in range(XA_HEADS):
        cols = slice(h * XA_HEAD_DIM, (h + 1) * XA_HEAD_DIM)
        k_ref[0, 0, :, cols] = _rms(kv[:, cols], kn_ref[0]).astype(BF16)
    v_ref[0, 0] = kv[:, xa_w:].astype(BF16)


def _mem_kv(mem, norm_mem, w_kv, k_norm):
    depth, d, _ = w_kv.shape
    b, m, _ = mem.shape
    xa_w = XA_HEADS * XA_HEAD_DIM
    out = jax.ShapeDtypeStruct((depth, b, m, xa_w), BF16)
    return pl.pallas_call(
        _mem_kv_body,
        grid=(depth, b),
        in_specs=[
            pl.BlockSpec((1, m, d), lambda i, j: (j, 0, 0)),
            pl.BlockSpec((1, 1, d), lambda i, j: (i, 0, 0)),
            pl.BlockSpec((1, d, 2 * xa_w), lambda i, j: (i, 0, 0)),
            pl.BlockSpec((1, 1, XA_HEAD_DIM), lambda i, j: (i, 0, 0)),
        ],
        out_specs=[pl.BlockSpec((1, 1, m, xa_w), lambda i, j: (i, j, 0, 0))] * 2,
        out_shape=[out, out],
        compiler_params=_params("arbitrary", "arbitrary"),
        name="mem_kv",
    )(mem, norm_mem[:, None, :], w_kv.astype(BF16), k_norm[:, None, :])


def _xattn_ffn_body(x_ref, g_ref, wq_ref, qn_ref, k_ref, v_ref, wo_ref, gf_ref, wu_ref, wd_ref, o_ref,
                    *, hid_chunk):
    x = x_ref[0]
    xn = _rms(x, g_ref[...]).astype(BF16)
    q = _dot(xn, wq_ref[...])
    scale = XA_HEAD_DIM ** -0.5
    heads = []
    for h in range(XA_HEADS):
        cols = slice(h * XA_HEAD_DIM, (h + 1) * XA_HEAD_DIM)
        qh = _rms(q[:, cols], qn_ref[...]).astype(BF16)
        s = _dot_nt(qh, k_ref[0, :, cols]) * scale
        p = jnp.exp(s - jnp.max(s, axis=-1, keepdims=True))
        l = jnp.sum(p, axis=-1, keepdims=True)
        oh = _dot(p.astype(BF16), v_ref[0, :, cols]) / l
        heads.append(oh.astype(BF16))
    o = jnp.concatenate(heads, axis=-1)
    x = x + _dot(o, wo_ref[...])
    xn = _rms(x, gf_ref[...]).astype(BF16)
    acc = x
    for c in range(0, wu_ref.shape[1], hid_chunk):
        h = _dot(xn, wu_ref[:, c:c + hid_chunk])
        h = jnp.square(jnp.maximum(h, 0.0)).astype(BF16)
        acc = acc + _dot(h, wd_ref[c:c + hid_chunk, :])
    o_ref[0] = acc


def _xattn_ffn(x, gain_xa, w_q, q_norm, k, v, w_out, gain_ffn, w_up, w_down, tm=512, hid_chunk=1024):
    b, s, d = x.shape
    m = k.shape[1]
    xa_w = XA_HEADS * XA_HEAD_DIM
    hid = w_up.shape[1]
    tm = _row_tile(s, tm)
    return pl.pallas_call(
        functools.partial(_xattn_ffn_body, hid_chunk=hid_chunk),
        grid=(b, s // tm),
        in_specs=[
            pl.BlockSpec((1, tm, d), lambda i, j: (i, j, 0)),
            _const_spec((1, d)),
            _const_spec((d, xa_w)),
            _const_spec((1, XA_HEAD_DIM)),
            pl.BlockSpec((1, m, xa_w), lambda i, j: (i, 0, 0)),
            pl.BlockSpec((1, m, xa_w), lambda i, j: (i, 0, 0)),
            _const_spec((xa_w, d)),
            _const_spec((1, d)),
            _const_spec((d, hid)),
            _const_spec((hid, d)),
        ],
        out_specs=pl.BlockSpec((1, tm, d), lambda i, j: (i, j, 0)),
        out_shape=jax.ShapeDtypeStruct(x.shape, F32),
        compiler_params=_params("arbitrary", "arbitrary"),
        name="xattn_ffn",
    )(x, gain_xa[None, :], w_q.astype(BF16), q_norm[None, :], k, v, w_out.astype(BF16),
      gain_ffn[None, :], w_up.astype(BF16), w_down.astype(BF16))


def _proj_res_body(x_ref, y_ref, w_ref, o_ref):
    o_ref[...] = x_ref[...] + _dot(y_ref[...], w_ref[...])


def _proj_res(x2, y2, w, tm=512):
    t, d = x2.shape
    k = y2.shape[1]
    tm = _row_tile(t, tm)
    return pl.pallas_call(
        _proj_res_body,
        grid=(t // tm,),
        in_specs=[
            pl.BlockSpec((tm, d), lambda i: (i, 0)),
            pl.BlockSpec((tm, k), lambda i: (i, 0)),
            _const_spec((k, d)),
        ],
        out_specs=pl.BlockSpec((tm, d), lambda i: (i, 0)),
        out_shape=jax.ShapeDtypeStruct(x2.shape, F32),
        compiler_params=_params("arbitrary"),
        name="proj_res",
    )(x2, y2, w.astype(BF16))


def _gmlp_body(x_ref, g_ref, win_ref, vn_ref, ws_ref, bs_ref, wout_ref, o_ref,
               xn_s, u_s, v_s, h_s, *, col_chunk):
    tm = x_ref.shape[0]
    hid = u_s.shape[1]
    gw = hid // GM_GROUPS
    x = x_ref[...]
    xn_s[...] = _rms(x, g_ref[...]).astype(BF16)
    for c in range(0, hid, col_chunk):
        u_s[:, c:c + col_chunk] = jax.nn.gelu(_dot(xn_s[...], win_ref[:, c:c + col_chunk]))
    ssq = jnp.zeros((tm, 1), F32)
    for c in range(0, hid, col_chunk):
        vc = jax.nn.gelu(_dot(xn_s[...], win_ref[:, hid + c:hid + c + col_chunk]))
        v_s[:, c:c + col_chunk] = vc
        ssq = ssq + jnp.sum(vc * vc, axis=-1, keepdims=True)
    inv = lax.rsqrt(ssq / hid + EPS)
    t_i = lax.broadcasted_iota(I32, (GM_BLOCK, GM_BLOCK), 0)
    s_i = lax.broadcasted_iota(I32, (GM_BLOCK, GM_BLOCK), 1)
    causal = (s_i // CHUNK) <= (t_i // CHUNK)
    for g in range(GM_GROUPS):
        cols = slice(g * gw, (g + 1) * gw)
        wsg = jnp.where(causal, ws_ref[g], 0.0).astype(BF16)
        bias = bs_ref[:, g:g + 1]
        for n in range(tm // GM_BLOCK):
            rows = slice(n * GM_BLOCK, (n + 1) * GM_BLOCK)
            vg = (v_s[rows, cols] * inv[rows] * vn_ref[:, cols]).astype(BF16)
            sg = _dot(wsg, vg) + bias
            h_s[rows, cols] = (u_s[rows, cols] * sg).astype(BF16)
    o_ref[...] = x + _dot(h_s[...], wout_ref[...])


def _gmlp(x2, gain, w_in, v_norm, w_s, b_s, w_out, tm=512, col_chunk=1024):
    t, d = x2.shape
    hid = w_out.shape[0]
    tm = _row_tile(t, tm)
    assert tm % GM_BLOCK == 0
    return pl.pallas_call(
        functools.partial(_gmlp_body, col_chunk=col_chunk),
        grid=(t // tm,),
        in_specs=[
            pl.BlockSpec((tm, d), lambda i: (i, 0)),
            _const_spec((1, d)),
            _const_spec((d, 2 * hid)),
            _const_spec((1, hid)),
            _const_spec((GM_GROUPS, GM_BLOCK, GM_BLOCK)),
            _const_spec((GM_BLOCK, GM_GROUPS)),
            _const_spec((hid, d)),
        ],
        out_specs=pl.BlockSpec((tm, d), lambda i: (i, 0)),
        out_shape=jax.ShapeDtypeStruct(x2.shape, F32),
        scratch_shapes=[
            pltpu.VMEM((tm, d), BF16),
            pltpu.VMEM((tm, hid), F32),
            pltpu.VMEM((tm, hid), F32),
            pltpu.VMEM((tm, hid), BF16),
        ],
        compiler_params=_params("arbitrary"),
        name="gmlp",
    )(x2, gain[None, :], w_in.astype(BF16), v_norm[None, :], w_s, b_s.T, w_out.astype(BF16))


def _ssd_in_body(x_ref, g_ref, wz_ref, wx_ref, wdt_ref, z_ref, xbc_ref, dt_ref, *, col_chunk):
    xn = _rms(x_ref[...], g_ref[...]).astype(BF16)
    for c in range(0, wz_ref.shape[1], col_chunk):
        z_ref[:, c:c + col_chunk] = _dot(xn, wz_ref[:, c:c + col_chunk]).astype(BF16)
    for c in range(0, wx_ref.shape[1], col_chunk):
        xbc_ref[:, c:c + col_chunk] = _dot(xn, wx_ref[:, c:c + col_chunk]).astype(BF16)
    dt_ref[...] = _dot(xn, wdt_ref[...])


def _ssd_in(x2, gain, w_z, w_xbc, w_dt, tm=512, col_chunk=1024):
    t, d = x2.shape
    tm = _row_tile(t, tm)
    nz, nx, ndt = w_z.shape[1], w_xbc.shape[1], w_dt.shape[1]
    return pl.pallas_call(
        functools.partial(_ssd_in_body, col_chunk=col_chunk),
        grid=(t // tm,),
        in_specs=[
            pl.BlockSpec((tm, d), lambda i: (i, 0)),
            _const_spec((1, d)),
            _const_spec((d, nz)),
            _const_spec((d, nx)),
            _const_spec((d, ndt)),
        ],
        out_specs=[
            pl.BlockSpec((tm, nz), lambda i: (i, 0)),
            pl.BlockSpec((tm, nx), lambda i: (i, 0)),
            pl.BlockSpec((tm, ndt), lambda i: (i, 0)),
        ],
        out_shape=[
            jax.ShapeDtypeStruct((t, nz), BF16),
            jax.ShapeDtypeStruct((t, nx), BF16),
            jax.ShapeDtypeStruct((t, ndt), F32),
        ],
        compiler_params=_params("arbitrary"),
        name="ssd_in",
    )(x2, gain[None, :], w_z, w_xbc, w_dt)


def _ssd_core_body(z_ref, xbc_ref, dt_ref, cw_ref, cb_ref, dtb_ref, alog_ref, dskip_ref, onorm_ref,
                   tri_ref, expand_ref, shift_ref, y_ref, state_s, tail_s, *, conv_chunk):
    q = SSD_CHUNK
    n_state = SSD_STATE
    d_inner = z_ref.shape[2]
    conv_ch = xbc_ref.shape[2]
    heads_per_group = d_inner // SSD_HEAD_DIM // SSD_GROUPS
    gw = heads_per_group * SSD_HEAD_DIM
    tail = tail_s.shape[0]

    @pl.when(pl.program_id(1) == 0)
    def _():
        state_s[...] = jnp.zeros_like(state_s)
        tail_s[...] = jnp.zeros_like(tail_s)

    xb = xbc_ref[0]
    x_ext = jnp.concatenate([tail_s[...], xb], axis=0)
    tail_s[...] = xb[q - tail:, :]
    pieces = []
    for c0 in range(0, conv_ch, conv_chunk):
        cols = slice(c0, c0 + conv_chunk)
        conv = cb_ref[:, cols] + cw_ref[SSD_CONV - 1:SSD_CONV, cols] * xb[:, cols].astype(F32)
        for k in range(SSD_CONV - 1):
            conv = conv + cw_ref[k:k + 1, cols] * _dot(shift_ref[k], x_ext[:, cols])
        pieces.append(conv * jax.nn.sigmoid(conv))
    xbc = jnp.concatenate(pieces, axis=1)
    xs = xbc[:, :d_inner]
    bm = xbc[:, d_inner:d_inner + SSD_GROUPS * n_state].astype(BF16)
    cm = xbc[:, d_inner + SSD_GROUPS * n_state:].astype(BF16)

    dt_raw = dt_ref[0] + dtb_ref[...]
    dt = jnp.maximum(dt_raw, 0.0) + jnp.log1p(jnp.exp(-jnp.abs(dt_raw)))
    a = dt * (-jnp.exp(alog_ref[...]))
    a_cum = _dot_f32_rhs(tri_ref[...], a)
    a_cum_t = a_cum.T
    expand = expand_ref[...]
    dt_e = _dot_f32_lhs(dt, expand)
    acum_e = _dot_f32_lhs(a_cum, expand)
    alast_e = acum_e[q - 1:q, :]
    xdt = xs * dt_e
    xw = (xdt * jnp.exp(alast_e - acum_e)).astype(BF16)
    xdt_b = xdt.astype(BF16)
    ea_e = jnp.exp(acum_e)
    chunk_decay_e = jnp.exp(alast_e)

    l_i = lax.broadcasted_iota(I32, (q, q), 0)
    s_i = lax.broadcasted_iota(I32, (q, q), 1)
    causal = l_i >= s_i
    lane_g = lax.broadcasted_iota(I32, (q, gw), 1) // SSD_HEAD_DIM

    zf = z_ref[0].astype(F32)
    for g in range(SSD_GROUPS):
        gcols = slice(g * gw, (g + 1) * gw)
        ncols = slice(g * n_state, (g + 1) * n_state)
        cg = cm[:, ncols]
        bg = bm[:, ncols]
        cb = _dot_nt(cg, bg)
        st = state_s[g]
        y_off = _dot(cg, st.astype(BF16)) * ea_e[:, gcols]
        ms = []
        xblk = []
        xg = xdt_b[:, gcols]
        for r in range(heads_per_group):
            h = g * heads_per_group + r
            seg = a_cum[:, h:h + 1] - a_cum_t[h:h + 1, :]
            decay = jnp.exp(jnp.where(causal, seg, -jnp.inf))
            ms.append((decay * cb).astype(BF16))
            xblk.append(jnp.where(lane_g == r, xg, jnp.zeros_like(xg)))
        y_diag = _dot(jnp.concatenate(ms, axis=1), jnp.concatenate(xblk, axis=0))
        y = y_diag + y_off + xs[:, gcols] * dskip_ref[:, gcols]
        state_s[g] = st * chunk_decay_e[:, gcols] + _dot_tn(bg, xw[:, gcols])
        zg = zf[:, gcols]
        gated = y * (zg * jax.nn.sigmoid(zg))
        gated = gated * lax.rsqrt(jnp.mean(gated * gated, axis=-1, keepdims=True) + EPS)
        y_ref[0, :, gcols] = (gated * onorm_ref[:, gcols]).astype(BF16)


def _ssd_core(z, xbc, dt, conv_w, conv_b, dt_bias, a_log, d_skip, out_norm, conv_chunk=1024):
    b, s, d_inner = z.shape
    conv_ch = xbc.shape[2]
    n_heads = d_inner // SSD_HEAD_DIM
    gw = d_inner // SSD_GROUPS
    q = SSD_CHUNK
    tail = 16
    assert s % q == 0 and n_heads <= LANES and dt.shape[2] == LANES and SSD_CONV - 1 <= tail

    def pad_heads(v):
        return jnp.pad(v, (0, LANES - n_heads))[None, :]

    tri = (jnp.arange(q)[:, None] >= jnp.arange(q)[None, :]).astype(BF16)
    expand = (jnp.arange(LANES)[:, None] == (jnp.arange(d_inner) // SSD_HEAD_DIM)[None, :]).astype(BF16)
    src = tail + jnp.arange(q)[None, :, None] - (SSD_CONV - 1 - jnp.arange(SSD_CONV - 1))[:, None, None]
    shift = (jnp.arange(tail + q)[None, None, :] == src).astype(BF16)
    return pl.pallas_call(
        functools.partial(_ssd_core_body, conv_chunk=conv_chunk),
        grid=(b, s // q),
        in_specs=[
            pl.BlockSpec((1, q, d_inner), lambda i, j: (i, j, 0)),
            pl.BlockSpec((1, q, conv_ch), lambda i, j: (i, j, 0)),
            pl.BlockSpec((1, q, LANES), lambda i, j: (i, j, 0)),
            _const_spec((SSD_CONV, conv_ch)),
            _const_spec((1, conv_ch)),
            _const_spec((1, LANES)),
            _const_spec((1, LANES)),
            _const_spec((1, d_inner)),
            _const_spec((1, d_inner)),
            _const_spec((q, q)),
            _const_spec((LANES, d_inner)),
            _const_spec((SSD_CONV - 1, q, tail + q)),
        ],
        out_specs=pl.BlockSpec((1, q, d_inner), lambda i, j: (i, j, 0)),
        out_shape=jax.ShapeDtypeStruct((b, s, d_inner), BF16),
        scratch_shapes=[
            pltpu.VMEM((SSD_GROUPS, SSD_STATE, gw), F32),
            pltpu.VMEM((tail, conv_ch), BF16),
        ],
        compiler_params=_params("arbitrary", "arbitrary"),
        name="ssd_core",
    )(z, xbc, dt, conv_w, conv_b[None, :], pad_heads(dt_bias), pad_heads(a_log),
      jnp.repeat(d_skip, SSD_HEAD_DIM)[None, :], out_norm[None, :], tri, expand, shift)


def _ssd_mixer(x, gain, w_in, conv_w, conv_b, dt_bias, a_log, d_skip, out_norm, w_out):
    b, s, d = x.shape
    d_inner = w_out.shape[0]
    conv_ch = conv_w.shape[1]
    n_heads = d_inner // SSD_HEAD_DIM
    w_in = w_in.astype(BF16)
    w_z = w_in[:, :d_inner]
    w_xbc = w_in[:, d_inner:d_inner + conv_ch]
    w_dt = jnp.pad(w_in[:, d_inner + conv_ch:], ((0, 0), (0, LANES - n_heads)))
    x2 = x.reshape(b * s, d)
    z, xbc, dt = _ssd_in(x2, gain, w_z, w_xbc, w_dt)
    y = _ssd_core(z.reshape(b, s, d_inner), xbc.reshape(b, s, conv_ch), dt.reshape(b, s, LANES),
                  conv_w, conv_b, dt_bias, a_log, d_skip, out_norm)
    return _proj_res(x2, y.reshape(b * s, d_inner), w_out).reshape(b, s, d)


def _halves_rms(x, gain, lane):
    lo = lane < (LANES // 2)
    x2 = x * x
    s_lo = jnp.sum(jnp.where(lo, x2, 0.0), axis=-1, keepdims=True)
    s_hi = jnp.sum(jnp.where(lo, 0.0, x2), axis=-1, keepdims=True)
    ms = jnp.where(lo, s_lo, s_hi) * (2.0 / LANES)
    return x * lax.rsqrt(ms + EPS) * gain


def _rope(x, cos, sin_lo, sin_hi):
    half = DSA_HEAD_DIM // 8
    return x * cos + pltpu.roll(x, LANES - half, 1) * sin_lo + pltpu.roll(x, half, 1) * sin_hi


def _dsa_prep_body(x_ref, g_ref, w_ref, pos_ref, invf_ref, qn_ref, kn_ref, kin_ref,
                   qt_ref, k_ref, vt_ref, qit_ref, kiw_ref, wit_ref):
    tm = x_ref.shape[1]
    rep = DSA_HEADS // DSA_KV_HEADS
    heads_per_block = LANES // DSA_HEAD_DIM
    q_w = DSA_HEADS * DSA_HEAD_DIM
    kv_w = DSA_KV_HEADS * DSA_HEAD_DIM
    qi_w = DSA_IDX_HEADS * DSA_IDX_DIM
    half = DSA_HEAD_DIM // 8
    xn = _rms(x_ref[0], g_ref[...]).astype(BF16)
    pos = pos_ref[0].astype(F32)
    lane = lax.broadcasted_iota(I32, (tm, LANES), 1)
    j = lane % DSA_HEAD_DIM

    ang = pos * invf_ref[...]
    cos, sin = jnp.cos(ang), jnp.sin(ang)
    sin_lo, sin_hi = jnp.where(j < half, -sin, 0.0), jnp.where(j >= half, sin, 0.0)

    def proj(c0, width):
        return _dot(xn, w_ref[:, c0:c0 + width])

    o1, o2, o3, o4 = q_w, q_w + kv_w, q_w + 2 * kv_w, q_w + 2 * kv_w + qi_w
    zeros = jnp.zeros((DSA_HEAD_DIM, tm), F32)
    for c in range(q_w // LANES):
        blk = _halves_rms(proj(c * LANES, LANES), qn_ref[...], lane)
        bt = (_rope(blk, cos, sin_lo, sin_hi) * (DSA_HEAD_DIM ** -0.5)).T
        for e in range(heads_per_block):
            h = c * heads_per_block + e
            rows = bt[e * DSA_HEAD_DIM:(e + 1) * DSA_HEAD_DIM]
            first_half = (h // rep) % heads_per_block == 0
            padded = jnp.concatenate([rows, zeros] if first_half else [zeros, rows], axis=0)
            qt_ref[0, 0, h] = padded.astype(BF16)
    for c in range(0, kv_w, LANES):
        blk = _halves_rms(proj(o1 + c, LANES), kn_ref[...], lane)
        k_ref[0, :, c:c + LANES] = _rope(blk, cos, sin_lo, sin_hi).astype(BF16)
    vt = proj(o2, kv_w).T
    for g in range(DSA_KV_HEADS):
        vt_ref[0, g, 0] = vt[g * DSA_HEAD_DIM:(g + 1) * DSA_HEAD_DIM].astype(BF16)
    for c in range(qi_w // LANES):
        bt = _rope(proj(o3 + c * LANES, LANES), cos, sin_lo, sin_hi).T
        for e in range(heads_per_block):
            qit_ref[0, 0, c * heads_per_block + e] = bt[e * DSA_IDX_DIM:(e + 1) * DSA_IDX_DIM].astype(BF16)
    blk = proj(o4, LANES)
    is_ki = lane < DSA_IDX_DIM
    ms = jnp.sum(jnp.where(is_ki, blk * blk, 0.0), axis=-1, keepdims=True) * (1.0 / DSA_IDX_DIM)
    kin = blk * lax.rsqrt(ms + EPS) * kin_ref[...]
    kiw_ref[0] = _rope(kin, jnp.where(is_ki, cos, 1.0), jnp.where(is_ki, sin_lo, 0.0),
                       jnp.where(is_ki, sin_hi, 0.0)).astype(BF16)
    wt = (blk * (DSA_IDX_HEADS ** -0.5 * DSA_IDX_DIM ** -0.5)).T
    wit_ref[0, 0] = wt[DSA_IDX_DIM:DSA_IDX_DIM + DSA_IDX_HEADS]


def _dsa_prep(x, gain, w_in, positions, q_norm, k_norm, kidx_norm, tm):
    b, s, d = x.shape
    q_w = DSA_HEADS * DSA_HEAD_DIM
    kv_w = DSA_KV_HEADS * DSA_HEAD_DIM
    qi_w = DSA_IDX_HEADS * DSA_IDX_DIM
    n_in = w_in.shape[1]
    assert n_in == q_w + 2 * kv_w + qi_w + DSA_IDX_DIM + DSA_IDX_HEADS
    n_pad = q_w + 2 * kv_w + qi_w + LANES
    w = jnp.pad(w_in.astype(BF16), ((0, 0), (0, n_pad - n_in)))
    assert s % tm == 0 and DSA_IDX_DIM + DSA_IDX_HEADS <= LANES and DSA_IDX_DIM % 8 == 0
    nq = s // tm
    half = DSA_HEAD_DIM // 8
    inv_freq = ROPE_THETA ** (-jnp.arange(half, dtype=F32) / half)
    lane = jnp.arange(LANES)
    j = lane % DSA_HEAD_DIM
    invf = jnp.where(j < 2 * half, inv_freq[j % half], 0.0)[None, :]
    kin = jnp.concatenate([kidx_norm, jnp.ones((LANES - DSA_IDX_DIM,), F32)])[None, :]

    def tok(width, dtype):
        return pl.BlockSpec((1, tm, width), lambda i, t: (i, t, 0)), jax.ShapeDtypeStruct((b, s, width), dtype)

    def per_tile(shape, dtype):
        nd = len(shape)
        return (pl.BlockSpec((1, 1) + shape, lambda i, t: (i, t) + (0,) * nd),
                jax.ShapeDtypeStruct((b, nq) + shape, dtype))

    outs = [
        per_tile((DSA_HEADS, LANES, tm), BF16),
        tok(kv_w, BF16),
        (pl.BlockSpec((1, DSA_KV_HEADS, 1, DSA_HEAD_DIM, tm), lambda i, t: (i, 0, t, 0, 0)),
         jax.ShapeDtypeStruct((b, DSA_KV_HEADS, nq, DSA_HEAD_DIM, tm), BF16)),
        per_tile((DSA_IDX_HEADS, DSA_IDX_DIM, tm), BF16),
        tok(LANES, BF16),
        per_tile((DSA_IDX_HEADS, tm), F32),
    ]
    return pl.pallas_call(
        _dsa_prep_body,
        grid=(b, s // tm),
        in_specs=[
            pl.BlockSpec((1, tm, d), lambda i, t: (i, t, 0)),
            _const_spec((1, d)),
            _const_spec((d, n_pad)),
            pl.BlockSpec((1, tm, 1), lambda i, t: (i, t, 0)),
            _const_spec((1, LANES)),
            _const_spec((1, LANES)),
            _const_spec((1, LANES)),
            _const_spec((1, LANES)),
        ],
        out_specs=[o[0] for o in outs],
        out_shape=[o[1] for o in outs],
        compiler_params=_params("arbitrary", "arbitrary"),
        name="dsa_prep",
    )(x, gain[None, :], w, positions[:, :, None], invf,
      jnp.tile(q_norm, 2)[None, :], jnp.tile(k_norm, 2)[None, :], kin)


def _dsa_attn_body(bound_ref, qt_ref, k_ref, vt_ref, qit_ref, kiw_ref, wit_ref, x_ref, wo_ref, o_ref,
                   key_s, top_s, mask_s, m_s, l_s, acc_s, s_s, o_s, *, topk):
    qb = qt_ref.shape[4]
    tk = qb
    i = pl.program_id(1)
    n_tiles = i + 1
    rep = DSA_HEADS // DSA_KV_HEADS
    krow = lax.broadcasted_iota(I32, (tk, qb), 0)
    qcol = lax.broadcasted_iota(I32, (tk, qb), 1)
    diag_ok = (krow // CHUNK) <= (qcol // CHUNK)

    def score_tile(t):
        kt = kiw_ref[0, pl.ds(pl.multiple_of(t * tk, tk), tk), 0:DSA_IDX_DIM]
        score = jnp.zeros((tk, qb), F32)
        for h in range(DSA_IDX_HEADS):
            score = score + wit_ref[0, 0, h:h + 1, :] * jnp.maximum(_dot(kt, qit_ref[0, 0, h]), 0.0)
        score = jnp.where(jnp.abs(score) < F32_MIN_NORMAL, 0.0, score)
        bits = pltpu.bitcast(score, I32)
        key = bits ^ ((bits >> 31) & jnp.int32(0x7FFFFFFF))
        admissible = (t < i) | diag_ok
        key_s[t] = jnp.where(admissible, key, jnp.int32(INT_MIN))
        top = pltpu.bitcast(bits & jnp.int32(-(1 << 16)), F32)
        top_s[t] = jnp.where(admissible, top, -jnp.inf).astype(BF16)

    def score_tile_pair(pair, carry):
        score_tile(2 * pair)
        score_tile(jnp.minimum(2 * pair + 1, n_tiles - 1))
        return carry

    n_pairs = (n_tiles + 1) // 2
    lax.fori_loop(0, n_pairs, score_tile_pair, 0)

    @pl.when(n_tiles % 2 == 1)
    def _():
        key_s[n_tiles] = jnp.full((tk, qb), INT_MIN, I32)
        top_s[n_tiles] = jnp.full((tk, qb), -jnp.inf, BF16)

    def count(pred_fn):
        def body(pair, acc):
            for t in (2 * pair, 2 * pair + 1):
                hit = jnp.where(pred_fn(key_s[t]), 1, 0).astype(I32)
                acc = acc + jnp.sum(hit.reshape(tk // 8, 8, qb), axis=0)
            return acc
        acc = lax.fori_loop(0, n_pairs, body, jnp.zeros((8, qb), I32))
        return jnp.sum(acc, axis=0, keepdims=True)

    pack = 16

    def count_top(cand_b):
        def body(pair, acc):
            for t in (2 * pair, 2 * pair + 1):
                hit = jnp.where(top_s[t] >= cand_b, jnp.ones((), BF16), jnp.zeros((), BF16))
                parts = [hit[r * pack:(r + 1) * pack] for r in range(tk // pack)]
                while len(parts) > 1:
                    parts = [a + b for a, b in zip(parts[0::2], parts[1::2])]
                acc = acc + parts[0]
            return acc
        acc = lax.fori_loop(0, n_pairs, body, jnp.zeros((pack, qb), BF16))
        return jnp.sum(acc.astype(F32), axis=0, keepdims=True)

    key16_neg_inf, key16_pos_inf = -32641, 32640
    key16_min_normal = 128

    def radix_top_step(b, carry):
        thr16, n_ge = carry
        cand = thr16 + lax.shift_left(jnp.int32(1), 15 - b)
        c = jnp.clip(cand, key16_neg_inf, key16_pos_inf)
        c = jnp.where((c >= 1) & (c < key16_min_normal), key16_min_normal, c)
        c = jnp.where((c <= -2) & (c >= -key16_min_normal), -1, c)
        pattern = c ^ ((c >> 15) & jnp.int32(0x7FFF))
        cand_b = pltpu.bitcast(lax.shift_left(pattern, 16), F32).astype(BF16)
        cnt = count_top(cand_b).astype(I32)
        accept = cnt >= topk
        return jnp.where(accept, cand, thr16), jnp.where(accept, cnt, n_ge)

    thr16, n_ge = lax.fori_loop(0, 16, radix_top_step,
                                (jnp.full((1, qb), -(1 << 15), I32), jnp.full((1, qb), n_pairs * 2 * tk, I32)))
    n_ge = jnp.where(thr16 <= key16_neg_inf, 0, n_ge)

    def radix_step(b, carry):
        thr, n_ge = carry
        cand = thr + lax.shift_left(jnp.int32(1), 31 - b)
        cnt = count(lambda key: key >= cand)
        accept = cnt >= topk
        return jnp.where(accept, cand, thr), jnp.where(accept, cnt, n_ge)

    thr, n_ge = lax.fori_loop(16, 32, radix_step, (lax.shift_left(thr16, 16), n_ge))
    has_cut_tie = jnp.max(n_ge) > topk

    @pl.when(jnp.logical_not(has_cut_tie))
    def _():
        def mask_tile(t, carry):
            key = key_s[t]
            mask_s[t] = jnp.where((key >= thr) & (key != jnp.int32(INT_MIN)), 0.0, NEG_BIG)
            return carry
        lax.fori_loop(0, n_tiles, mask_tile, 0)

    @pl.when(has_cut_tie)
    def _():
        n_gt = count(lambda key: key > thr)
        need = (topk - n_gt).astype(F32)
        strict_lower = (qcol < krow).astype(BF16)
        ones = jnp.ones((tk, tk), BF16)

        def mask_tile(t, before):
            key = key_s[t]
            eq = key == thr
            eqb = jnp.where(eq, 1.0, 0.0).astype(BF16)
            rank = before + _dot(strict_lower, eqb)
            sel = (key > thr) | (eq & (rank < need))
            sel = sel & (key != jnp.int32(INT_MIN))
            mask_s[t] = jnp.where(sel, 0.0, NEG_BIG)
            return before + _dot(ones, eqb)

        lax.fori_loop(0, n_tiles, mask_tile, jnp.zeros((tk, qb), F32))

    l_s[...] = jnp.zeros(l_s.shape, F32)
    acc_s[...] = jnp.zeros(acc_s.shape, F32)
    heads_per_block = LANES // DSA_HEAD_DIM

    def logits(t, h, mask):
        kblk = (h // rep) // heads_per_block
        start = t * tk if isinstance(t, int) else pl.multiple_of(t * tk, tk)
        kt = k_ref[0, pl.ds(start, tk), kblk * LANES:(kblk + 1) * LANES]
        return _dot(kt, qt_ref[0, 0, h]) + mask

    def exp_sum_pv(use_offset):
        n_slots = s_s.shape[0]
        ahead = n_slots - 1
        assert DSA_HEADS % n_slots == 0
        for h in range(ahead):
            s_s[h] = logits(0, h, mask_s[0])

        def attn_tile(t, carry):
            t_next = jnp.minimum(t + 1, n_tiles - 1)
            mask, mask_next = mask_s[t], mask_s[t_next]
            for h in range(DSA_HEADS):
                nxt = h + ahead
                if nxt < DSA_HEADS:
                    s_s[nxt % n_slots] = logits(t, nxt, mask)
                else:
                    s_s[nxt % n_slots] = logits(t_next, nxt - DSA_HEADS, mask_next)
                s = s_s[h % n_slots]
                p = jnp.exp(s - m_s[h, 0:1, :] if use_offset else s)
                l_s[h] = l_s[h] + jnp.sum(p.reshape(tk // 8, 8, qb), axis=0)
                acc_s[h] = acc_s[h] + _dot(vt_ref[0, h // rep, t], p.astype(BF16))
            return carry

        lax.fori_loop(0, n_tiles, attn_tile, 0)

    bounded = bound_ref[0] <= DSA_SAFE_LOGIT

    @pl.when(bounded)
    def _():
        exp_sum_pv(use_offset=False)

    @pl.when(jnp.logical_not(bounded))
    def _():
        m_s[...] = jnp.full(m_s.shape, NEG_BIG, F32)

        def max_tile(t, carry):
            mask = mask_s[t]
            for h in range(DSA_HEADS):
                s = logits(t, h, mask)
                m_s[h] = jnp.maximum(m_s[h], jnp.max(s.reshape(tk // 8, 8, qb), axis=0))
            return carry

        lax.fori_loop(0, n_tiles, max_tile, 0)
        for h in range(DSA_HEADS):
            m_s[h] = jnp.broadcast_to(jnp.max(m_s[h], axis=0, keepdims=True), (8, qb))
        exp_sum_pv(use_offset=True)

    for c in range(DSA_HEADS // heads_per_block):
        pair = []
        for h in range(c * heads_per_block, (c + 1) * heads_per_block):
            pair.append(acc_s[h] / jnp.sum(l_s[h], axis=0, keepdims=True))
        o_s[:, c * LANES:(c + 1) * LANES] = jnp.concatenate(pair, axis=0).T.astype(BF16)
    o_ref[0] = x_ref[0] + _dot(o_s[...], wo_ref[...])


def _dsa_attn(bound, qt, k, vt, qit, kiw, wit, x, w_out):
    b, nq, n_heads, _, qb = qt.shape
    _, s, d = x.shape
    kvh, dh = vt.shape[1], vt.shape[3]
    topk = min(DSA_TOPK_MAX, s // 4)
    n_key_tiles = nq + nq % 2
    assert (qb // 16) * n_key_tiles <= 256, "packed bf16 hit counts must stay exact"
    return pl.pallas_call(
        functools.partial(_dsa_attn_body, topk=topk),
        grid=(b, nq),
        in_specs=[
            pl.BlockSpec(memory_space=pltpu.SMEM),
            pl.BlockSpec((1, 1, n_heads, LANES, qb), lambda i, j: (i, j, 0, 0, 0)),
            pl.BlockSpec((1, s, kvh * dh), lambda i, j: (i, 0, 0)),
            pl.BlockSpec((1, kvh, nq, dh, qb), lambda i, j: (i, 0, 0, 0, 0)),
            pl.BlockSpec((1, 1, DSA_IDX_HEADS, DSA_IDX_DIM, qb), lambda i, j: (i, j, 0, 0, 0)),
            pl.BlockSpec((1, s, LANES), lambda i, j: (i, 0, 0)),
            pl.BlockSpec((1, 1, DSA_IDX_HEADS, qb), lambda i, j: (i, j, 0, 0)),
            pl.BlockSpec((1, qb, d), lambda i, j: (i, j, 0)),
            _const_spec((n_heads * dh, d)),
        ],
        out_specs=pl.BlockSpec((1, qb, d), lambda i, j: (i, j, 0)),
        out_shape=jax.ShapeDtypeStruct(x.shape, F32),
        scratch_shapes=[
            pltpu.VMEM((n_key_tiles, qb, qb), I32),
            pltpu.VMEM((n_key_tiles, qb, qb), BF16),
            pltpu.VMEM((nq, qb, qb), F32),
            pltpu.VMEM((n_heads, 8, qb), F32),
            pltpu.VMEM((n_heads, 8, qb), F32),
            pltpu.VMEM((n_heads, dh, qb), F32),
            pltpu.VMEM((8, qb, qb), F32),
            pltpu.VMEM((qb, n_heads * dh), BF16),
        ],
        compiler_params=_params("arbitrary", "arbitrary"),
        name="dsa_attn",
    )(bound, qt, k, vt, qit, kiw, wit, x, w_out.astype(BF16))


def _dsa_mixer(x, positions, gain, w_in, q_norm, k_norm, kidx_norm, w_out, qb=DSA_QTILE):
    s = x.shape[1]
    qb = _row_tile(s, qb)
    qt, k, vt, qit, kiw, wit = _dsa_prep(x, gain, w_in, positions, q_norm, k_norm, kidx_norm, qb)
    bound = (DSA_HEAD_DIM ** 0.5) * jnp.max(jnp.abs(q_norm)) * jnp.max(jnp.abs(k_norm)) * (1.0 + 2.0 ** -6)
    return _dsa_attn(bound.reshape(1), qt, k, vt, qit, kiw, wit, x, w_out)


def kernel(x, mem, positions, norm_mix, gm_w_in, gm_v_norm, gm_w_s, gm_b_s, gm_w_out, ssd_w_in, ssd_conv_w, ssd_conv_b, ssd_dt_bias, ssd_a_log, ssd_d, ssd_out_norm, ssd_w_out, dsa_w_in, dsa_q_norm, dsa_k_norm, dsa_kidx_norm, dsa_w_out, norm_xa, norm_mem, xa_w_q, xa_w_kv, xa_q_norm, xa_k_norm, xa_w_out, norm_ffn, ffn_w_up, ffn_w_down):
    b, s, d = x.shape
    depth = norm_mix.shape[0]
    mem_k, mem_v = _mem_kv(mem, norm_mem, xa_w_kv, xa_k_norm)
    for i in range(depth):
        kind, j = i % 3, i // 3
        if kind == 0:
            x = _gmlp(x.reshape(b * s, d), norm_mix[i], gm_w_in[j], gm_v_norm[j], gm_w_s[j], gm_b_s[j],
                      gm_w_out[j]).reshape(b, s, d)
        elif kind == 1:
            x = _ssd_mixer(x, norm_mix[i], ssd_w_in[j], ssd_conv_w[j], ssd_conv_b[j], ssd_dt_bias[j],
                           ssd_a_log[j], ssd_d[j], ssd_out_norm[j], ssd_w_out[j])
        else:
            x = _dsa_mixer(x, positions, norm_mix[i], dsa_w_in[j], dsa_q_norm[j], dsa_k_norm[j],
                           dsa_kidx_norm[j], dsa_w_out[j])
        x = _xattn_ffn(x, norm_xa[i], xa_w_q[i], xa_q_norm[i], mem_k[i], mem_v[i], xa_w_out[i],
                       norm_ffn[i], ffn_w_up[i], ffn_w_down[i])
    return x
```

```python
import functools

import jax
import jax.numpy as jnp
from jax import lax
from jax.experimental import pallas as pl
from jax.experimental.pallas import tpu as pltpu

F32 = jnp.float32
BF16 = jnp.bfloat16
I32 = jnp.int32

EPS = 1e-6
ROPE_THETA = 500000.0
CHUNK = 64
GM_BLOCK = 128
GM_GROUPS = 8
SSD_HEAD_DIM = 64
SSD_GROUPS = 8
SSD_STATE = 128
SSD_CONV = 4
SSD_CHUNK = 128
DSA_HEADS = 16
DSA_KV_HEADS = 4
DSA_HEAD_DIM = 64
DSA_IDX_HEADS = 8
DSA_IDX_DIM = 64
DSA_TOPK_MAX = 256
DSA_QTILE = 256
DSA_SAFE_LOGIT = 60.0
RADIX_CHECK_BITS = (25, 28, 30)
XA_HEADS = 4
XA_HEAD_DIM = 128

LANES = 128
VMEM_LIMIT_BYTES = 56 * 1024 * 1024
NEG_BIG = -1e30
INT_MIN = -(2 ** 31)
F32_MIN_NORMAL = 2.0 ** -126


def _params(*semantics):
    return pltpu.CompilerParams(dimension_semantics=semantics, vmem_limit_bytes=VMEM_LIMIT_BYTES)


def _const_spec(shape):
    nd = len(shape)
    return pl.BlockSpec(shape, lambda *_: (0,) * nd, pipeline_mode=pl.Buffered(1))


def _row_tile(n, want):
    t = min(n, want)
    assert n % t == 0, (n, t)
    return t


def _rms(x, gain):
    ms = jnp.mean(x * x, axis=-1, keepdims=True)
    return x * lax.rsqrt(ms + EPS) * gain


def _dot(a, b):
    return jnp.dot(a, b, preferred_element_type=F32)


def _dot_nt(a, b):
    return lax.dot_general(a, b, (((1,), (1,)), ((), ())), preferred_element_type=F32)


def _dot_tn(a, b):
    return lax.dot_general(a, b, (((0,), (0,)), ((), ())), preferred_element_type=F32)


def _split3(a):
    hi = a.astype(BF16)
    r1 = a - hi.astype(F32)
    mid = r1.astype(BF16)
    lo = (r1 - mid.astype(F32)).astype(BF16)
    return hi, mid, lo


def _dot_f32_lhs(a, b_exact):
    hi, mid, lo = _split3(a)
    return _dot(hi, b_exact) + _dot(mid, b_exact) + _dot(lo, b_exact)


def _dot_f32_rhs(a_exact, b):
    hi, mid, lo = _split3(b)
    return _dot(a_exact, hi) + _dot(a_exact, mid) + _dot(a_exact, lo)


def _mem_kv_body(mem_ref, g_ref, w_ref, kn_ref, k_ref, v_ref):
    mn = _rms(mem_ref[0], g_ref[0]).astype(BF16)
    kv = _dot(mn, w_ref[0])
    xa_w = XA_HEADS * XA_HEAD_DIM
    for h in range(XA_HEADS):
        cols = slice(h * XA_HEAD_DIM, (h + 1) * XA_HEAD_DIM)
        k_ref[0, 0, :, cols] = _rms(kv[:, cols], kn_ref[0]).astype(BF16)
    v_ref[0, 0] = kv[:, xa_w:].astype(BF16)


def _mem_kv(mem, norm_mem, w_kv, k_norm):
    depth, d, _ = w_kv.shape
    b, m, _ = mem.shape
    xa_w = XA_HEADS * XA_HEAD_DIM
    out = jax.ShapeDtypeStruct((depth, b, m, xa_w), BF16)
    return pl.pallas_call(
        _mem_kv_body,
        grid=(depth, b),
        in_specs=[
            pl.BlockSpec((1, m, d), lambda i, j: (j, 0, 0)),
            pl.BlockSpec((1, 1, d), lambda i, j: (i, 0, 0)),
            pl.BlockSpec((1, d, 2 * xa_w), lambda i, j: (i, 0, 0)),
            pl.BlockSpec((1, 1, XA_HEAD_DIM), lambda i, j: (i, 0, 0)),
        ],
        out_specs=[pl.BlockSpec((1, 1, m, xa_w), lambda i, j: (i, j, 0, 0))] * 2,
        out_shape=[out, out],
        compiler_params=_params("arbitrary", "arbitrary"),
        name="mem_kv",
    )(mem, norm_mem[:, None, :], w_kv.astype(BF16), k_norm[:, None, :])


def _xattn_ffn_body(x_ref, g_ref, wq_ref, qn_ref, k_ref, v_ref, wo_ref, gf_ref, wu_ref, wd_ref, o_ref,
                    *, hid_chunk):
    x = x_ref[0]
    xn = _rms(x, g_ref[...]).astype(BF16)
    q = _dot(xn, wq_ref[...])
    scale = XA_HEAD_DIM ** -0.5
    heads = []
    for h in range(XA_HEADS):
        cols = slice(h * XA_HEAD_DIM, (h + 1) * XA_HEAD_DIM)
        qh = _rms(q[:, cols], qn_ref[...]).astype(BF16)
        s = _dot_nt(qh, k_ref[0, :, cols]) * scale
        p = jnp.exp(s - jnp.max(s, axis=-1, keepdims=True))
        l = jnp.sum(p, axis=-1, keepdims=True)
        oh = _dot(p.astype(BF16), v_ref[0, :, cols]) / l
        heads.append(oh.astype(BF16))
    o = jnp.concatenate(heads, axis=-1)
    x = x + _dot(o, wo_ref[...])
    xn = _rms(x, gf_ref[...]).astype(BF16)
    acc = x
    for c in range(0, wu_ref.shape[1], hid_chunk):
        h = _dot(xn, wu_ref[:, c:c + hid_chunk])
        h = jnp.square(jnp.maximum(h, 0.0)).astype(BF16)
        acc = acc + _dot(h, wd_ref[c:c + hid_chunk, :])
    o_ref[0] = acc


def _xattn_ffn(x, gain_xa, w_q, q_norm, k, v, w_out, gain_ffn, w_up, w_down, tm=512, hid_chunk=1024):
    b, s, d = x.shape
    m = k.shape[1]
    xa_w = XA_HEADS * XA_HEAD_DIM
    hid = w_up.shape[1]
    tm = _row_tile(s, tm)
    return pl.pallas_call(
        functools.partial(_xattn_ffn_body, hid_chunk=hid_chunk),
        grid=(b, s // tm),
        in_specs=[
            pl.BlockSpec((1, tm, d), lambda i, j: (i, j, 0)),
            _const_spec((1, d)),
            _const_spec((d, xa_w)),
            _const_spec((1, XA_HEAD_DIM)),
            pl.BlockSpec((1, m, xa_w), lambda i, j: (i, 0, 0)),
            pl.BlockSpec((1, m, xa_w), lambda i, j: (i, 0, 0)),
            _const_spec((xa_w, d)),
            _const_spec((1, d)),
            _const_spec((d, hid)),
            _const_spec((hid, d)),
        ],
        out_specs=pl.BlockSpec((1, tm, d), lambda i, j: (i, j, 0)),
        out_shape=jax.ShapeDtypeStruct(x.shape, F32),
        compiler_params=_params("arbitrary", "arbitrary"),
        name="xattn_ffn",
    )(x, gain_xa[None, :], w_q.astype(BF16), q_norm[None, :], k, v, w_out.astype(BF16),
      gain_ffn[None, :], w_up.astype(BF16), w_down.astype(BF16))


def _proj_res_body(x_ref, y_ref, w_ref, o_ref):
    o_ref[...] = x_ref[...] + _dot(y_ref[...], w_ref[...])


def _proj_res(x2, y2, w, tm=512):
    t, d = x2.shape
    k = y2.shape[1]
    tm = _row_tile(t, tm)
    return pl.pallas_call(
        _proj_res_body,
        grid=(t // tm,),
        in_specs=[
            pl.BlockSpec((tm, d), lambda i: (i, 0)),
            pl.BlockSpec((tm, k), lambda i: (i, 0)),
            _const_spec((k, d)),
        ],
        out_specs=pl.BlockSpec((tm, d), lambda i: (i, 0)),
        out_shape=jax.ShapeDtypeStruct(x2.shape, F32),
        compiler_params=_params("arbitrary"),
        name="proj_res",
    )(x2, y2, w.astype(BF16))


def _gmlp_body(x_ref, g_ref, win_ref, vn_ref, ws_ref, bs_ref, wout_ref, o_ref,
               xn_s, u_s, v_s, h_s, *, col_chunk):
    tm = x_ref.shape[0]
    hid = u_s.shape[1]
    gw = hid // GM_GROUPS
    x = x_ref[...]
    xn_s[...] = _rms(x, g_ref[...]).astype(BF16)
    for c in range(0, hid, col_chunk):
        u_s[:, c:c + col_chunk] = jax.nn.gelu(_dot(xn_s[...], win_ref[:, c:c + col_chunk]))
    ssq = jnp.zeros((tm, 1), F32)
    for c in range(0, hid, col_chunk):
        vc = jax.nn.gelu(_dot(xn_s[...], win_ref[:, hid + c:hid + c + col_chunk]))
        v_s[:, c:c + col_chunk] = vc
        ssq = ssq + jnp.sum(vc * vc, axis=-1, keepdims=True)
    inv = lax.rsqrt(ssq / hid + EPS)
    t_i = lax.broadcasted_iota(I32, (GM_BLOCK, GM_BLOCK), 0)
    s_i = lax.broadcasted_iota(I32, (GM_BLOCK, GM_BLOCK), 1)
    causal = (s_i // CHUNK) <= (t_i // CHUNK)
    for g in range(GM_GROUPS):
        cols = slice(g * gw, (g + 1) * gw)
        wsg = jnp.where(causal, ws_ref[g], 0.0).astype(BF16)
        bias = bs_ref[:, g:g + 1]
        for n in range(tm // GM_BLOCK):
            rows = slice(n * GM_BLOCK, (n + 1) * GM_BLOCK)
            vg = (v_s[rows, cols] * inv[rows] * vn_ref[:, cols]).astype(BF16)
            sg = _dot(wsg, vg) + bias
            h_s[rows, cols] = (u_s[rows, cols] * sg).astype(BF16)
    o_ref[...] = x + _dot(h_s[...], wout_ref[...])


def _gmlp(x2, gain, w_in, v_norm, w_s, b_s, w_out, tm=512, col_chunk=1024):
    t, d = x2.shape
    hid = w_out.shape[0]
    tm = _row_tile(t, tm)
    assert tm % GM_BLOCK == 0
    return pl.pallas_call(
        functools.partial(_gmlp_body, col_chunk=col_chunk),
        grid=(t // tm,),
        in_specs=[
            pl.BlockSpec((tm, d), lambda i: (i, 0)),
            _const_spec((1, d)),
            _const_spec((d, 2 * hid)),
            _const_spec((1, hid)),
            _const_spec((GM_GROUPS, GM_BLOCK, GM_BLOCK)),
            _const_spec((GM_BLOCK, GM_GROUPS)),
            _const_spec((hid, d)),
        ],
        out_specs=pl.BlockSpec((tm, d), lambda i: (i, 0)),
        out_shape=jax.ShapeDtypeStruct(x2.shape, F32),
        scratch_shapes=[
            pltpu.VMEM((tm, d), BF16),
            pltpu.VMEM((tm, hid), F32),
            pltpu.VMEM((tm, hid), F32),
            pltpu.VMEM((tm, hid), BF16),
        ],
        compiler_params=_params("arbitrary"),
        name="gmlp",
    )(x2, gain[None, :], w_in.astype(BF16), v_norm[None, :], w_s, b_s.T, w_out.astype(BF16))


def _ssd_in_body(x_ref, g_ref, wz_ref, wx_ref, wdt_ref, z_ref, xbc_ref, dt_ref, *, col_chunk):
    xn = _rms(x_ref[...], g_ref[...]).astype(BF16)
    for c in range(0, wz_ref.shape[1], col_chunk):
        z_ref[:, c:c + col_chunk] = _dot(xn, wz_ref[:, c:c + col_chunk]).astype(BF16)
    for c in range(0, wx_ref.shape[1], col_chunk):
        xbc_ref[:, c:c + col_chunk] = _dot(xn, wx_ref[:, c:c + col_chunk]).astype(BF16)
    dt_ref[...] = _dot(xn, wdt_ref[...])


def _ssd_in(x2, gain, w_z, w_xbc, w_dt, tm=512, col_chunk=1024):
    t, d = x2.shape
    tm = _row_tile(t, tm)
    nz, nx, ndt = w_z.shape[1], w_xbc.shape[1], w_dt.shape[1]
    return pl.pallas_call(
        functools.partial(_ssd_in_body, col_chunk=col_chunk),
        grid=(t // tm,),
        in_specs=[
            pl.BlockSpec((tm, d), lambda i: (i, 0)),
            _const_spec((1, d)),
            _const_spec((d, nz)),
            _const_spec((d, nx)),
            _const_spec((d, ndt)),
        ],
        out_specs=[
            pl.BlockSpec((tm, nz), lambda i: (i, 0)),
            pl.BlockSpec((tm, nx), lambda i: (i, 0)),
            pl.BlockSpec((tm, ndt), lambda i: (i, 0)),
        ],
        out_shape=[
            jax.ShapeDtypeStruct((t, nz), BF16),
            jax.ShapeDtypeStruct((t, nx), BF16),
            jax.ShapeDtypeStruct((t, ndt), F32),
        ],
        compiler_params=_params("arbitrary"),
        name="ssd_in",
    )(x2, gain[None, :], w_z, w_xbc, w_dt)


def _ssd_core_body(z_ref, xbc_ref, dt_ref, cw_ref, cb_ref, dtb_ref, alog_ref, dskip_ref, onorm_ref,
                   tri_ref, expand_ref, shift_ref, y_ref, state_s, tail_s, *, conv_chunk):
    q = SSD_CHUNK
    n_state = SSD_STATE
    d_inner = z_ref.shape[2]
    conv_ch = xbc_ref.shape[2]
    heads_per_group = d_inner // SSD_HEAD_DIM // SSD_GROUPS
    gw = heads_per_group * SSD_HEAD_DIM
    tail = tail_s.shape[0]

    @pl.when(pl.program_id(1) == 0)
    def _():
        state_s[...] = jnp.zeros_like(state_s)
        tail_s[...] = jnp.zeros_like(tail_s)

    xb = xbc_ref[0]
    x_ext = jnp.concatenate([tail_s[...], xb], axis=0)
    tail_s[...] = xb[q - tail:, :]
    pieces = []
    for c0 in range(0, conv_ch, conv_chunk):
        cols = slice(c0, c0 + conv_chunk)
        conv = cb_ref[:, cols] + cw_ref[SSD_CONV - 1:SSD_CONV, cols] * xb[:, cols].astype(F32)
        for k in range(SSD_CONV - 1):
            conv = conv + cw_ref[k:k + 1, cols] * _dot(shift_ref[k], x_ext[:, cols])
        pieces.append(conv * jax.nn.sigmoid(conv))
    xbc = jnp.concatenate(pieces, axis=1)
    xs = xbc[:, :d_inner]
    bm = xbc[:, d_inner:d_inner + SSD_GROUPS * n_state].astype(BF16)
    cm = xbc[:, d_inner + SSD_GROUPS * n_state:].astype(BF16)

    dt_raw = dt_ref[0] + dtb_ref[...]
    dt = jnp.maximum(dt_raw, 0.0) + jnp.log1p(jnp.exp(-jnp.abs(dt_raw)))
    a = dt * (-jnp.exp(alog_ref[...]))
    a_cum = _dot_f32_rhs(tri_ref[...], a)
    a_cum_t = a_cum.T
    expand = expand_ref[...]
    dt_e = _dot_f32_lhs(dt, expand)
    acum_e = _dot_f32_lhs(a_cum, expand)
    alast_e = acum_e[q - 1:q, :]
    xdt = xs * dt_e
    xw = (xdt * jnp.exp(alast_e - acum_e)).astype(BF16)
    xdt_b = xdt.astype(BF16)
    ea_e = jnp.exp(acum_e)
    chunk_decay_e = jnp.exp(alast_e)

    l_i = lax.broadcasted_iota(I32, (q, q), 0)
    s_i = lax.broadcasted_iota(I32, (q, q), 1)
    causal = l_i >= s_i
    lane_g = lax.broadcasted_iota(I32, (q, gw), 1) // SSD_HEAD_DIM

    zf = z_ref[0].astype(F32)
    for g in range(SSD_GROUPS):
        gcols = slice(g * gw, (g + 1) * gw)
        ncols = slice(g * n_state, (g + 1) * n_state)
        cg = cm[:, ncols]
        bg = bm[:, ncols]
        cb = _dot_nt(cg, bg)
        st = state_s[g]
        y_off = _dot(cg, st.astype(BF16)) * ea_e[:, gcols]
        ms = []
        xblk = []
        xg = xdt_b[:, gcols]
        for r in range(heads_per_group):
            h = g * heads_per_group + r
            seg = a_cum[:, h:h + 1] - a_cum_t[h:h + 1, :]
            decay = jnp.exp(jnp.where(causal, seg, -jnp.inf))
            ms.append((decay * cb).astype(BF16))
            xblk.append(jnp.where(lane_g == r, xg, jnp.zeros_like(xg)))
        y_diag = _dot(jnp.concatenate(ms, axis=1), jnp.concatenate(xblk, axis=0))
        y = y_diag + y_off + xs[:, gcols] * dskip_ref[:, gcols]
        state_s[g] = st * chunk_decay_e[:, gcols] + _dot_tn(bg, xw[:, gcols])
        zg = zf[:, gcols]
        gated = y * (zg * jax.nn.sigmoid(zg))
        gated = gated * lax.rsqrt(jnp.mean(gated * gated, axis=-1, keepdims=True) + EPS)
        y_ref[0, :, gcols] = (gated * onorm_ref[:, gcols]).astype(BF16)


def _ssd_core(z, xbc, dt, conv_w, conv_b, dt_bias, a_log, d_skip, out_norm, conv_chunk=256):
    b, s, d_inner = z.shape
    conv_ch = xbc.shape[2]
    n_heads = d_inner // SSD_HEAD_DIM
    gw = d_inner // SSD_GROUPS
    q = SSD_CHUNK
    tail = 16
    assert s % q == 0 and n_heads <= LANES and dt.shape[2] == LANES and SSD_CONV - 1 <= tail

    def pad_heads(v):
        return jnp.pad(v, (0, LANES - n_heads))[None, :]

    tri = (jnp.arange(q)[:, None] >= jnp.arange(q)[None, :]).astype(BF16)
    expand = (jnp.arange(LANES)[:, None] == (jnp.arange(d_inner) // SSD_HEAD_DIM)[None, :]).astype(BF16)
    src = tail + jnp.arange(q)[None, :, None] - (SSD_CONV - 1 - jnp.arange(SSD_CONV - 1))[:, None, None]
    shift = (jnp.arange(tail + q)[None, None, :] == src).astype(BF16)
    return pl.pallas_call(
        functools.partial(_ssd_core_body, conv_chunk=conv_chunk),
        grid=(b, s // q),
        in_specs=[
            pl.BlockSpec((1, q, d_inner), lambda i, j: (i, j, 0)),
            pl.BlockSpec((1, q, conv_ch), lambda i, j: (i, j, 0)),
            pl.BlockSpec((1, q, LANES), lambda i, j: (i, j, 0)),
            _const_spec((SSD_CONV, conv_ch)),
            _const_spec((1, conv_ch)),
            _const_spec((1, LANES)),
            _const_spec((1, LANES)),
            _const_spec((1, d_inner)),
            _const_spec((1, d_inner)),
            _const_spec((q, q)),
            _const_spec((LANES, d_inner)),
            _const_spec((SSD_CONV - 1, q, tail + q)),
        ],
        out_specs=pl.BlockSpec((1, q, d_inner), lambda i, j: (i, j, 0)),
        out_shape=jax.ShapeDtypeStruct((b, s, d_inner), BF16),
        scratch_shapes=[
            pltpu.VMEM((SSD_GROUPS, SSD_STATE, gw), F32),
            pltpu.VMEM((tail, conv_ch), BF16),
        ],
        compiler_params=_params("arbitrary", "arbitrary"),
        name="ssd_core",
    )(z, xbc, dt, conv_w, conv_b[None, :], pad_heads(dt_bias), pad_heads(a_log),
      jnp.repeat(d_skip, SSD_HEAD_DIM)[None, :], out_norm[None, :], tri, expand, shift)


def _ssd_mixer(x, gain, w_in, conv_w, conv_b, dt_bias, a_log, d_skip, out_norm, w_out):
    b, s, d = x.shape
    d_inner = w_out.shape[0]
    conv_ch = conv_w.shape[1]
    n_heads = d_inner // SSD_HEAD_DIM
    w_in = w_in.astype(BF16)
    w_z = w_in[:, :d_inner]
    w_xbc = w_in[:, d_inner:d_inner + conv_ch]
    w_dt = jnp.pad(w_in[:, d_inner + conv_ch:], ((0, 0), (0, LANES - n_heads)))
    x2 = x.reshape(b * s, d)
    z, xbc, dt = _ssd_in(x2, gain, w_z, w_xbc, w_dt)
    y = _ssd_core(z.reshape(b, s, d_inner), xbc.reshape(b, s, conv_ch), dt.reshape(b, s, LANES),
                  conv_w, conv_b, dt_bias, a_log, d_skip, out_norm)
    return _proj_res(x2, y.reshape(b * s, d_inner), w_out).reshape(b, s, d)


def _halves_rms(x, gain, lane):
    lo = lane < (LANES // 2)
    x2 = x * x
    s_lo = jnp.sum(jnp.where(lo, x2, 0.0), axis=-1, keepdims=True)
    s_hi = jnp.sum(jnp.where(lo, 0.0, x2), axis=-1, keepdims=True)
    ms = jnp.where(lo, s_lo, s_hi) * (2.0 / LANES)
    return x * lax.rsqrt(ms + EPS) * gain


def _rope(x, cos, sin_lo, sin_hi):
    half = DSA_HEAD_DIM // 8
    return x * cos + pltpu.roll(x, LANES - half, 1) * sin_lo + pltpu.roll(x, half, 1) * sin_hi


def _dsa_prep_body(x_ref, g_ref, w_ref, pos_ref, invf_ref, qn_ref, kn_ref, kin_ref,
                   qt_ref, k_ref, vt_ref, qit_ref, kiw_ref, wit_ref):
    tm = x_ref.shape[1]
    rep = DSA_HEADS // DSA_KV_HEADS
    heads_per_block = LANES // DSA_HEAD_DIM
    q_w = DSA_HEADS * DSA_HEAD_DIM
    kv_w = DSA_KV_HEADS * DSA_HEAD_DIM
    qi_w = DSA_IDX_HEADS * DSA_IDX_DIM
    half = DSA_HEAD_DIM // 8
    xn = _rms(x_ref[0], g_ref[...]).astype(BF16)
    pos = pos_ref[0].astype(F32)
    lane = lax.broadcasted_iota(I32, (tm, LANES), 1)
    j = lane % DSA_HEAD_DIM

    ang = pos * invf_ref[...]
    cos, sin = jnp.cos(ang), jnp.sin(ang)
    sin_lo, sin_hi = jnp.where(j < half, -sin, 0.0), jnp.where(j >= half, sin, 0.0)

    def proj(c0, width):
        return _dot(xn, w_ref[:, c0:c0 + width])

    o1, o2, o3, o4 = q_w, q_w + kv_w, q_w + 2 * kv_w, q_w + 2 * kv_w + qi_w
    zeros = jnp.zeros((DSA_HEAD_DIM, tm), F32)
    for c in range(q_w // LANES):
        blk = _halves_rms(proj(c * LANES, LANES), qn_ref[...], lane)
        bt = (_rope(blk, cos, sin_lo, sin_hi) * (DSA_HEAD_DIM ** -0.5)).T
        for e in range(heads_per_block):
            h = c * heads_per_block + e
            rows = bt[e * DSA_HEAD_DIM:(e + 1) * DSA_HEAD_DIM]
            first_half = (h // rep) % heads_per_block == 0
            padded = jnp.concatenate([rows, zeros] if first_half else [zeros, rows], axis=0)
            qt_ref[0, 0, h] = padded.astype(BF16)
    for c in range(0, kv_w, LANES):
        blk = _halves_rms(proj(o1 + c, LANES), kn_ref[...], lane)
        k_ref[0, :, c:c + LANES] = _rope(blk, cos, sin_lo, sin_hi).astype(BF16)
    vt = proj(o2, kv_w).T
    for g in range(DSA_KV_HEADS):
        vt_ref[0, g, 0] = vt[g * DSA_HEAD_DIM:(g + 1) * DSA_HEAD_DIM].astype(BF16)
    for c in range(qi_w // LANES):
        bt = _rope(proj(o3 + c * LANES, LANES), cos, sin_lo, sin_hi).T
        for e in range(heads_per_block):
            qit_ref[0, 0, c * heads_per_block + e] = bt[e * DSA_IDX_DIM:(e + 1) * DSA_IDX_DIM].astype(BF16)
    blk = proj(o4, LANES)
    is_ki = lane < DSA_IDX_DIM
    ms = jnp.sum(jnp.where(is_ki, blk * blk, 0.0), axis=-1, keepdims=True) * (1.0 / DSA_IDX_DIM)
    kin = blk * lax.rsqrt(ms + EPS) * kin_ref[...]
    kiw_ref[0] = _rope(kin, jnp.where(is_ki, cos, 1.0), jnp.where(is_ki, sin_lo, 0.0),
                       jnp.where(is_ki, sin_hi, 0.0)).astype(BF16)
    wt = (blk * (DSA_IDX_HEADS ** -0.5 * DSA_IDX_DIM ** -0.5)).T
    wit_ref[0, 0] = wt[DSA_IDX_DIM:DSA_IDX_DIM + DSA_IDX_HEADS]


def _dsa_prep(x, gain, w_in, positions, q_norm, k_norm, kidx_norm, tm):
    b, s, d = x.shape
    q_w = DSA_HEADS * DSA_HEAD_DIM
    kv_w = DSA_KV_HEADS * DSA_HEAD_DIM
    qi_w = DSA_IDX_HEADS * DSA_IDX_DIM
    n_in = w_in.shape[1]
    assert n_in == q_w + 2 * kv_w + qi_w + DSA_IDX_DIM + DSA_IDX_HEADS
    n_pad = q_w + 2 * kv_w + qi_w + LANES
    w = jnp.pad(w_in.astype(BF16), ((0, 0), (0, n_pad - n_in)))
    assert s % tm == 0 and DSA_IDX_DIM + DSA_IDX_HEADS <= LANES and DSA_IDX_DIM % 8 == 0
    nq = s // tm
    half = DSA_HEAD_DIM // 8
    inv_freq = ROPE_THETA ** (-jnp.arange(half, dtype=F32) / half)
    lane = jnp.arange(LANES)
    j = lane % DSA_HEAD_DIM
    invf = jnp.where(j < 2 * half, inv_freq[j % half], 0.0)[None, :]
    kin = jnp.concatenate([kidx_norm, jnp.ones((LANES - DSA_IDX_DIM,), F32)])[None, :]

    def tok(width, dtype):
        return pl.BlockSpec((1, tm, width), lambda i, t: (i, t, 0)), jax.ShapeDtypeStruct((b, s, width), dtype)

    def per_tile(shape, dtype):
        nd = len(shape)
        return (pl.BlockSpec((1, 1) + shape, lambda i, t: (i, t) + (0,) * nd),
                jax.ShapeDtypeStruct((b, nq) + shape, dtype))

    outs = [
        per_tile((DSA_HEADS, LANES, tm), BF16),
        tok(kv_w, BF16),
        (pl.BlockSpec((1, DSA_KV_HEADS, 1, DSA_HEAD_DIM, tm), lambda i, t: (i, 0, t, 0, 0)),
         jax.ShapeDtypeStruct((b, DSA_KV_HEADS, nq, DSA_HEAD_DIM, tm), BF16)),
        per_tile((DSA_IDX_HEADS, DSA_IDX_DIM, tm), BF16),
        tok(LANES, BF16),
        per_tile((DSA_IDX_HEADS, tm), F32),
    ]
    return pl.pallas_call(
        _dsa_prep_body,
        grid=(b, s // tm),
        in_specs=[
            pl.BlockSpec((1, tm, d), lambda i, t: (i, t, 0)),
            _const_spec((1, d)),
            _const_spec((d, n_pad)),
            pl.BlockSpec((1, tm, 1), lambda i, t: (i, t, 0)),
            _const_spec((1, LANES)),
            _const_spec((1, LANES)),
            _const_spec((1, LANES)),
            _const_spec((1, LANES)),
        ],
        out_specs=[o[0] for o in outs],
        out_shape=[o[1] for o in outs],
        compiler_params=_params("arbitrary", "arbitrary"),
        name="dsa_prep",
    )(x, gain[None, :], w, positions[:, :, None], invf,
      jnp.tile(q_norm, 2)[None, :], jnp.tile(k_norm, 2)[None, :], kin)


def _dsa_attn_body(bound_ref, qt_ref, k_ref, vt_ref, qit_ref, kiw_ref, wit_ref, x_ref, wo_ref, o_ref,
                   key_s, top_s, mask_s, m_s, l_s, acc_s, s_s, o_s, *, topk):
    qb = qt_ref.shape[4]
    tk = qb
    i = pl.program_id(1)
    n_tiles = i + 1
    rep = DSA_HEADS // DSA_KV_HEADS
    krow = lax.broadcasted_iota(I32, (tk, qb), 0)
    qcol = lax.broadcasted_iota(I32, (tk, qb), 1)
    diag_ok = (krow // CHUNK) <= (qcol // CHUNK)

    def score_tile(t):
        kt = kiw_ref[0, pl.ds(pl.multiple_of(t * tk, tk), tk), 0:DSA_IDX_DIM]
        score = jnp.zeros((tk, qb), F32)
        for h in range(DSA_IDX_HEADS):
            score = score + wit_ref[0, 0, h:h + 1, :] * jnp.maximum(_dot(kt, qit_ref[0, 0, h]), 0.0)
        score = jnp.where(jnp.abs(score) < F32_MIN_NORMAL, 0.0, score)
        bits = pltpu.bitcast(score, I32)
        key = bits ^ ((bits >> 31) & jnp.int32(0x7FFFFFFF))
        admissible = (t < i) | diag_ok
        key_s[t] = jnp.where(admissible, key, jnp.int32(INT_MIN))
        top = pltpu.bitcast(bits & jnp.int32(-(1 << 16)), F32)
        top_s[t] = jnp.where(admissible, top, -jnp.inf).astype(BF16)

    def score_tile_pair(pair, carry):
        score_tile(2 * pair)
        score_tile(jnp.minimum(2 * pair + 1, n_tiles - 1))
        return carry

    n_pairs = (n_tiles + 1) // 2
    lax.fori_loop(0, n_pairs, score_tile_pair, 0)

    @pl.when(n_tiles % 2 == 1)
    def _():
        key_s[n_tiles] = jnp.full((tk, qb), INT_MIN, I32)
        top_s[n_tiles] = jnp.full((tk, qb), -jnp.inf, BF16)

    def count(pred_fn):
        def body(pair, acc):
            for t in (2 * pair, 2 * pair + 1):
                hit = jnp.where(pred_fn(key_s[t]), 1, 0).astype(I32)
                acc = acc + jnp.sum(hit.reshape(tk // 8, 8, qb), axis=0)
            return acc
        acc = lax.fori_loop(0, n_pairs, body, jnp.zeros((8, qb), I32))
        return jnp.sum(acc, axis=0, keepdims=True)

    pack = 16

    def count_top(cand_b):
        def body(pair, acc):
            for t in (2 * pair, 2 * pair + 1):
                hit = jnp.where(top_s[t] >= cand_b, jnp.ones((), BF16), jnp.zeros((), BF16))
                parts = [hit[r * pack:(r + 1) * pack] for r in range(tk // pack)]
                while len(parts) > 1:
                    parts = [a + b for a, b in zip(parts[0::2], parts[1::2])]
                acc = acc + parts[0]
            return acc
        acc = lax.fori_loop(0, n_pairs, body, jnp.zeros((pack, qb), BF16))
        return jnp.sum(acc.astype(F32), axis=0, keepdims=True)

    key16_neg_inf, key16_pos_inf = -32641, 32640
    key16_min_normal = 128

    def radix_top_step(b, carry):
        thr16, n_ge = carry
        cand = thr16 + lax.shift_left(jnp.int32(1), 15 - b)
        c = jnp.clip(cand, key16_neg_inf, key16_pos_inf)
        c = jnp.where((c >= 1) & (c < key16_min_normal), key16_min_normal, c)
        c = jnp.where((c <= -2) & (c >= -key16_min_normal), -1, c)
        pattern = c ^ ((c >> 15) & jnp.int32(0x7FFF))
        cand_b = pltpu.bitcast(lax.shift_left(pattern, 16), F32).astype(BF16)
        cnt = count_top(cand_b).astype(I32)
        accept = cnt >= topk
        return jnp.where(accept, cand, thr16), jnp.where(accept, cnt, n_ge)

    thr16, n_ge = lax.fori_loop(0, 16, radix_top_step,
                                (jnp.full((1, qb), -(1 << 15), I32), jnp.full((1, qb), n_pairs * 2 * tk, I32)))
    n_ge = jnp.where(thr16 <= key16_neg_inf, 0, n_ge)

    def radix_step(b, carry):
        thr, n_ge = carry
        cand = thr + lax.shift_left(jnp.int32(1), 31 - b)
        cnt = count(lambda key: key >= cand)
        accept = cnt >= topk
        return jnp.where(accept, cand, thr), jnp.where(accept, cnt, n_ge)

    carry = lax.fori_loop(16, RADIX_CHECK_BITS[0], radix_step, (lax.shift_left(thr16, 16), n_ge))
    for lo, hi in zip(RADIX_CHECK_BITS, RADIX_CHECK_BITS[1:] + (32,)):
        carry = lax.cond(jnp.max(carry[1]) > topk,
                         functools.partial(lax.fori_loop, lo, hi, radix_step), lambda c: c, carry)
    thr, n_ge = carry
    has_cut_tie = jnp.max(n_ge) > topk

    @pl.when(jnp.logical_not(has_cut_tie))
    def _():
        def mask_tile(t, carry):
            key = key_s[t]
            mask_s[t] = jnp.where((key >= thr) & (key != jnp.int32(INT_MIN)), 0.0, NEG_BIG)
            return carry
        lax.fori_loop(0, n_tiles, mask_tile, 0)

    @pl.when(has_cut_tie)
    def _():
        n_gt = count(lambda key: key > thr)
        need = (topk - n_gt).astype(F32)
        strict_lower = (qcol < krow).astype(BF16)
        ones = jnp.ones((tk, tk), BF16)

        def mask_tile(t, before):
            key = key_s[t]
            eq = key == thr
            eqb = jnp.where(eq, 1.0, 0.0).astype(BF16)
            rank = before + _dot(strict_lower, eqb)
            sel = (key > thr) | (eq & (rank < need))
            sel = sel & (key != jnp.int32(INT_MIN))
            mask_s[t] = jnp.where(sel, 0.0, NEG_BIG)
            return before + _dot(ones, eqb)

        lax.fori_loop(0, n_tiles, mask_tile, jnp.zeros((tk, qb), F32))

    l_s[...] = jnp.zeros(l_s.shape, F32)
    acc_s[...] = jnp.zeros(acc_s.shape, F32)
    heads_per_block = LANES // DSA_HEAD_DIM

    def logits(t, h, mask):
        kblk = (h // rep) // heads_per_block
        start = t * tk if isinstance(t, int) else pl.multiple_of(t * tk, tk)
        kt = k_ref[0, pl.ds(start, tk), kblk * LANES:(kblk + 1) * LANES]
        return _dot(kt, qt_ref[0, 0, h]) + mask

    def exp_sum_pv(use_offset):
        n_slots = s_s.shape[0]
        ahead = n_slots - 1
        assert DSA_HEADS % n_slots == 0
        for h in range(ahead):
            s_s[h] = logits(0, h, mask_s[0])

        def attn_tile(t, carry):
            t_next = jnp.minimum(t + 1, n_tiles - 1)
            mask, mask_next = mask_s[t], mask_s[t_next]
            for h in range(DSA_HEADS):
                nxt = h + ahead
                if nxt < DSA_HEADS:
                    s_s[nxt % n_slots] = logits(t, nxt, mask)
                else:
                    s_s[nxt % n_slots] = logits(t_next, nxt - DSA_HEADS, mask_next)
                s = s_s[h % n_slots]
                p = jnp.exp(s - m_s[h, 0:1, :] if use_offset else s)
                l_s[h] = l_s[h] + jnp.sum(p.reshape(tk // 8, 8, qb), axis=0)
                acc_s[h] = acc_s[h] + _dot(vt_ref[0, h // rep, t], p.astype(BF16))
            return carry

        lax.fori_loop(0, n_tiles, attn_tile, 0)

    bounded = bound_ref[0] <= DSA_SAFE_LOGIT

    @pl.when(bounded)
    def _():
        exp_sum_pv(use_offset=False)

    @pl.when(jnp.logical_not(bounded))
    def _():
        m_s[...] = jnp.full(m_s.shape, NEG_BIG, F32)

        def max_tile(t, carry):
            mask = mask_s[t]
            for h in range(DSA_HEADS):
                s = logits(t, h, mask)
                m_s[h] = jnp.maximum(m_s[h], jnp.max(s.reshape(tk // 8, 8, qb), axis=0))
            return carry

        lax.fori_loop(0, n_tiles, max_tile, 0)
        for h in range(DSA_HEADS):
            m_s[h] = jnp.broadcast_to(jnp.max(m_s[h], axis=0, keepdims=True), (8, qb))
        exp_sum_pv(use_offset=True)

    for c in range(DSA_HEADS // heads_per_block):
        pair = []
        for h in range(c * heads_per_block, (c + 1) * heads_per_block):
            pair.append(acc_s[h] / jnp.sum(l_s[h], axis=0, keepdims=True))
        o_s[:, c * LANES:(c + 1) * LANES] = jnp.concatenate(pair, axis=0).T.astype(BF16)
    o_ref[0] = x_ref[0] + _dot(o_s[...], wo_ref[...])


def _dsa_attn(bound, qt, k, vt, qit, kiw, wit, x, w_out):
    b, nq, n_heads, _, qb = qt.shape
    _, s, d = x.shape
    kvh, dh = vt.shape[1], vt.shape[3]
    topk = min(DSA_TOPK_MAX, s // 4)
    n_key_tiles = nq + nq % 2
    assert (qb // 16) * n_key_tiles <= 256, "packed bf16 hit counts must stay exact"
    return pl.pallas_call(
        functools.partial(_dsa_attn_body, topk=topk),
        grid=(b, nq),
        in_specs=[
            pl.BlockSpec(memory_space=pltpu.SMEM),
            pl.BlockSpec((1, 1, n_heads, LANES, qb), lambda i, j: (i, j, 0, 0, 0)),
            pl.BlockSpec((1, s, kvh * dh), lambda i, j: (i, 0, 0)),
            pl.BlockSpec((1, kvh, nq, dh, qb), lambda i, j: (i, 0, 0, 0, 0)),
            pl.BlockSpec((1, 1, DSA_IDX_HEADS, DSA_IDX_DIM, qb), lambda i, j: (i, j, 0, 0, 0)),
            pl.BlockSpec((1, s, LANES), lambda i, j: (i, 0, 0)),
            pl.BlockSpec((1, 1, DSA_IDX_HEADS, qb), lambda i, j: (i, j, 0, 0)),
            pl.BlockSpec((1, qb, d), lambda i, j: (i, j, 0)),
            _const_spec((n_heads * dh, d)),
        ],
        out_specs=pl.BlockSpec((1, qb, d), lambda i, j: (i, j, 0)),
        out_shape=jax.ShapeDtypeStruct(x.shape, F32),
        scratch_shapes=[
            pltpu.VMEM((n_key_tiles, qb, qb), I32),
            pltpu.VMEM((n_key_tiles, qb, qb), BF16),
            pltpu.VMEM((nq, qb, qb), F32),
            pltpu.VMEM((n_heads, 8, qb), F32),
            pltpu.VMEM((n_heads, 8, qb), F32),
            pltpu.VMEM((n_heads, dh, qb), F32),
            pltpu.VMEM((8, qb, qb), F32),
            pltpu.VMEM((qb, n_heads * dh), BF16),
        ],
        compiler_params=_params("arbitrary", "arbitrary"),
        name="dsa_attn",
    )(bound, qt, k, vt, qit, kiw, wit, x, w_out.astype(BF16))


def _dsa_mixer(x, positions, gain, w_in, q_norm, k_norm, kidx_norm, w_out, qb=DSA_QTILE):
    s = x.shape[1]
    qb = _row_tile(s, qb)
    qt, k, vt, qit, kiw, wit = _dsa_prep(x, gain, w_in, positions, q_norm, k_norm, kidx_norm, qb)
    bound = (DSA_HEAD_DIM ** 0.5) * jnp.max(jnp.abs(q_norm)) * jnp.max(jnp.abs(k_norm)) * (1.0 + 2.0 ** -6)
    return _dsa_attn(bound.reshape(1), qt, k, vt, qit, kiw, wit, x, w_out)


def kernel(x, mem, positions, norm_mix, gm_w_in, gm_v_norm, gm_w_s, gm_b_s, gm_w_out, ssd_w_in, ssd_conv_w, ssd_conv_b, ssd_dt_bias, ssd_a_log, ssd_d, ssd_out_norm, ssd_w_out, dsa_w_in, dsa_q_norm, dsa_k_norm, dsa_kidx_norm, dsa_w_out, norm_xa, norm_mem, xa_w_q, xa_w_kv, xa_q_norm, xa_k_norm, xa_w_out, norm_ffn, ffn_w_up, ffn_w_down):
    b, s, d = x.shape
    depth = norm_mix.shape[0]
    mem_k, mem_v = _mem_kv(mem, norm_mem, xa_w_kv, xa_k_norm)
    for i in range(depth):
        kind, j = i % 3, i // 3
        if kind == 0:
            x = _gmlp(x.reshape(b * s, d), norm_mix[i], gm_w_in[j], gm_v_norm[j], gm_w_s[j], gm_b_s[j],
                      gm_w_out[j]).reshape(b, s, d)
        elif kind == 1:
            x = _ssd_mixer(x, norm_mix[i], ssd_w_in[j], ssd_conv_w[j], ssd_conv_b[j], ssd_dt_bias[j],
                           ssd_a_log[j], ssd_d[j], ssd_out_norm[j], ssd_w_out[j])
        else:
            x = _dsa_mixer(x, positions, norm_mix[i], dsa_w_in[j], dsa_q_norm[j], dsa_k_norm[j],
                           dsa_kidx_norm[j], dsa_w_out[j])
        x = _xattn_ffn(x, norm_xa[i], xa_w_q[i], xa_q_norm[i], mem_k[i], mem_v[i], xa_w_out[i],
                       norm_ffn[i], ffn_w_up[i], ffn_w_down[i])
    return x
```

```python
import functools

import jax
import jax.numpy as jnp
from jax import lax
from jax.experimental import pallas as pl
from jax.experimental.pallas import tpu as pltpu

F32 = jnp.float32
BF16 = jnp.bfloat16
I32 = jnp.int32

EPS = 1e-6
ROPE_THETA = 500000.0
CHUNK = 64
GM_BLOCK = 128
GM_GROUPS = 8
SSD_HEAD_DIM = 64
SSD_GROUPS = 8
SSD_STATE = 128
SSD_CONV = 4
SSD_CHUNK = 128
DSA_HEADS = 16
DSA_KV_HEADS = 4
DSA_HEAD_DIM = 64
DSA_IDX_HEADS = 8
DSA_IDX_DIM = 64
DSA_TOPK_MAX = 256
DSA_QTILE = 256
DSA_SAFE_LOGIT = 60.0
RADIX_CHECK_BITS = (25, 28, 30)
XA_HEADS = 4
XA_HEAD_DIM = 128

LANES = 128
MXU_WIDTH = 256
VMEM_LIMIT_BYTES = 56 * 1024 * 1024
NEG_BIG = -1e30
INT_MIN = -(2 ** 31)
F32_MIN_NORMAL = 2.0 ** -126


def _params(*semantics):
    return pltpu.CompilerParams(dimension_semantics=semantics, vmem_limit_bytes=VMEM_LIMIT_BYTES)


def _const_spec(shape):
    nd = len(shape)
    return pl.BlockSpec(shape, lambda *_: (0,) * nd, pipeline_mode=pl.Buffered(1))


def _row_tile(n, want):
    t = min(n, want)
    assert n % t == 0, (n, t)
    return t


def _rms(x, gain):
    ms = jnp.mean(x * x, axis=-1, keepdims=True)
    return x * lax.rsqrt(ms + EPS) * gain


def _dot(a, b):
    return jnp.dot(a, b, preferred_element_type=F32)


def _dot_nt(a, b):
    return lax.dot_general(a, b, (((1,), (1,)), ((), ())), preferred_element_type=F32)


def _dot_tn(a, b):
    return lax.dot_general(a, b, (((0,), (0,)), ((), ())), preferred_element_type=F32)


def _split3(a):
    hi = a.astype(BF16)
    r1 = a - hi.astype(F32)
    mid = r1.astype(BF16)
    lo = (r1 - mid.astype(F32)).astype(BF16)
    return hi, mid, lo


def _dot_f32_lhs(a, b_exact):
    hi, mid, lo = _split3(a)
    return _dot(hi, b_exact) + _dot(mid, b_exact) + _dot(lo, b_exact)


def _dot_f32_rhs(a_exact, b):
    hi, mid, lo = _split3(b)
    return _dot(a_exact, hi) + _dot(a_exact, mid) + _dot(a_exact, lo)


def _mem_kv_body(mem_ref, g_ref, w_ref, kn_ref, k_ref, v_ref):
    mn = _rms(mem_ref[0], g_ref[0]).astype(BF16)
    kv = _dot(mn, w_ref[0])
    xa_w = XA_HEADS * XA_HEAD_DIM
    for h in range(XA_HEADS):
        cols = slice(h * XA_HEAD_DIM, (h + 1) * XA_HEAD_DIM)
        k_ref[0, 0, :, cols] = _rms(kv[:, cols], kn_ref[0]).astype(BF16)
    v_ref[0, 0] = kv[:, xa_w:].astype(BF16)


def _mem_kv(mem, norm_mem, w_kv, k_norm):
    depth, d, _ = w_kv.shape
    b, m, _ = mem.shape
    xa_w = XA_HEADS * XA_HEAD_DIM
    out = jax.ShapeDtypeStruct((depth, b, m, xa_w), BF16)
    return pl.pallas_call(
        _mem_kv_body,
        grid=(depth, b),
        in_specs=[
            pl.BlockSpec((1, m, d), lambda i, j: (j, 0, 0)),
            pl.BlockSpec((1, 1, d), lambda i, j: (i, 0, 0)),
            pl.BlockSpec((1, d, 2 * xa_w), lambda i, j: (i, 0, 0)),
            pl.BlockSpec((1, 1, XA_HEAD_DIM), lambda i, j: (i, 0, 0)),
        ],
        out_specs=[pl.BlockSpec((1, 1, m, xa_w), lambda i, j: (i, j, 0, 0))] * 2,
        out_shape=[out, out],
        compiler_params=_params("arbitrary", "arbitrary"),
        name="mem_kv",
    )(mem, norm_mem[:, None, :], w_kv.astype(BF16), k_norm[:, None, :])


def _xattn_ffn_body(x_ref, g_ref, wq_ref, qn_ref, k_ref, v_ref, wo_ref, gf_ref, wu_ref, wd_ref, o_ref,
                    *, hid_chunk):
    x = x_ref[0]
    xn = _rms(x, g_ref[...]).astype(BF16)
    q = _dot(xn, wq_ref[...])
    scale = XA_HEAD_DIM ** -0.5
    heads = []
    for h in range(XA_HEADS):
        cols = slice(h * XA_HEAD_DIM, (h + 1) * XA_HEAD_DIM)
        qh = _rms(q[:, cols], qn_ref[...]).astype(BF16)
        s = _dot_nt(qh, k_ref[0, :, cols]) * scale
        p = jnp.exp(s - jnp.max(s, axis=-1, keepdims=True))
        l = jnp.sum(p, axis=-1, keepdims=True)
        oh = _dot(p.astype(BF16), v_ref[0, :, cols]) / l
        heads.append(oh.astype(BF16))
    o = jnp.concatenate(heads, axis=-1)
    x = x + _dot(o, wo_ref[...])
    xn = _rms(x, gf_ref[...]).astype(BF16)
    acc = x
    for c in range(0, wu_ref.shape[1], hid_chunk):
        h = _dot(xn, wu_ref[:, c:c + hid_chunk])
        h = jnp.square(jnp.maximum(h, 0.0)).astype(BF16)
        acc = acc + _dot(h, wd_ref[c:c + hid_chunk, :])
    o_ref[0] = acc


def _xattn_ffn(x, gain_xa, w_q, q_norm, k, v, w_out, gain_ffn, w_up, w_down, tm=512, hid_chunk=1024):
    b, s, d = x.shape
    m = k.shape[1]
    xa_w = XA_HEADS * XA_HEAD_DIM
    hid = w_up.shape[1]
    tm = _row_tile(s, tm)
    return pl.pallas_call(
        functools.partial(_xattn_ffn_body, hid_chunk=hid_chunk),
        grid=(b, s // tm),
        in_specs=[
            pl.BlockSpec((1, tm, d), lambda i, j: (i, j, 0)),
            _const_spec((1, d)),
            _const_spec((d, xa_w)),
            _const_spec((1, XA_HEAD_DIM)),
            pl.BlockSpec((1, m, xa_w), lambda i, j: (i, 0, 0)),
            pl.BlockSpec((1, m, xa_w), lambda i, j: (i, 0, 0)),
            _const_spec((xa_w, d)),
            _const_spec((1, d)),
            _const_spec((d, hid)),
            _const_spec((hid, d)),
        ],
        out_specs=pl.BlockSpec((1, tm, d), lambda i, j: (i, j, 0)),
        out_shape=jax.ShapeDtypeStruct(x.shape, F32),
        compiler_params=_params("arbitrary", "arbitrary"),
        name="xattn_ffn",
    )(x, gain_xa[None, :], w_q.astype(BF16), q_norm[None, :], k, v, w_out.astype(BF16),
      gain_ffn[None, :], w_up.astype(BF16), w_down.astype(BF16))


def _proj_res_body(x_ref, y_ref, w_ref, o_ref):
    o_ref[...] = x_ref[...] + _dot(y_ref[...], w_ref[...])


def _proj_res(x2, y2, w, tm=512):
    t, d = x2.shape
    k = y2.shape[1]
    tm = _row_tile(t, tm)
    return pl.pallas_call(
        _proj_res_body,
        grid=(t // tm,),
        in_specs=[
            pl.BlockSpec((tm, d), lambda i: (i, 0)),
            pl.BlockSpec((tm, k), lambda i: (i, 0)),
            _const_spec((k, d)),
        ],
        out_specs=pl.BlockSpec((tm, d), lambda i: (i, 0)),
        out_shape=jax.ShapeDtypeStruct(x2.shape, F32),
        compiler_params=_params("arbitrary"),
        name="proj_res",
    )(x2, y2, w.astype(BF16))


def _gmlp_body(x_ref, g_ref, win_ref, vn_ref, ws_ref, bs_ref, wout_ref, o_ref,
               xn_s, u_s, v_s, h_s, *, col_chunk):
    tm = x_ref.shape[0]
    hid = u_s.shape[1]
    gw = hid // GM_GROUPS
    x = x_ref[...]
    xn_s[...] = _rms(x, g_ref[...]).astype(BF16)
    for c in range(0, hid, col_chunk):
        u_s[:, c:c + col_chunk] = jax.nn.gelu(_dot(xn_s[...], win_ref[:, c:c + col_chunk]))
    ssq = jnp.zeros((tm, 1), F32)
    for c in range(0, hid, col_chunk):
        vc = jax.nn.gelu(_dot(xn_s[...], win_ref[:, hid + c:hid + c + col_chunk]))
        v_s[:, c:c + col_chunk] = vc
        ssq = ssq + jnp.sum(vc * vc, axis=-1, keepdims=True)
    inv = lax.rsqrt(ssq / hid + EPS)
    t_i = lax.broadcasted_iota(I32, (GM_BLOCK, GM_BLOCK), 0)
    s_i = lax.broadcasted_iota(I32, (GM_BLOCK, GM_BLOCK), 1)
    causal = (s_i // CHUNK) <= (t_i // CHUNK)
    for g in range(GM_GROUPS):
        cols = slice(g * gw, (g + 1) * gw)
        wsg = jnp.where(causal, ws_ref[g], 0.0).astype(BF16)
        bias = bs_ref[:, g:g + 1]
        for n in range(tm // GM_BLOCK):
            rows = slice(n * GM_BLOCK, (n + 1) * GM_BLOCK)
            vg = (v_s[rows, cols] * inv[rows] * vn_ref[:, cols]).astype(BF16)
            sg = _dot(wsg, vg) + bias
            h_s[rows, cols] = (u_s[rows, cols] * sg).astype(BF16)
    o_ref[...] = x + _dot(h_s[...], wout_ref[...])


def _gmlp(x2, gain, w_in, v_norm, w_s, b_s, w_out, tm=512, col_chunk=1024):
    t, d = x2.shape
    hid = w_out.shape[0]
    tm = _row_tile(t, tm)
    assert tm % GM_BLOCK == 0
    return pl.pallas_call(
        functools.partial(_gmlp_body, col_chunk=col_chunk),
        grid=(t // tm,),
        in_specs=[
            pl.BlockSpec((tm, d), lambda i: (i, 0)),
            _const_spec((1, d)),
            _const_spec((d, 2 * hid)),
            _const_spec((1, hid)),
            _const_spec((GM_GROUPS, GM_BLOCK, GM_BLOCK)),
            _const_spec((GM_BLOCK, GM_GROUPS)),
            _const_spec((hid, d)),
        ],
        out_specs=pl.BlockSpec((tm, d), lambda i: (i, 0)),
        out_shape=jax.ShapeDtypeStruct(x2.shape, F32),
        scratch_shapes=[
            pltpu.VMEM((tm, d), BF16),
            pltpu.VMEM((tm, hid), F32),
            pltpu.VMEM((tm, hid), F32),
            pltpu.VMEM((tm, hid), BF16),
        ],
        compiler_params=_params("arbitrary"),
        name="gmlp",
    )(x2, gain[None, :], w_in.astype(BF16), v_norm[None, :], w_s, b_s.T, w_out.astype(BF16))


def _ssd_in_body(x_ref, g_ref, wz_ref, wx_ref, wdt_ref, z_ref, xbc_ref, dt_ref, *, col_chunk):
    xn = _rms(x_ref[...], g_ref[...]).astype(BF16)
    for c in range(0, wz_ref.shape[1], col_chunk):
        z_ref[:, c:c + col_chunk] = _dot(xn, wz_ref[:, c:c + col_chunk]).astype(BF16)
    for c in range(0, wx_ref.shape[1], col_chunk):
        xbc_ref[:, c:c + col_chunk] = _dot(xn, wx_ref[:, c:c + col_chunk]).astype(BF16)
    dt_ref[...] = _dot(xn, wdt_ref[...])


def _ssd_in(x2, gain, w_z, w_xbc, w_dt, tm=512, col_chunk=1024):
    t, d = x2.shape
    tm = _row_tile(t, tm)
    nz, nx, ndt = w_z.shape[1], w_xbc.shape[1], w_dt.shape[1]
    return pl.pallas_call(
        functools.partial(_ssd_in_body, col_chunk=col_chunk),
        grid=(t // tm,),
        in_specs=[
            pl.BlockSpec((tm, d), lambda i: (i, 0)),
            _const_spec((1, d)),
            _const_spec((d, nz)),
            _const_spec((d, nx)),
            _const_spec((d, ndt)),
        ],
        out_specs=[
            pl.BlockSpec((tm, nz), lambda i: (i, 0)),
            pl.BlockSpec((tm, nx), lambda i: (i, 0)),
            pl.BlockSpec((tm, ndt), lambda i: (i, 0)),
        ],
        out_shape=[
            jax.ShapeDtypeStruct((t, nz), BF16),
            jax.ShapeDtypeStruct((t, nx), BF16),
            jax.ShapeDtypeStruct((t, ndt), F32),
        ],
        compiler_params=_params("arbitrary"),
        name="ssd_in",
    )(x2, gain[None, :], w_z, w_xbc, w_dt)


def _ssd_core_body(z_ref, xbc_ref, dt_ref, cw_ref, cb_ref, dtb_ref, alog_ref, dskip_ref, onorm_ref,
                   tri_ref, expand_ref, shift_ref, y_ref, state_s, tail_s, *, conv_chunk):
    q = SSD_CHUNK
    n_state = SSD_STATE
    d_inner = z_ref.shape[2]
    conv_ch = xbc_ref.shape[2]
    heads_per_group = d_inner // SSD_HEAD_DIM // SSD_GROUPS
    gw = heads_per_group * SSD_HEAD_DIM
    tail = tail_s.shape[0]

    @pl.when(pl.program_id(1) == 0)
    def _():
        state_s[...] = jnp.zeros_like(state_s)
        tail_s[...] = jnp.zeros_like(tail_s)

    xb = xbc_ref[0]
    x_ext = jnp.concatenate([tail_s[...], xb], axis=0)
    tail_s[...] = xb[q - tail:, :]
    pieces = []
    for c0 in range(0, conv_ch, conv_chunk):
        cols = slice(c0, c0 + conv_chunk)
        conv = cb_ref[:, cols] + cw_ref[SSD_CONV - 1:SSD_CONV, cols] * xb[:, cols].astype(F32)
        for k in range(SSD_CONV - 1):
            conv = conv + cw_ref[k:k + 1, cols] * _dot(shift_ref[k], x_ext[:, cols])
        pieces.append(conv * jax.nn.sigmoid(conv))
    xbc = jnp.concatenate(pieces, axis=1)
    xs = xbc[:, :d_inner]
    bm = xbc[:, d_inner:d_inner + SSD_GROUPS * n_state].astype(BF16)
    cm = xbc[:, d_inner + SSD_GROUPS * n_state:].astype(BF16)

    dt_raw = dt_ref[0] + dtb_ref[...]
    dt = jnp.maximum(dt_raw, 0.0) + jnp.log1p(jnp.exp(-jnp.abs(dt_raw)))
    a = dt * (-jnp.exp(alog_ref[...]))
    a_cum = _dot_f32_rhs(tri_ref[...], a)
    a_cum_t = a_cum.T
    expand = expand_ref[...]
    dt_e = _dot_f32_lhs(dt, expand)
    acum_e = _dot_f32_lhs(a_cum, expand)
    alast_e = acum_e[q - 1:q, :]
    xdt = xs * dt_e
    xw = (xdt * jnp.exp(alast_e - acum_e)).astype(BF16)
    xdt_b = xdt.astype(BF16)
    ea_e = jnp.exp(acum_e)
    chunk_decay_e = jnp.exp(alast_e)

    l_i = lax.broadcasted_iota(I32, (q, q), 0)
    s_i = lax.broadcasted_iota(I32, (q, q), 1)
    causal = l_i >= s_i
    lane_g = lax.broadcasted_iota(I32, (q, gw), 1) // SSD_HEAD_DIM

    zf = z_ref[0].astype(F32)
    for g in range(SSD_GROUPS):
        gcols = slice(g * gw, (g + 1) * gw)
        ncols = slice(g * n_state, (g + 1) * n_state)
        cg = cm[:, ncols]
        bg = bm[:, ncols]
        cb = _dot_nt(cg, bg)
        st = state_s[g]
        y_off = _dot(cg, st.astype(BF16)) * ea_e[:, gcols]
        ms = []
        xblk = []
        xg = xdt_b[:, gcols]
        for r in range(heads_per_group):
            h = g * heads_per_group + r
            seg = a_cum[:, h:h + 1] - a_cum_t[h:h + 1, :]
            decay = jnp.exp(jnp.where(causal, seg, -jnp.inf))
            ms.append((decay * cb).astype(BF16))
            xblk.append(jnp.where(lane_g == r, xg, jnp.zeros_like(xg)))
        y_diag = _dot(jnp.concatenate(ms, axis=1), jnp.concatenate(xblk, axis=0))
        y = y_diag + y_off + xs[:, gcols] * dskip_ref[:, gcols]
        state_s[g] = st * chunk_decay_e[:, gcols] + _dot_tn(bg, xw[:, gcols])
        zg = zf[:, gcols]
        gated = y * (zg * jax.nn.sigmoid(zg))
        gated = gated * lax.rsqrt(jnp.mean(gated * gated, axis=-1, keepdims=True) + EPS)
        y_ref[0, :, gcols] = (gated * onorm_ref[:, gcols]).astype(BF16)


def _ssd_core(z, xbc, dt, conv_w, conv_b, dt_bias, a_log, d_skip, out_norm, conv_chunk=256):
    b, s, d_inner = z.shape
    conv_ch = xbc.shape[2]
    n_heads = d_inner // SSD_HEAD_DIM
    gw = d_inner // SSD_GROUPS
    q = SSD_CHUNK
    tail = 16
    assert s % q == 0 and n_heads <= LANES and dt.shape[2] == LANES and SSD_CONV - 1 <= tail

    def pad_heads(v):
        return jnp.pad(v, (0, LANES - n_heads))[None, :]

    tri = (jnp.arange(q)[:, None] >= jnp.arange(q)[None, :]).astype(BF16)
    expand = (jnp.arange(LANES)[:, None] == (jnp.arange(d_inner) // SSD_HEAD_DIM)[None, :]).astype(BF16)
    src = tail + jnp.arange(q)[None, :, None] - (SSD_CONV - 1 - jnp.arange(SSD_CONV - 1))[:, None, None]
    shift = (jnp.arange(tail + q)[None, None, :] == src).astype(BF16)
    return pl.pallas_call(
        functools.partial(_ssd_core_body, conv_chunk=conv_chunk),
        grid=(b, s // q),
        in_specs=[
            pl.BlockSpec((1, q, d_inner), lambda i, j: (i, j, 0)),
            pl.BlockSpec((1, q, conv_ch), lambda i, j: (i, j, 0)),
            pl.BlockSpec((1, q, LANES), lambda i, j: (i, j, 0)),
            _const_spec((SSD_CONV, conv_ch)),
            _const_spec((1, conv_ch)),
            _const_spec((1, LANES)),
            _const_spec((1, LANES)),
            _const_spec((1, d_inner)),
            _const_spec((1, d_inner)),
            _const_spec((q, q)),
            _const_spec((LANES, d_inner)),
            _const_spec((SSD_CONV - 1, q, tail + q)),
        ],
        out_specs=pl.BlockSpec((1, q, d_inner), lambda i, j: (i, j, 0)),
        out_shape=jax.ShapeDtypeStruct((b, s, d_inner), BF16),
        scratch_shapes=[
            pltpu.VMEM((SSD_GROUPS, SSD_STATE, gw), F32),
            pltpu.VMEM((tail, conv_ch), BF16),
        ],
        compiler_params=_params("arbitrary", "arbitrary"),
        name="ssd_core",
    )(z, xbc, dt, conv_w, conv_b[None, :], pad_heads(dt_bias), pad_heads(a_log),
      jnp.repeat(d_skip, SSD_HEAD_DIM)[None, :], out_norm[None, :], tri, expand, shift)


def _ssd_mixer(x, gain, w_in, conv_w, conv_b, dt_bias, a_log, d_skip, out_norm, w_out):
    b, s, d = x.shape
    d_inner = w_out.shape[0]
    conv_ch = conv_w.shape[1]
    n_heads = d_inner // SSD_HEAD_DIM
    w_in = w_in.astype(BF16)
    w_z = w_in[:, :d_inner]
    w_xbc = w_in[:, d_inner:d_inner + conv_ch]
    w_dt = jnp.pad(w_in[:, d_inner + conv_ch:], ((0, 0), (0, LANES - n_heads)))
    x2 = x.reshape(b * s, d)
    z, xbc, dt = _ssd_in(x2, gain, w_z, w_xbc, w_dt)
    y = _ssd_core(z.reshape(b, s, d_inner), xbc.reshape(b, s, conv_ch), dt.reshape(b, s, LANES),
                  conv_w, conv_b, dt_bias, a_log, d_skip, out_norm)
    return _proj_res(x2, y.reshape(b * s, d_inner), w_out).reshape(b, s, d)


def _halves_rms(x, gain, lane):
    lo = lane < (LANES // 2)
    x2 = x * x
    s_lo = jnp.sum(jnp.where(lo, x2, 0.0), axis=-1, keepdims=True)
    s_hi = jnp.sum(jnp.where(lo, 0.0, x2), axis=-1, keepdims=True)
    ms = jnp.where(lo, s_lo, s_hi) * (2.0 / LANES)
    return x * lax.rsqrt(ms + EPS) * gain


def _rope(x, cos, sin_lo, sin_hi):
    half = DSA_HEAD_DIM // 8
    return x * cos + pltpu.roll(x, LANES - half, 1) * sin_lo + pltpu.roll(x, half, 1) * sin_hi


def _dsa_prep_body(x_ref, g_ref, w_ref, pos_ref, invf_ref, qn_ref, kn_ref, kin_ref,
                   qt_ref, k_ref, vt_ref, qit_ref, kiw_ref, wit_ref):
    tm = x_ref.shape[1]
    rep = DSA_HEADS // DSA_KV_HEADS
    heads_per_block = LANES // DSA_HEAD_DIM
    q_w = DSA_HEADS * DSA_HEAD_DIM
    kv_w = DSA_KV_HEADS * DSA_HEAD_DIM
    qi_w = DSA_IDX_HEADS * DSA_IDX_DIM
    half = DSA_HEAD_DIM // 8
    xn = _rms(x_ref[0], g_ref[...]).astype(BF16)
    pos = pos_ref[0].astype(F32)
    lane = lax.broadcasted_iota(I32, (tm, LANES), 1)
    j = lane % DSA_HEAD_DIM

    ang = pos * invf_ref[...]
    cos, sin = jnp.cos(ang), jnp.sin(ang)
    sin_lo, sin_hi = jnp.where(j < half, -sin, 0.0), jnp.where(j >= half, sin, 0.0)

    def proj(c0, width):
        return _dot(xn, w_ref[:, c0:c0 + width])

    def proj_blocks(c0, width):
        blocks = []
        for c in range(c0, c0 + width, MXU_WIDTH):
            wide = proj(c, min(MXU_WIDTH, c0 + width - c))
            blocks += [wide[:, b:b + LANES] for b in range(0, wide.shape[1], LANES)]
        return blocks

    o1, o2, o3, o4 = q_w, q_w + kv_w, q_w + 2 * kv_w, q_w + 2 * kv_w + qi_w
    zeros = jnp.zeros((DSA_HEAD_DIM, tm), F32)
    for c, raw in enumerate(proj_blocks(0, q_w)):
        blk = _halves_rms(raw, qn_ref[...], lane)
        bt = (_rope(blk, cos, sin_lo, sin_hi) * (DSA_HEAD_DIM ** -0.5)).T
        for e in range(heads_per_block):
            h = c * heads_per_block + e
            rows = bt[e * DSA_HEAD_DIM:(e + 1) * DSA_HEAD_DIM]
            first_half = (h // rep) % heads_per_block == 0
            padded = jnp.concatenate([rows, zeros] if first_half else [zeros, rows], axis=0)
            qt_ref[0, 0, h] = padded.astype(BF16)
    for c, raw in enumerate(proj_blocks(o1, kv_w)):
        blk = _halves_rms(raw, kn_ref[...], lane)
        k_ref[0, :, c * LANES:(c + 1) * LANES] = _rope(blk, cos, sin_lo, sin_hi).astype(BF16)
    vt = proj(o2, kv_w).T
    for g in range(DSA_KV_HEADS):
        vt_ref[0, g, 0] = vt[g * DSA_HEAD_DIM:(g + 1) * DSA_HEAD_DIM].astype(BF16)
    for c, raw in enumerate(proj_blocks(o3, qi_w)):
        bt = _rope(raw, cos, sin_lo, sin_hi).T
        for e in range(heads_per_block):
            qit_ref[0, 0, c * heads_per_block + e] = bt[e * DSA_IDX_DIM:(e + 1) * DSA_IDX_DIM].astype(BF16)
    blk = proj(o4, LANES)
    is_ki = lane < DSA_IDX_DIM
    ms = jnp.sum(jnp.where(is_ki, blk * blk, 0.0), axis=-1, keepdims=True) * (1.0 / DSA_IDX_DIM)
    kin = blk * lax.rsqrt(ms + EPS) * kin_ref[...]
    kiw_ref[0] = _rope(kin, jnp.where(is_ki, cos, 1.0), jnp.where(is_ki, sin_lo, 0.0),
                       jnp.where(is_ki, sin_hi, 0.0)).astype(BF16)
    wt = (blk * (DSA_IDX_HEADS ** -0.5 * DSA_IDX_DIM ** -0.5)).T
    wit_ref[0, 0] = wt[DSA_IDX_DIM:DSA_IDX_DIM + DSA_IDX_HEADS]


def _dsa_prep(x, gain, w_in, positions, q_norm, k_norm, kidx_norm, tm):
    b, s, d = x.shape
    q_w = DSA_HEADS * DSA_HEAD_DIM
    kv_w = DSA_KV_HEADS * DSA_HEAD_DIM
    qi_w = DSA_IDX_HEADS * DSA_IDX_DIM
    n_in = w_in.shape[1]
    assert n_in == q_w + 2 * kv_w + qi_w + DSA_IDX_DIM + DSA_IDX_HEADS
    n_pad = q_w + 2 * kv_w + qi_w + LANES
    w = jnp.pad(w_in.astype(BF16), ((0, 0), (0, n_pad - n_in)))
    assert s % tm == 0 and DSA_IDX_DIM + DSA_IDX_HEADS <= LANES and DSA_IDX_DIM % 8 == 0
    nq = s // tm
    half = DSA_HEAD_DIM // 8
    inv_freq = ROPE_THETA ** (-jnp.arange(half, dtype=F32) / half)
    lane = jnp.arange(LANES)
    j = lane % DSA_HEAD_DIM
    invf = jnp.where(j < 2 * half, inv_freq[j % half], 0.0)[None, :]
    kin = jnp.concatenate([kidx_norm, jnp.ones((LANES - DSA_IDX_DIM,), F32)])[None, :]

    def tok(width, dtype):
        return pl.BlockSpec((1, tm, width), lambda i, t: (i, t, 0)), jax.ShapeDtypeStruct((b, s, width), dtype)

    def per_tile(shape, dtype):
        nd = len(shape)
        return (pl.BlockSpec((1, 1) + shape, lambda i, t: (i, t) + (0,) * nd),
                jax.ShapeDtypeStruct((b, nq) + shape, dtype))

    outs = [
        per_tile((DSA_HEADS, LANES, tm), BF16),
        tok(kv_w, BF16),
        (pl.BlockSpec((1, DSA_KV_HEADS, 1, DSA_HEAD_DIM, tm), lambda i, t: (i, 0, t, 0, 0)),
         jax.ShapeDtypeStruct((b, DSA_KV_HEADS, nq, DSA_HEAD_DIM, tm), BF16)),
        per_tile((DSA_IDX_HEADS, DSA_IDX_DIM, tm), BF16),
        tok(LANES, BF16),
        per_tile((DSA_IDX_HEADS, tm), F32),
    ]
    return pl.pallas_call(
        _dsa_prep_body,
        grid=(b, s // tm),
        in_specs=[
            pl.BlockSpec((1, tm, d), lambda i, t: (i, t, 0)),
            _const_spec((1, d)),
            _const_spec((d, n_pad)),
            pl.BlockSpec((1, tm, 1), lambda i, t: (i, t, 0)),
            _const_spec((1, LANES)),
            _const_spec((1, LANES)),
            _const_spec((1, LANES)),
            _const_spec((1, LANES)),
        ],
        out_specs=[o[0] for o in outs],
        out_shape=[o[1] for o in outs],
        compiler_params=_params("arbitrary", "arbitrary"),
        name="dsa_prep",
    )(x, gain[None, :], w, positions[:, :, None], invf,
      jnp.tile(q_norm, 2)[None, :], jnp.tile(k_norm, 2)[None, :], kin)


def _dsa_attn_body(bound_ref, qt_ref, k_ref, vt_ref, qit_ref, kiw_ref, wit_ref, x_ref, wo_ref, o_ref,
                   key_s, top_s, mask_s, m_s, l_s, acc_s, s_s, o_s, *, topk):
    qb = qt_ref.shape[4]
    tk = qb
    i = pl.program_id(1)
    n_tiles = i + 1
    rep = DSA_HEADS // DSA_KV_HEADS
    krow = lax.broadcasted_iota(I32, (tk, qb), 0)
    qcol = lax.broadcasted_iota(I32, (tk, qb), 1)
    diag_ok = (krow // CHUNK) <= (qcol // CHUNK)

    def score_tile(t):
        kt = kiw_ref[0, pl.ds(pl.multiple_of(t * tk, tk), tk), 0:DSA_IDX_DIM]
        score = jnp.zeros((tk, qb), F32)
        for h in range(DSA_IDX_HEADS):
            score = score + wit_ref[0, 0, h:h + 1, :] * jnp.maximum(_dot(kt, qit_ref[0, 0, h]), 0.0)
        score = jnp.where(jnp.abs(score) < F32_MIN_NORMAL, 0.0, score)
        bits = pltpu.bitcast(score, I32)
        key = bits ^ ((bits >> 31) & jnp.int32(0x7FFFFFFF))
        admissible = (t < i) | diag_ok
        key_s[t] = jnp.where(admissible, key, jnp.int32(INT_MIN))
        top = pltpu.bitcast(bits & jnp.int32(-(1 << 16)), F32)
        top_s[t] = jnp.where(admissible, top, -jnp.inf).astype(BF16)

    def score_tile_pair(pair, carry):
        score_tile(2 * pair)
        score_tile(jnp.minimum(2 * pair + 1, n_tiles - 1))
        return carry

    n_pairs = (n_tiles + 1) // 2
    lax.fori_loop(0, n_pairs, score_tile_pair, 0)

    @pl.when(n_tiles % 2 == 1)
    def _():
        key_s[n_tiles] = jnp.full((tk, qb), INT_MIN, I32)
        top_s[n_tiles] = jnp.full((tk, qb), -jnp.inf, BF16)

    def count(pred_fn):
        def body(pair, acc):
            for t in (2 * pair, 2 * pair + 1):
                hit = jnp.where(pred_fn(key_s[t]), 1, 0).astype(I32)
                acc = acc + jnp.sum(hit.reshape(tk // 8, 8, qb), axis=0)
            return acc
        acc = lax.fori_loop(0, n_pairs, body, jnp.zeros((8, qb), I32))
        return jnp.sum(acc, axis=0, keepdims=True)

    pack = 16

    def count_top(cand_b):
        def body(pair, acc):
            for t in (2 * pair, 2 * pair + 1):
                hit = jnp.where(top_s[t] >= cand_b, jnp.ones((), BF16), jnp.zeros((), BF16))
                parts = [hit[r * pack:(r + 1) * pack] for r in range(tk // pack)]
                while len(parts) > 1:
                    parts = [a + b for a, b in zip(parts[0::2], parts[1::2])]
                acc = acc + parts[0]
            return acc
        acc = lax.fori_loop(0, n_pairs, body, jnp.zeros((pack, qb), BF16))
        return jnp.sum(acc.astype(F32), axis=0, keepdims=True)

    key16_neg_inf, key16_pos_inf = -32641, 32640
    key16_min_normal = 128

    def radix_top_step(b, carry):
        thr16, n_ge = carry
        cand = thr16 + lax.shift_left(jnp.int32(1), 15 - b)
        c = jnp.clip(cand, key16_neg_inf, key16_pos_inf)
        c = jnp.where((c >= 1) & (c < key16_min_normal), key16_min_normal, c)
        c = jnp.where((c <= -2) & (c >= -key16_min_normal), -1, c)
        pattern = c ^ ((c >> 15) & jnp.int32(0x7FFF))
        cand_b = pltpu.bitcast(lax.shift_left(pattern, 16), F32).astype(BF16)
        cnt = count_top(cand_b).astype(I32)
        accept = cnt >= topk
        return jnp.where(accept, cand, thr16), jnp.where(accept, cnt, n_ge)

    thr16, n_ge = lax.fori_loop(0, 16, radix_top_step,
                                (jnp.full((1, qb), -(1 << 15), I32), jnp.full((1, qb), n_pairs * 2 * tk, I32)))
    n_ge = jnp.where(thr16 <= key16_neg_inf, 0, n_ge)

    def radix_step(b, carry):
        thr, n_ge = carry
        cand = thr + lax.shift_left(jnp.int32(1), 31 - b)
        cnt = count(lambda key: key >= cand)
        accept = cnt >= topk
        return jnp.where(accept, cand, thr), jnp.where(accept, cnt, n_ge)

    carry = lax.fori_loop(16, RADIX_CHECK_BITS[0], radix_step, (lax.shift_left(thr16, 16), n_ge))
    for lo, hi in zip(RADIX_CHECK_BITS, RADIX_CHECK_BITS[1:] + (32,)):
        carry = lax.cond(jnp.max(carry[1]) > topk,
                         functools.partial(lax.fori_loop, lo, hi, radix_step), lambda c: c, carry)
    thr, n_ge = carry
    has_cut_tie = jnp.max(n_ge) > topk

    @pl.when(jnp.logical_not(has_cut_tie))
    def _():
        def mask_tile(t, carry):
            key = key_s[t]
            mask_s[t] = jnp.where((key >= thr) & (key != jnp.int32(INT_MIN)), 0.0, NEG_BIG)
            return carry
        lax.fori_loop(0, n_tiles, mask_tile, 0)

    @pl.when(has_cut_tie)
    def _():
        n_gt = count(lambda key: key > thr)
        need = (topk - n_gt).astype(F32)
        strict_lower = (qcol < krow).astype(BF16)
        ones = jnp.ones((tk, tk), BF16)

        def mask_tile(t, before):
            key = key_s[t]
            eq = key == thr
            eqb = jnp.where(eq, 1.0, 0.0).astype(BF16)
            rank = before + _dot(strict_lower, eqb)
            sel = (key > thr) | (eq & (rank < need))
            sel = sel & (key != jnp.int32(INT_MIN))
            mask_s[t] = jnp.where(sel, 0.0, NEG_BIG)
            return before + _dot(ones, eqb)

        lax.fori_loop(0, n_tiles, mask_tile, jnp.zeros((tk, qb), F32))

    l_s[...] = jnp.zeros(l_s.shape, F32)
    acc_s[...] = jnp.zeros(acc_s.shape, F32)
    heads_per_block = LANES // DSA_HEAD_DIM

    def logits(t, h, mask):
        kblk = (h // rep) // heads_per_block
        start = t * tk if isinstance(t, int) else pl.multiple_of(t * tk, tk)
        kt = k_ref[0, pl.ds(start, tk), kblk * LANES:(kblk + 1) * LANES]
        return _dot(kt, qt_ref[0, 0, h]) + mask

    def exp_sum_pv(use_offset):
        n_slots = s_s.shape[0]
        ahead = n_slots - 1
        assert DSA_HEADS % n_slots == 0
        for h in range(ahead):
            s_s[h] = logits(0, h, mask_s[0])

        def attn_tile(t, carry):
            t_next = jnp.minimum(t + 1, n_tiles - 1)
            mask, mask_next = mask_s[t], mask_s[t_next]
            for h in range(DSA_HEADS):
                nxt = h + ahead
                if nxt < DSA_HEADS:
                    s_s[nxt % n_slots] = logits(t, nxt, mask)
                else:
                    s_s[nxt % n_slots] = logits(t_next, nxt - DSA_HEADS, mask_next)
                s = s_s[h % n_slots]
                p = jnp.exp(s - m_s[h, 0:1, :] if use_offset else s)
                l_s[h] = l_s[h] + jnp.sum(p.reshape(tk // 8, 8, qb), axis=0)
                acc_s[h] = acc_s[h] + _dot(vt_ref[0, h // rep, t], p.astype(BF16))
            return carry

        lax.fori_loop(0, n_tiles, attn_tile, 0)

    bounded = bound_ref[0] <= DSA_SAFE_LOGIT

    @pl.when(bounded)
    def _():
        exp_sum_pv(use_offset=False)

    @pl.when(jnp.logical_not(bounded))
    def _():
        m_s[...] = jnp.full(m_s.shape, NEG_BIG, F32)

        def max_tile(t, carry):
            mask = mask_s[t]
            for h in range(DSA_HEADS):
                s = logits(t, h, mask)
                m_s[h] = jnp.maximum(m_s[h], jnp.max(s.reshape(tk // 8, 8, qb), axis=0))
            return carry

        lax.fori_loop(0, n_tiles, max_tile, 0)
        for h in range(DSA_HEADS):
            m_s[h] = jnp.broadcast_to(jnp.max(m_s[h], axis=0, keepdims=True), (8, qb))
        exp_sum_pv(use_offset=True)

    for c in range(DSA_HEADS // heads_per_block):
        pair = []
        for h in range(c * heads_per_block, (c + 1) * heads_per_block):
            pair.append(acc_s[h] / jnp.sum(l_s[h], axis=0, keepdims=True))
        o_s[:, c * LANES:(c + 1) * LANES] = jnp.concatenate(pair, axis=0).T.astype(BF16)
    o_ref[0] = x_ref[0] + _dot(o_s[...], wo_ref[...])


def _dsa_attn(bound, qt, k, vt, qit, kiw, wit, x, w_out):
    b, nq, n_heads, _, qb = qt.shape
    _, s, d = x.shape
    kvh, dh = vt.shape[1], vt.shape[3]
    topk = min(DSA_TOPK_MAX, s // 4)
    n_key_tiles = nq + nq % 2
    assert (qb // 16) * n_key_tiles <= 256, "packed bf16 hit counts must stay exact"
    return pl.pallas_call(
        functools.partial(_dsa_attn_body, topk=topk),
        grid=(b, nq),
        in_specs=[
            pl.BlockSpec(memory_space=pltpu.SMEM),
            pl.BlockSpec((1, 1, n_heads, LANES, qb), lambda i, j: (i, j, 0, 0, 0)),
            pl.BlockSpec((1, s, kvh * dh), lambda i, j: (i, 0, 0)),
            pl.BlockSpec((1, kvh, nq, dh, qb), lambda i, j: (i, 0, 0, 0, 0)),
            pl.BlockSpec((1, 1, DSA_IDX_HEADS, DSA_IDX_DIM, qb), lambda i, j: (i, j, 0, 0, 0)),
            pl.BlockSpec((1, s, LANES), lambda i, j: (i, 0, 0)),
            pl.BlockSpec((1, 1, DSA_IDX_HEADS, qb), lambda i, j: (i, j, 0, 0)),
            pl.BlockSpec((1, qb, d), lambda i, j: (i, j, 0)),
            _const_spec((n_heads * dh, d)),
        ],
        out_specs=pl.BlockSpec((1, qb, d), lambda i, j: (i, j, 0)),
        out_shape=jax.ShapeDtypeStruct(x.shape, F32),
        scratch_shapes=[
            pltpu.VMEM((n_key_tiles, qb, qb), I32),
            pltpu.VMEM((n_key_tiles, qb, qb), BF16),
            pltpu.VMEM((nq, qb, qb), F32),
            pltpu.VMEM((n_heads, 8, qb), F32),
            pltpu.VMEM((n_heads, 8, qb), F32),
            pltpu.VMEM((n_heads, dh, qb), F32),
            pltpu.VMEM((8, qb, qb), F32),
            pltpu.VMEM((qb, n_heads * dh), BF16),
        ],
        compiler_params=_params("arbitrary", "arbitrary"),
        name="dsa_attn",
    )(bound, qt, k, vt, qit, kiw, wit, x, w_out.astype(BF16))


def _dsa_mixer(x, positions, gain, w_in, q_norm, k_norm, kidx_norm, w_out, qb=DSA_QTILE):
    s = x.shape[1]
    qb = _row_tile(s, qb)
    qt, k, vt, qit, kiw, wit = _dsa_prep(x, gain, w_in, positions, q_norm, k_norm, kidx_norm, qb)
    bound = (DSA_HEAD_DIM ** 0.5) * jnp.max(jnp.abs(q_norm)) * jnp.max(jnp.abs(k_norm)) * (1.0 + 2.0 ** -6)
    return _dsa_attn(bound.reshape(1), qt, k, vt, qit, kiw, wit, x, w_out)


def kernel(x, mem, positions, norm_mix, gm_w_in, gm_v_norm, gm_w_s, gm_b_s, gm_w_out, ssd_w_in, ssd_conv_w, ssd_conv_b, ssd_dt_bias, ssd_a_log, ssd_d, ssd_out_norm, ssd_w_out, dsa_w_in, dsa_q_norm, dsa_k_norm, dsa_kidx_norm, dsa_w_out, norm_xa, norm_mem, xa_w_q, xa_w_kv, xa_q_norm, xa_k_norm, xa_w_out, norm_ffn, ffn_w_up, ffn_w_down):
    b, s, d = x.shape
    depth = norm_mix.shape[0]
    mem_k, mem_v = _mem_kv(mem, norm_mem, xa_w_kv, xa_k_norm)
    for i in range(depth):
        kind, j = i % 3, i // 3
        if kind == 0:
            x = _gmlp(x.reshape(b * s, d), norm_mix[i], gm_w_in[j], gm_v_norm[j], gm_w_s[j], gm_b_s[j],
                      gm_w_out[j]).reshape(b, s, d)
        elif kind == 1:
            x = _ssd_mixer(x, norm_mix[i], ssd_w_in[j], ssd_conv_w[j], ssd_conv_b[j], ssd_dt_bias[j],
                           ssd_a_log[j], ssd_d[j], ssd_out_norm[j], ssd_w_out[j])
        else:
            x = _dsa_mixer(x, positions, norm_mix[i], dsa_w_in[j], dsa_q_norm[j], dsa_k_norm[j],
                           dsa_kidx_norm[j], dsa_w_out[j])
        x = _xattn_ffn(x, norm_xa[i], xa_w_q[i], xa_q_norm[i], mem_k[i], mem_v[i], xa_w_out[i],
                       norm_ffn[i], ffn_w_up[i], ffn_w_down[i])
    return x
```

```python
import functools

import jax
import jax.numpy as jnp
from jax import lax
from jax.experimental import pallas as pl
from jax.experimental.pallas import tpu as pltpu

F32 = jnp.float32
BF16 = jnp.bfloat16
I32 = jnp.int32

EPS = 1e-6
ROPE_THETA = 500000.0
CHUNK = 64
GM_BLOCK = 128
GM_GROUPS = 8
SSD_HEAD_DIM = 64
SSD_GROUPS = 8
SSD_STATE = 128
SSD_CONV = 4
SSD_CHUNK = 128
DSA_HEADS = 16
DSA_KV_HEADS = 4
DSA_HEAD_DIM = 64
DSA_IDX_HEADS = 8
DSA_IDX_DIM = 64
DSA_TOPK_MAX = 256
DSA_QTILE = 256
DSA_SAFE_LOGIT = 60.0
RADIX_CHECK_BITS = (25, 28, 30)
XA_HEADS = 4
XA_HEAD_DIM = 128

LANES = 128
MXU_WIDTH = 256
VMEM_LIMIT_BYTES = 56 * 1024 * 1024
NEG_BIG = -1e30
INT_MIN = -(2 ** 31)
F32_MIN_NORMAL = 2.0 ** -126


def _params(*semantics):
    return pltpu.CompilerParams(dimension_semantics=semantics, vmem_limit_bytes=VMEM_LIMIT_BYTES)


def _const_spec(shape):
    nd = len(shape)
    return pl.BlockSpec(shape, lambda *_: (0,) * nd, pipeline_mode=pl.Buffered(1))


def _row_tile(n, want):
    t = min(n, want)
    assert n % t == 0, (n, t)
    return t


def _rms(x, gain):
    ms = jnp.mean(x * x, axis=-1, keepdims=True)
    return x * lax.rsqrt(ms + EPS) * gain


def _dot(a, b):
    return jnp.dot(a, b, preferred_element_type=F32)


def _dot_nt(a, b):
    return lax.dot_general(a, b, (((1,), (1,)), ((), ())), preferred_element_type=F32)


def _dot_tn(a, b):
    return lax.dot_general(a, b, (((0,), (0,)), ((), ())), preferred_element_type=F32)


def _split3(a):
    hi = a.astype(BF16)
    r1 = a - hi.astype(F32)
    mid = r1.astype(BF16)
    lo = (r1 - mid.astype(F32)).astype(BF16)
    return hi, mid, lo


def _dot_f32_lhs(a, b_exact):
    hi, mid, lo = _split3(a)
    return _dot(hi, b_exact) + _dot(mid, b_exact) + _dot(lo, b_exact)


def _dot_f32_rhs(a_exact, b):
    hi, mid, lo = _split3(b)
    return _dot(a_exact, hi) + _dot(a_exact, mid) + _dot(a_exact, lo)


def _mem_kv_body(mem_ref, g_ref, w_ref, kn_ref, k_ref, v_ref):
    mn = _rms(mem_ref[0], g_ref[0]).astype(BF16)
    kv = _dot(mn, w_ref[0])
    xa_w = XA_HEADS * XA_HEAD_DIM
    for h in range(XA_HEADS):
        cols = slice(h * XA_HEAD_DIM, (h + 1) * XA_HEAD_DIM)
        k_ref[0, 0, :, cols] = _rms(kv[:, cols], kn_ref[0]).astype(BF16)
    v_ref[0, 0] = kv[:, xa_w:].astype(BF16)


def _mem_kv(mem, norm_mem, w_kv, k_norm):
    depth, d, _ = w_kv.shape
    b, m, _ = mem.shape
    xa_w = XA_HEADS * XA_HEAD_DIM
    out = jax.ShapeDtypeStruct((depth, b, m, xa_w), BF16)
    return pl.pallas_call(
        _mem_kv_body,
        grid=(depth, b),
        in_specs=[
            pl.BlockSpec((1, m, d), lambda i, j: (j, 0, 0)),
            pl.BlockSpec((1, 1, d), lambda i, j: (i, 0, 0)),
            pl.BlockSpec((1, d, 2 * xa_w), lambda i, j: (i, 0, 0)),
            pl.BlockSpec((1, 1, XA_HEAD_DIM), lambda i, j: (i, 0, 0)),
        ],
        out_specs=[pl.BlockSpec((1, 1, m, xa_w), lambda i, j: (i, j, 0, 0))] * 2,
        out_shape=[out, out],
        compiler_params=_params("arbitrary", "arbitrary"),
        name="mem_kv",
    )(mem, norm_mem[:, None, :], w_kv.astype(BF16), k_norm[:, None, :])


def _xattn_ffn_body(x_ref, g_ref, wq_ref, qn_ref, k_ref, v_ref, wo_ref, gf_ref, wu_ref, wd_ref, o_ref,
                    *, hid_chunk):
    x = x_ref[0]
    xn = _rms(x, g_ref[...]).astype(BF16)
    q = _dot(xn, wq_ref[...])
    scale = XA_HEAD_DIM ** -0.5
    heads = []
    for h in range(XA_HEADS):
        cols = slice(h * XA_HEAD_DIM, (h + 1) * XA_HEAD_DIM)
        qh = _rms(q[:, cols], qn_ref[...]).astype(BF16)
        s = _dot_nt(qh, k_ref[0, :, cols]) * scale
        p = jnp.exp(s - jnp.max(s, axis=-1, keepdims=True))
        l = jnp.sum(p, axis=-1, keepdims=True)
        oh = _dot(p.astype(BF16), v_ref[0, :, cols]) / l
        heads.append(oh.astype(BF16))
    o = jnp.concatenate(heads, axis=-1)
    x = x + _dot(o, wo_ref[...])
    xn = _rms(x, gf_ref[...]).astype(BF16)
    acc = x
    for c in range(0, wu_ref.shape[1], hid_chunk):
        h = _dot(xn, wu_ref[:, c:c + hid_chunk])
        h = jnp.square(jnp.maximum(h, 0.0)).astype(BF16)
        acc = acc + _dot(h, wd_ref[c:c + hid_chunk, :])
    o_ref[0] = acc


def _xattn_ffn(x, gain_xa, w_q, q_norm, k, v, w_out, gain_ffn, w_up, w_down, tm=512, hid_chunk=1024):
    b, s, d = x.shape
    m = k.shape[1]
    xa_w = XA_HEADS * XA_HEAD_DIM
    hid = w_up.shape[1]
    tm = _row_tile(s, tm)
    return pl.pallas_call(
        functools.partial(_xattn_ffn_body, hid_chunk=hid_chunk),
        grid=(b, s // tm),
        in_specs=[
            pl.BlockSpec((1, tm, d), lambda i, j: (i, j, 0)),
            _const_spec((1, d)),
            _const_spec((d, xa_w)),
            _const_spec((1, XA_HEAD_DIM)),
            pl.BlockSpec((1, m, xa_w), lambda i, j: (i, 0, 0)),
            pl.BlockSpec((1, m, xa_w), lambda i, j: (i, 0, 0)),
            _const_spec((xa_w, d)),
            _const_spec((1, d)),
            _const_spec((d, hid)),
            _const_spec((hid, d)),
        ],
        out_specs=pl.BlockSpec((1, tm, d), lambda i, j: (i, j, 0)),
        out_shape=jax.ShapeDtypeStruct(x.shape, F32),
        compiler_params=_params("arbitrary", "arbitrary"),
        name="xattn_ffn",
    )(x, gain_xa[None, :], w_q.astype(BF16), q_norm[None, :], k, v, w_out.astype(BF16),
      gain_ffn[None, :], w_up.astype(BF16), w_down.astype(BF16))


def _proj_res_body(x_ref, y_ref, w_ref, o_ref):
    o_ref[...] = x_ref[...] + _dot(y_ref[...], w_ref[...])


def _proj_res(x2, y2, w, tm=512):
    t, d = x2.shape
    k = y2.shape[1]
    tm = _row_tile(t, tm)
    return pl.pallas_call(
        _proj_res_body,
        grid=(t // tm,),
        in_specs=[
            pl.BlockSpec((tm, d), lambda i: (i, 0)),
            pl.BlockSpec((tm, k), lambda i: (i, 0)),
            _const_spec((k, d)),
        ],
        out_specs=pl.BlockSpec((tm, d), lambda i: (i, 0)),
        out_shape=jax.ShapeDtypeStruct(x2.shape, F32),
        compiler_params=_params("arbitrary"),
        name="proj_res",
    )(x2, y2, w.astype(BF16))


def _gmlp_body(x_ref, g_ref, win_ref, vn_ref, ws_ref, bs_ref, wout_ref, o_ref,
               xn_s, u_s, v_s, h_s, *, col_chunk):
    tm = x_ref.shape[0]
    hid = u_s.shape[1]
    gw = hid // GM_GROUPS
    x = x_ref[...]
    xn_s[...] = _rms(x, g_ref[...]).astype(BF16)
    for c in range(0, hid, col_chunk):
        u_s[:, c:c + col_chunk] = jax.nn.gelu(_dot(xn_s[...], win_ref[:, c:c + col_chunk]))
    ssq = jnp.zeros((tm, 1), F32)
    for c in range(0, hid, col_chunk):
        vc = jax.nn.gelu(_dot(xn_s[...], win_ref[:, hid + c:hid + c + col_chunk]))
        v_s[:, c:c + col_chunk] = vc
        ssq = ssq + jnp.sum(vc * vc, axis=-1, keepdims=True)
    inv = lax.rsqrt(ssq / hid + EPS)
    t_i = lax.broadcasted_iota(I32, (GM_BLOCK, GM_BLOCK), 0)
    s_i = lax.broadcasted_iota(I32, (GM_BLOCK, GM_BLOCK), 1)
    causal = (s_i // CHUNK) <= (t_i // CHUNK)
    for g in range(GM_GROUPS):
        cols = slice(g * gw, (g + 1) * gw)
        wsg = jnp.where(causal, ws_ref[g], 0.0).astype(BF16)
        bias = bs_ref[:, g:g + 1]
        for n in range(tm // GM_BLOCK):
            rows = slice(n * GM_BLOCK, (n + 1) * GM_BLOCK)
            vg = (v_s[rows, cols] * inv[rows] * vn_ref[:, cols]).astype(BF16)
            sg = _dot(wsg, vg) + bias
            h_s[rows, cols] = (u_s[rows, cols] * sg).astype(BF16)
    o_ref[...] = x + _dot(h_s[...], wout_ref[...])


def _gmlp(x2, gain, w_in, v_norm, w_s, b_s, w_out, tm=512, col_chunk=1024):
    t, d = x2.shape
    hid = w_out.shape[0]
    tm = _row_tile(t, tm)
    assert tm % GM_BLOCK == 0
    return pl.pallas_call(
        functools.partial(_gmlp_body, col_chunk=col_chunk),
        grid=(t // tm,),
        in_specs=[
            pl.BlockSpec((tm, d), lambda i: (i, 0)),
            _const_spec((1, d)),
            _const_spec((d, 2 * hid)),
            _const_spec((1, hid)),
            _const_spec((GM_GROUPS, GM_BLOCK, GM_BLOCK)),
            _const_spec((GM_BLOCK, GM_GROUPS)),
            _const_spec((hid, d)),
        ],
        out_specs=pl.BlockSpec((tm, d), lambda i: (i, 0)),
        out_shape=jax.ShapeDtypeStruct(x2.shape, F32),
        scratch_shapes=[
            pltpu.VMEM((tm, d), BF16),
            pltpu.VMEM((tm, hid), F32),
            pltpu.VMEM((tm, hid), F32),
            pltpu.VMEM((tm, hid), BF16),
        ],
        compiler_params=_params("arbitrary"),
        name="gmlp",
    )(x2, gain[None, :], w_in.astype(BF16), v_norm[None, :], w_s, b_s.T, w_out.astype(BF16))


def _ssd_in_body(x_ref, g_ref, wz_ref, wx_ref, wdt_ref, z_ref, xbc_ref, dt_ref, *, col_chunk):
    xn = _rms(x_ref[...], g_ref[...]).astype(BF16)
    for c in range(0, wz_ref.shape[1], col_chunk):
        z_ref[:, c:c + col_chunk] = _dot(xn, wz_ref[:, c:c + col_chunk]).astype(BF16)
    for c in range(0, wx_ref.shape[1], col_chunk):
        xbc_ref[:, c:c + col_chunk] = _dot(xn, wx_ref[:, c:c + col_chunk]).astype(BF16)
    dt_ref[...] = _dot(xn, wdt_ref[...])


def _ssd_in(x2, gain, w_z, w_xbc, w_dt, tm=512, col_chunk=1024):
    t, d = x2.shape
    tm = _row_tile(t, tm)
    nz, nx, ndt = w_z.shape[1], w_xbc.shape[1], w_dt.shape[1]
    return pl.pallas_call(
        functools.partial(_ssd_in_body, col_chunk=col_chunk),
        grid=(t // tm,),
        in_specs=[
            pl.BlockSpec((tm, d), lambda i: (i, 0)),
            _const_spec((1, d)),
            _const_spec((d, nz)),
            _const_spec((d, nx)),
            _const_spec((d, ndt)),
        ],
        out_specs=[
            pl.BlockSpec((tm, nz), lambda i: (i, 0)),
            pl.BlockSpec((tm, nx), lambda i: (i, 0)),
            pl.BlockSpec((tm, ndt), lambda i: (i, 0)),
        ],
        out_shape=[
            jax.ShapeDtypeStruct((t, nz), BF16),
            jax.ShapeDtypeStruct((t, nx), BF16),
            jax.ShapeDtypeStruct((t, ndt), F32),
        ],
        compiler_params=_params("arbitrary"),
        name="ssd_in",
    )(x2, gain[None, :], w_z, w_xbc, w_dt)


def _ssd_core_body(z_ref, xbc_ref, dt_ref, cw_ref, cb_ref, dtb_ref, alog_ref, dskip_ref, onorm_ref,
                   tri_ref, expand_ref, shift_ref, y_ref, state_s, tail_s, *, conv_chunk):
    q = SSD_CHUNK
    n_state = SSD_STATE
    d_inner = z_ref.shape[2]
    conv_ch = xbc_ref.shape[2]
    heads_per_group = d_inner // SSD_HEAD_DIM // SSD_GROUPS
    gw = heads_per_group * SSD_HEAD_DIM
    tail = tail_s.shape[0]

    @pl.when(pl.program_id(1) == 0)
    def _():
        state_s[...] = jnp.zeros_like(state_s)
        tail_s[...] = jnp.zeros_like(tail_s)

    xb = xbc_ref[0]
    x_ext = jnp.concatenate([tail_s[...], xb], axis=0)
    tail_s[...] = xb[q - tail:, :]
    pieces = []
    for c0 in range(0, conv_ch, conv_chunk):
        cols = slice(c0, c0 + conv_chunk)
        conv = cb_ref[:, cols] + cw_ref[SSD_CONV - 1:SSD_CONV, cols] * xb[:, cols].astype(F32)
        for k in range(SSD_CONV - 1):
            conv = conv + cw_ref[k:k + 1, cols] * _dot(shift_ref[k], x_ext[:, cols])
        pieces.append(conv * jax.nn.sigmoid(conv))
    xbc = jnp.concatenate(pieces, axis=1)
    xs = xbc[:, :d_inner]
    bm = xbc[:, d_inner:d_inner + SSD_GROUPS * n_state].astype(BF16)
    cm = xbc[:, d_inner + SSD_GROUPS * n_state:].astype(BF16)

    dt_raw = dt_ref[0] + dtb_ref[...]
    dt = jnp.maximum(dt_raw, 0.0) + jnp.log1p(jnp.exp(-jnp.abs(dt_raw)))
    a = dt * (-jnp.exp(alog_ref[...]))
    a_cum = _dot_f32_rhs(tri_ref[...], a)
    a_cum_t = a_cum.T
    expand = expand_ref[...]
    dt_e = _dot_f32_lhs(dt, expand)
    acum_e = _dot_f32_lhs(a_cum, expand)
    alast_e = acum_e[q - 1:q, :]
    xdt = xs * dt_e
    xw = (xdt * jnp.exp(alast_e - acum_e)).astype(BF16)
    xdt_b = xdt.astype(BF16)
    ea_e = jnp.exp(acum_e)
    chunk_decay_e = jnp.exp(alast_e)

    l_i = lax.broadcasted_iota(I32, (q, q), 0)
    s_i = lax.broadcasted_iota(I32, (q, q), 1)
    causal = l_i >= s_i
    lane_g = lax.broadcasted_iota(I32, (q, gw), 1) // SSD_HEAD_DIM

    zf = z_ref[0].astype(F32)
    for g in range(SSD_GROUPS):
        gcols = slice(g * gw, (g + 1) * gw)
        ncols = slice(g * n_state, (g + 1) * n_state)
        cg = cm[:, ncols]
        bg = bm[:, ncols]
        cb = _dot_nt(cg, bg)
        st = state_s[g]
        y_off = _dot(cg, st.astype(BF16)) * ea_e[:, gcols]
        ms = []
        xblk = []
        xg = xdt_b[:, gcols]
        for r in range(heads_per_group):
            h = g * heads_per_group + r
            seg = a_cum[:, h:h + 1] - a_cum_t[h:h + 1, :]
            decay = jnp.exp(jnp.where(causal, seg, -jnp.inf))
            ms.append((decay * cb).astype(BF16))
            xblk.append(jnp.where(lane_g == r, xg, jnp.zeros_like(xg)))
        y_diag = _dot(jnp.concatenate(ms, axis=1), jnp.concatenate(xblk, axis=0))
        y = y_diag + y_off + xs[:, gcols] * dskip_ref[:, gcols]
        state_s[g] = st * chunk_decay_e[:, gcols] + _dot_tn(bg, xw[:, gcols])
        zg = zf[:, gcols]
        gated = y * (zg * jax.nn.sigmoid(zg))
        gated = gated * lax.rsqrt(jnp.mean(gated * gated, axis=-1, keepdims=True) + EPS)
        y_ref[0, :, gcols] = (gated * onorm_ref[:, gcols]).astype(BF16)


def _ssd_core(z, xbc, dt, conv_w, conv_b, dt_bias, a_log, d_skip, out_norm, conv_chunk=256):
    b, s, d_inner = z.shape
    conv_ch = xbc.shape[2]
    n_heads = d_inner // SSD_HEAD_DIM
    gw = d_inner // SSD_GROUPS
    q = SSD_CHUNK
    tail = 16
    assert s % q == 0 and n_heads <= LANES and dt.shape[2] == LANES and SSD_CONV - 1 <= tail

    def pad_heads(v):
        return jnp.pad(v, (0, LANES - n_heads))[None, :]

    tri = (jnp.arange(q)[:, None] >= jnp.arange(q)[None, :]).astype(BF16)
    expand = (jnp.arange(LANES)[:, None] == (jnp.arange(d_inner) // SSD_HEAD_DIM)[None, :]).astype(BF16)
    src = tail + jnp.arange(q)[None, :, None] - (SSD_CONV - 1 - jnp.arange(SSD_CONV - 1))[:, None, None]
    shift = (jnp.arange(tail + q)[None, None, :] == src).astype(BF16)
    return pl.pallas_call(
        functools.partial(_ssd_core_body, conv_chunk=conv_chunk),
        grid=(b, s // q),
        in_specs=[
            pl.BlockSpec((1, q, d_inner), lambda i, j: (i, j, 0)),
            pl.BlockSpec((1, q, conv_ch), lambda i, j: (i, j, 0)),
            pl.BlockSpec((1, q, LANES), lambda i, j: (i, j, 0)),
            _const_spec((SSD_CONV, conv_ch)),
            _const_spec((1, conv_ch)),
            _const_spec((1, LANES)),
            _const_spec((1, LANES)),
            _const_spec((1, d_inner)),
            _const_spec((1, d_inner)),
            _const_spec((q, q)),
            _const_spec((LANES, d_inner)),
            _const_spec((SSD_CONV - 1, q, tail + q)),
        ],
        out_specs=pl.BlockSpec((1, q, d_inner), lambda i, j: (i, j, 0)),
        out_shape=jax.ShapeDtypeStruct((b, s, d_inner), BF16),
        scratch_shapes=[
            pltpu.VMEM((SSD_GROUPS, SSD_STATE, gw), F32),
            pltpu.VMEM((tail, conv_ch), BF16),
        ],
        compiler_params=_params("arbitrary", "arbitrary"),
        name="ssd_core",
    )(z, xbc, dt, conv_w, conv_b[None, :], pad_heads(dt_bias), pad_heads(a_log),
      jnp.repeat(d_skip, SSD_HEAD_DIM)[None, :], out_norm[None, :], tri, expand, shift)


def _ssd_mixer(x, gain, w_in, conv_w, conv_b, dt_bias, a_log, d_skip, out_norm, w_out):
    b, s, d = x.shape
    d_inner = w_out.shape[0]
    conv_ch = conv_w.shape[1]
    n_heads = d_inner // SSD_HEAD_DIM
    w_in = w_in.astype(BF16)
    w_z = w_in[:, :d_inner]
    w_xbc = w_in[:, d_inner:d_inner + conv_ch]
    w_dt = jnp.pad(w_in[:, d_inner + conv_ch:], ((0, 0), (0, LANES - n_heads)))
    x2 = x.reshape(b * s, d)
    z, xbc, dt = _ssd_in(x2, gain, w_z, w_xbc, w_dt)
    y = _ssd_core(z.reshape(b, s, d_inner), xbc.reshape(b, s, conv_ch), dt.reshape(b, s, LANES),
                  conv_w, conv_b, dt_bias, a_log, d_skip, out_norm)
    return _proj_res(x2, y.reshape(b * s, d_inner), w_out).reshape(b, s, d)


def _halves_rms(x, gain, seg):
    x2 = x * x
    hi = x2.astype(BF16)
    lo = (x2 - hi.astype(F32)).astype(BF16)
    ms = (_dot(hi, seg) + _dot(lo, seg)) * (2.0 / LANES)
    return x * lax.rsqrt(ms + EPS) * gain


def _rope(x, cos, sin_lo, sin_hi):
    half = DSA_HEAD_DIM // 8
    return x * cos + pltpu.roll(x, LANES - half, 1) * sin_lo + pltpu.roll(x, half, 1) * sin_hi


def _dsa_prep_body(x_ref, g_ref, w_ref, pos_ref, invf_ref, qn_ref, kn_ref, kin_ref, seg_ref,
                   qt_ref, k_ref, vt_ref, qit_ref, kiw_ref, wit_ref):
    tm = x_ref.shape[1]
    rep = DSA_HEADS // DSA_KV_HEADS
    heads_per_block = LANES // DSA_HEAD_DIM
    q_w = DSA_HEADS * DSA_HEAD_DIM
    kv_w = DSA_KV_HEADS * DSA_HEAD_DIM
    qi_w = DSA_IDX_HEADS * DSA_IDX_DIM
    half = DSA_HEAD_DIM // 8
    xn = _rms(x_ref[0], g_ref[...]).astype(BF16)
    pos = pos_ref[0].astype(F32)
    lane = lax.broadcasted_iota(I32, (tm, LANES), 1)
    j = lane % DSA_HEAD_DIM

    ang = pos * invf_ref[...]
    cos, sin = jnp.cos(ang), jnp.sin(ang)
    sin_lo, sin_hi = jnp.where(j < half, -sin, 0.0), jnp.where(j >= half, sin, 0.0)

    def proj(c0, width):
        return _dot(xn, w_ref[:, c0:c0 + width])

    def proj_blocks(c0, width):
        blocks = []
        for c in range(c0, c0 + width, MXU_WIDTH):
            wide = proj(c, min(MXU_WIDTH, c0 + width - c))
            blocks += [wide[:, b:b + LANES] for b in range(0, wide.shape[1], LANES)]
        return blocks

    o1, o2, o3, o4 = q_w, q_w + kv_w, q_w + 2 * kv_w, q_w + 2 * kv_w + qi_w
    zeros = jnp.zeros((DSA_HEAD_DIM, tm), F32)
    for c, raw in enumerate(proj_blocks(0, q_w)):
        blk = _halves_rms(raw, qn_ref[...], seg_ref[...])
        bt = (_rope(blk, cos, sin_lo, sin_hi) * (DSA_HEAD_DIM ** -0.5)).T
        for e in range(heads_per_block):
            h = c * heads_per_block + e
            rows = bt[e * DSA_HEAD_DIM:(e + 1) * DSA_HEAD_DIM]
            first_half = (h // rep) % heads_per_block == 0
            padded = jnp.concatenate([rows, zeros] if first_half else [zeros, rows], axis=0)
            qt_ref[0, 0, h] = padded.astype(BF16)
    for c, raw in enumerate(proj_blocks(o1, kv_w)):
        blk = _halves_rms(raw, kn_ref[...], seg_ref[...])
        k_ref[0, :, c * LANES:(c + 1) * LANES] = _rope(blk, cos, sin_lo, sin_hi).astype(BF16)
    vt = proj(o2, kv_w).T
    for g in range(DSA_KV_HEADS):
        vt_ref[0, g, 0] = vt[g * DSA_HEAD_DIM:(g + 1) * DSA_HEAD_DIM].astype(BF16)
    for c, raw in enumerate(proj_blocks(o3, qi_w)):
        bt = _rope(raw, cos, sin_lo, sin_hi).T
        for e in range(heads_per_block):
            qit_ref[0, 0, c * heads_per_block + e] = bt[e * DSA_IDX_DIM:(e + 1) * DSA_IDX_DIM].astype(BF16)
    blk = proj(o4, LANES)
    is_ki = lane < DSA_IDX_DIM
    ms = jnp.sum(jnp.where(is_ki, blk * blk, 0.0), axis=-1, keepdims=True) * (1.0 / DSA_IDX_DIM)
    kin = blk * lax.rsqrt(ms + EPS) * kin_ref[...]
    kiw_ref[0] = _rope(kin, jnp.where(is_ki, cos, 1.0), jnp.where(is_ki, sin_lo, 0.0),
                       jnp.where(is_ki, sin_hi, 0.0)).astype(BF16)
    wt = (blk * (DSA_IDX_HEADS ** -0.5 * DSA_IDX_DIM ** -0.5)).T
    wit_ref[0, 0] = wt[DSA_IDX_DIM:DSA_IDX_DIM + DSA_IDX_HEADS]


def _dsa_prep(x, gain, w_in, positions, q_norm, k_norm, kidx_norm, tm):
    b, s, d = x.shape
    q_w = DSA_HEADS * DSA_HEAD_DIM
    kv_w = DSA_KV_HEADS * DSA_HEAD_DIM
    qi_w = DSA_IDX_HEADS * DSA_IDX_DIM
    n_in = w_in.shape[1]
    assert n_in == q_w + 2 * kv_w + qi_w + DSA_IDX_DIM + DSA_IDX_HEADS
    n_pad = q_w + 2 * kv_w + qi_w + LANES
    w = jnp.pad(w_in.astype(BF16), ((0, 0), (0, n_pad - n_in)))
    assert s % tm == 0 and DSA_IDX_DIM + DSA_IDX_HEADS <= LANES and DSA_IDX_DIM % 8 == 0
    nq = s // tm
    half = DSA_HEAD_DIM // 8
    inv_freq = ROPE_THETA ** (-jnp.arange(half, dtype=F32) / half)
    lane = jnp.arange(LANES)
    j = lane % DSA_HEAD_DIM
    invf = jnp.where(j < 2 * half, inv_freq[j % half], 0.0)[None, :]
    kin = jnp.concatenate([kidx_norm, jnp.ones((LANES - DSA_IDX_DIM,), F32)])[None, :]

    def tok(width, dtype):
        return pl.BlockSpec((1, tm, width), lambda i, t: (i, t, 0)), jax.ShapeDtypeStruct((b, s, width), dtype)

    def per_tile(shape, dtype):
        nd = len(shape)
        return (pl.BlockSpec((1, 1) + shape, lambda i, t: (i, t) + (0,) * nd),
                jax.ShapeDtypeStruct((b, nq) + shape, dtype))

    outs = [
        per_tile((DSA_HEADS, LANES, tm), BF16),
        tok(kv_w, BF16),
        (pl.BlockSpec((1, DSA_KV_HEADS, 1, DSA_HEAD_DIM, tm), lambda i, t: (i, 0, t, 0, 0)),
         jax.ShapeDtypeStruct((b, DSA_KV_HEADS, nq, DSA_HEAD_DIM, tm), BF16)),
        per_tile((DSA_IDX_HEADS, DSA_IDX_DIM, tm), BF16),
        tok(LANES, BF16),
        per_tile((DSA_IDX_HEADS, tm), F32),
    ]
    return pl.pallas_call(
        _dsa_prep_body,
        grid=(b, s // tm),
        in_specs=[
            pl.BlockSpec((1, tm, d), lambda i, t: (i, t, 0)),
            _const_spec((1, d)),
            _const_spec((d, n_pad)),
            pl.BlockSpec((1, tm, 1), lambda i, t: (i, t, 0)),
            _const_spec((1, LANES)),
            _const_spec((1, LANES)),
            _const_spec((1, LANES)),
            _const_spec((1, LANES)),
            _const_spec((LANES, LANES)),
        ],
        out_specs=[o[0] for o in outs],
        out_shape=[o[1] for o in outs],
        compiler_params=_params("arbitrary", "arbitrary"),
        name="dsa_prep",
    )(x, gain[None, :], w, positions[:, :, None], invf,
      jnp.tile(q_norm, 2)[None, :], jnp.tile(k_norm, 2)[None, :], kin,
      (lane[:, None] // DSA_HEAD_DIM == lane[None, :] // DSA_HEAD_DIM).astype(BF16))


def _dsa_attn_body(bound_ref, qt_ref, k_ref, vt_ref, qit_ref, kiw_ref, wit_ref, x_ref, wo_ref, o_ref,
                   key_s, top_s, mask_s, m_s, l_s, acc_s, s_s, o_s, *, topk):
    qb = qt_ref.shape[4]
    tk = qb
    i = pl.program_id(1)
    n_tiles = i + 1
    rep = DSA_HEADS // DSA_KV_HEADS
    krow = lax.broadcasted_iota(I32, (tk, qb), 0)
    qcol = lax.broadcasted_iota(I32, (tk, qb), 1)
    diag_ok = (krow // CHUNK) <= (qcol // CHUNK)

    def score_tile(t):
        kt = kiw_ref[0, pl.ds(pl.multiple_of(t * tk, tk), tk), 0:DSA_IDX_DIM]
        score = jnp.zeros((tk, qb), F32)
        for h in range(DSA_IDX_HEADS):
            score = score + wit_ref[0, 0, h:h + 1, :] * jnp.maximum(_dot(kt, qit_ref[0, 0, h]), 0.0)
        score = jnp.where(jnp.abs(score) < F32_MIN_NORMAL, 0.0, score)
        bits = pltpu.bitcast(score, I32)
        key = bits ^ ((bits >> 31) & jnp.int32(0x7FFFFFFF))
        admissible = (t < i) | diag_ok
        key_s[t] = jnp.where(admissible, key, jnp.int32(INT_MIN))
        top = pltpu.bitcast(bits & jnp.int32(-(1 << 16)), F32)
        top_s[t] = jnp.where(admissible, top, -jnp.inf).astype(BF16)

    def score_tile_pair(pair, carry):
        score_tile(2 * pair)
        score_tile(jnp.minimum(2 * pair + 1, n_tiles - 1))
        return carry

    n_pairs = (n_tiles + 1) // 2
    lax.fori_loop(0, n_pairs, score_tile_pair, 0)

    @pl.when(n_tiles % 2 == 1)
    def _():
        key_s[n_tiles] = jnp.full((tk, qb), INT_MIN, I32)
        top_s[n_tiles] = jnp.full((tk, qb), -jnp.inf, BF16)

    def count(pred_fn):
        def body(pair, acc):
            for t in (2 * pair, 2 * pair + 1):
                hit = jnp.where(pred_fn(key_s[t]), 1, 0).astype(I32)
                acc = acc + jnp.sum(hit.reshape(tk // 8, 8, qb), axis=0)
            return acc
        acc = lax.fori_loop(0, n_pairs, body, jnp.zeros((8, qb), I32))
        return jnp.sum(acc, axis=0, keepdims=True)

    pack = 16

    def count_top(cand_b):
        def body(pair, acc):
            for t in (2 * pair, 2 * pair + 1):
                hit = jnp.where(top_s[t] >= cand_b, jnp.ones((), BF16), jnp.zeros((), BF16))
                parts = [hit[r * pack:(r + 1) * pack] for r in range(tk // pack)]
                while len(parts) > 1:
                    parts = [a + b for a, b in zip(parts[0::2], parts[1::2])]
                acc = acc + parts[0]
            return acc
        acc = lax.fori_loop(0, n_pairs, body, jnp.zeros((pack, qb), BF16))
        return jnp.sum(acc.astype(F32), axis=0, keepdims=True)

    key16_neg_inf, key16_pos_inf = -32641, 32640
    key16_min_normal = 128

    def radix_top_step(b, carry):
        thr16, n_ge = carry
        cand = thr16 + lax.shift_left(jnp.int32(1), 15 - b)
        c = jnp.clip(cand, key16_neg_inf, key16_pos_inf)
        c = jnp.where((c >= 1) & (c < key16_min_normal), key16_min_normal, c)
        c = jnp.where((c <= -2) & (c >= -key16_min_normal), -1, c)
        pattern = c ^ ((c >> 15) & jnp.int32(0x7FFF))
        cand_b = pltpu.bitcast(lax.shift_left(pattern, 16), F32).astype(BF16)
        cnt = count_top(cand_b).astype(I32)
        accept = cnt >= topk
        return jnp.where(accept, cand, thr16), jnp.where(accept, cnt, n_ge)

    thr16, n_ge = lax.fori_loop(0, 16, radix_top_step,
                                (jnp.full((1, qb), -(1 << 15), I32), jnp.full((1, qb), n_pairs * 2 * tk, I32)))
    n_ge = jnp.where(thr16 <= key16_neg_inf, 0, n_ge)

    def radix_step(b, carry):
        thr, n_ge = carry
        cand = thr + lax.shift_left(jnp.int32(1), 31 - b)
        cnt = count(lambda key: key >= cand)
        accept = cnt >= topk
        return jnp.where(accept, cand, thr), jnp.where(accept, cnt, n_ge)

    carry = lax.fori_loop(16, RADIX_CHECK_BITS[0], radix_step, (lax.shift_left(thr16, 16), n_ge))
    for lo, hi in zip(RADIX_CHECK_BITS, RADIX_CHECK_BITS[1:] + (32,)):
        carry = lax.cond(jnp.max(carry[1]) > topk,
                         functools.partial(lax.fori_loop, lo, hi, radix_step), lambda c: c, carry)
    thr, n_ge = carry
    has_cut_tie = jnp.max(n_ge) > topk

    @pl.when(jnp.logical_not(has_cut_tie))
    def _():
        def mask_tile(t, carry):
            key = key_s[t]
            mask_s[t] = jnp.where((key >= thr) & (key != jnp.int32(INT_MIN)), 0.0, NEG_BIG)
            return carry
        lax.fori_loop(0, n_tiles, mask_tile, 0)

    @pl.when(has_cut_tie)
    def _():
        n_gt = count(lambda key: key > thr)
        need = (topk - n_gt).astype(F32)
        strict_lower = (qcol < krow).astype(BF16)
        ones = jnp.ones((tk, tk), BF16)

        def mask_tile(t, before):
            key = key_s[t]
            eq = key == thr
            eqb = jnp.where(eq, 1.0, 0.0).astype(BF16)
            rank = before + _dot(strict_lower, eqb)
            sel = (key > thr) | (eq & (rank < need))
            sel = sel & (key != jnp.int32(INT_MIN))
            mask_s[t] = jnp.where(sel, 0.0, NEG_BIG)
            return before + _dot(ones, eqb)

        lax.fori_loop(0, n_tiles, mask_tile, jnp.zeros((tk, qb), F32))

    l_s[...] = jnp.zeros(l_s.shape, F32)
    acc_s[...] = jnp.zeros(acc_s.shape, F32)
    heads_per_block = LANES // DSA_HEAD_DIM

    def logits(t, h, mask):
        kblk = (h // rep) // heads_per_block
        start = t * tk if isinstance(t, int) else pl.multiple_of(t * tk, tk)
        kt = k_ref[0, pl.ds(start, tk), kblk * LANES:(kblk + 1) * LANES]
        return _dot(kt, qt_ref[0, 0, h]) + mask

    def exp_sum_pv(use_offset):
        n_slots = s_s.shape[0]
        ahead = n_slots - 1
        assert DSA_HEADS % n_slots == 0
        for h in range(ahead):
            s_s[h] = logits(0, h, mask_s[0])

        def attn_tile(t, carry):
            t_next = jnp.minimum(t + 1, n_tiles - 1)
            mask, mask_next = mask_s[t], mask_s[t_next]
            for h in range(DSA_HEADS):
                nxt = h + ahead
                if nxt < DSA_HEADS:
                    s_s[nxt % n_slots] = logits(t, nxt, mask)
                else:
                    s_s[nxt % n_slots] = logits(t_next, nxt - DSA_HEADS, mask_next)
                s = s_s[h % n_slots]
                p = jnp.exp(s - m_s[h, 0:1, :] if use_offset else s)
                l_s[h] = l_s[h] + jnp.sum(p.reshape(tk // 8, 8, qb), axis=0)
                acc_s[h] = acc_s[h] + _dot(vt_ref[0, h // rep, t], p.astype(BF16))
            return carry

        lax.fori_loop(0, n_tiles, attn_tile, 0)

    bounded = bound_ref[0] <= DSA_SAFE_LOGIT

    @pl.when(bounded)
    def _():
        exp_sum_pv(use_offset=False)

    @pl.when(jnp.logical_not(bounded))
    def _():
        m_s[...] = jnp.full(m_s.shape, NEG_BIG, F32)

        def max_tile(t, carry):
            mask = mask_s[t]
            for h in range(DSA_HEADS):
                s = logits(t, h, mask)
                m_s[h] = jnp.maximum(m_s[h], jnp.max(s.reshape(tk // 8, 8, qb), axis=0))
            return carry

        lax.fori_loop(0, n_tiles, max_tile, 0)
        for h in range(DSA_HEADS):
            m_s[h] = jnp.broadcast_to(jnp.max(m_s[h], axis=0, keepdims=True), (8, qb))
        exp_sum_pv(use_offset=True)

    for c in range(DSA_HEADS // heads_per_block):
        pair = []
        for h in range(c * heads_per_block, (c + 1) * heads_per_block):
            pair.append(acc_s[h] / jnp.sum(l_s[h], axis=0, keepdims=True))
        o_s[:, c * LANES:(c + 1) * LANES] = jnp.concatenate(pair, axis=0).T.astype(BF16)
    o_ref[0] = x_ref[0] + _dot(o_s[...], wo_ref[...])


def _dsa_attn(bound, qt, k, vt, qit, kiw, wit, x, w_out):
    b, nq, n_heads, _, qb = qt.shape
    _, s, d = x.shape
    kvh, dh = vt.shape[1], vt.shape[3]
    topk = min(DSA_TOPK_MAX, s // 4)
    n_key_tiles = nq + nq % 2
    assert (qb // 16) * n_key_tiles <= 256, "packed bf16 hit counts must stay exact"
    return pl.pallas_call(
        functools.partial(_dsa_attn_body, topk=topk),
        grid=(b, nq),
        in_specs=[
            pl.BlockSpec(memory_space=pltpu.SMEM),
            pl.BlockSpec((1, 1, n_heads, LANES, qb), lambda i, j: (i, j, 0, 0, 0)),
            pl.BlockSpec((1, s, kvh * dh), lambda i, j: (i, 0, 0)),
            pl.BlockSpec((1, kvh, nq, dh, qb), lambda i, j: (i, 0, 0, 0, 0)),
            pl.BlockSpec((1, 1, DSA_IDX_HEADS, DSA_IDX_DIM, qb), lambda i, j: (i, j, 0, 0, 0)),
            pl.BlockSpec((1, s, LANES), lambda i, j: (i, 0, 0)),
            pl.BlockSpec((1, 1, DSA_IDX_HEADS, qb), lambda i, j: (i, j, 0, 0)),
            pl.BlockSpec((1, qb, d), lambda i, j: (i, j, 0)),
            _const_spec((n_heads * dh, d)),
        ],
        out_specs=pl.BlockSpec((1, qb, d), lambda i, j: (i, j, 0)),
        out_shape=jax.ShapeDtypeStruct(x.shape, F32),
        scratch_shapes=[
            pltpu.VMEM((n_key_tiles, qb, qb), I32),
            pltpu.VMEM((n_key_tiles, qb, qb), BF16),
            pltpu.VMEM((nq, qb, qb), F32),
            pltpu.VMEM((n_heads, 8, qb), F32),
            pltpu.VMEM((n_heads, 8, qb), F32),
            pltpu.VMEM((n_heads, dh, qb), F32),
            pltpu.VMEM((8, qb, qb), F32),
            pltpu.VMEM((qb, n_heads * dh), BF16),
        ],
        compiler_params=_params("arbitrary", "arbitrary"),
        name="dsa_attn",
    )(bound, qt, k, vt, qit, kiw, wit, x, w_out.astype(BF16))


def _dsa_mixer(x, positions, gain, w_in, q_norm, k_norm, kidx_norm, w_out, qb=DSA_QTILE):
    s = x.shape[1]
    qb = _row_tile(s, qb)
    qt, k, vt, qit, kiw, wit = _dsa_prep(x, gain, w_in, positions, q_norm, k_norm, kidx_norm, qb)
    bound = (DSA_HEAD_DIM ** 0.5) * jnp.max(jnp.abs(q_norm)) * jnp.max(jnp.abs(k_norm)) * (1.0 + 2.0 ** -6)
    return _dsa_attn(bound.reshape(1), qt, k, vt, qit, kiw, wit, x, w_out)


def kernel(x, mem, positions, norm_mix, gm_w_in, gm_v_norm, gm_w_s, gm_b_s, gm_w_out, ssd_w_in, ssd_conv_w, ssd_conv_b, ssd_dt_bias, ssd_a_log, ssd_d, ssd_out_norm, ssd_w_out, dsa_w_in, dsa_q_norm, dsa_k_norm, dsa_kidx_norm, dsa_w_out, norm_xa, norm_mem, xa_w_q, xa_w_kv, xa_q_norm, xa_k_norm, xa_w_out, norm_ffn, ffn_w_up, ffn_w_down):
    b, s, d = x.shape
    depth = norm_mix.shape[0]
    mem_k, mem_v = _mem_kv(mem, norm_mem, xa_w_kv, xa_k_norm)
    for i in range(depth):
        kind, j = i % 3, i // 3
        if kind == 0:
            x = _gmlp(x.reshape(b * s, d), norm_mix[i], gm_w_in[j], gm_v_norm[j], gm_w_s[j], gm_b_s[j],
                      gm_w_out[j]).reshape(b, s, d)
        elif kind == 1:
            x = _ssd_mixer(x, norm_mix[i], ssd_w_in[j], ssd_conv_w[j], ssd_conv_b[j], ssd_dt_bias[j],
                           ssd_a_log[j], ssd_d[j], ssd_out_norm[j], ssd_w_out[j])
        else:
            x = _dsa_mixer(x, positions, norm_mix[i], dsa_w_in[j], dsa_q_norm[j], dsa_k_norm[j],
                           dsa_kidx_norm[j], dsa_w_out[j])
        x = _xattn_ffn(x, norm_xa[i], xa_w_q[i], xa_q_norm[i], mem_k[i], mem_v[i], xa_w_out[i],
                       norm_ffn[i], ffn_w_up[i], ffn_w_down[i])
    return x
```
